```python
import jax
import jax.numpy as jnp
from jax import lax
import numpy as np

D_MODEL = 1024
BATCH = 8
SEQ = 4096
DEPTH = 2

CTX_LEN = 256
GRID_W = 64
EPS = 1e-6

MLA_HEADS = 4
MLA_Q_RANK = 256
MLA_KV_RANK = 128
MLA_NOPE = 64
MLA_ROPE = 32
MLA_V = 64
ROPE_BASE = 10000.0
Q_BLOCK = 128

NA_HEADS = 4
NA_DIM = 64
NA_KR = 8
NA_KC = 16

ML_HEADS = 4
ML_QK = 64
ML_V = 128
ML_CHUNK = 128

N_EXPERTS = 16
D_EXPERT = 1024
EC_FACTOR = 2

MLA_W = MLA_HEADS * MLA_V
NA_W = NA_HEADS * NA_DIM
ML_W = ML_HEADS * ML_V

IN_SIZES = (MLA_Q_RANK, MLA_KV_RANK, MLA_ROPE,
            NA_W, NA_W, NA_W,
            ML_HEADS * ML_QK, ML_HEADS * ML_QK, ML_W, ML_W, 4 * ML_HEADS,
            3 * D_MODEL)
D_IN = sum(IN_SIZES)

kernel_name = 'hybrid_mla_natten_mlstm_ecmoe_diffusion'


def rmsnorm(x, g):
    xf = x.astype(jnp.float32)
    xf = xf * lax.rsqrt(jnp.mean(jnp.square(xf), axis=-1, keepdims=True) + EPS)
    return (xf * g.astype(jnp.float32)).astype(x.dtype)


def split_in(z):
    offs = np.cumsum(IN_SIZES)[:-1].tolist()
    return jnp.split(z, offs, axis=-1)


def rope_2d(n):
    t = jnp.arange(n)
    pos = jnp.stack([t // GRID_W, t % GRID_W], axis=-1).astype(jnp.float32)
    nf = MLA_ROPE // 4
    inv = ROPE_BASE ** (-jnp.arange(nf, dtype=jnp.float32) / nf)
    ang = pos[..., None] * inv
    return jnp.cos(ang), jnp.sin(ang)


def apply_rope_2d(x, cos, sin):
    xs = x.reshape(x.shape[:-1] + (2, 2, MLA_ROPE // 4))
    x1, x2 = xs[..., 0, :], xs[..., 1, :]
    cos, sin = cos.astype(x.dtype), sin.astype(x.dtype)
    out = jnp.stack([x1 * cos - x2 * sin, x1 * sin + x2 * cos], axis=-2)
    return out.reshape(x.shape)


def mla_queries(cq, g_q, w_uq, cos, sin):
    q = jnp.einsum('bnr,rhe->bnhe', rmsnorm(cq, g_q), w_uq)
    q_nope, q_rope = q[..., :MLA_NOPE], q[..., MLA_NOPE:]
    if cos is not None:
        q_rope = apply_rope_2d(q_rope, cos[:, None], sin[:, None])
    return jnp.concatenate([q_nope, q_rope], axis=-1)


def mla_keys_values(ckv, kr, g_kv, w_ukv, cos, sin):
    kv = jnp.einsum('bnr,rhe->bnhe', rmsnorm(ckv, g_kv), w_ukv)
    k_nope, v = kv[..., :MLA_NOPE], kv[..., MLA_NOPE:]
    if cos is not None:
        kr = apply_rope_2d(kr, cos, sin)
    k_rope = jnp.broadcast_to(kr[:, :, None, :], k_nope.shape[:3] + (MLA_ROPE,))
    return jnp.concatenate([k_nope, k_rope], axis=-1), v


def block_attention(q, k, v):
    B, n, H, E = q.shape
    scale = E ** -0.5
    qb = jnp.moveaxis(q.reshape(B, n // Q_BLOCK, Q_BLOCK, H, E), 1, 0)

    def one_block(q_blk):
        s = jnp.einsum('bqhe,bkhe->bhqk', q_blk, k).astype(jnp.float32) * scale
        p = jax.nn.softmax(s, axis=-1).astype(v.dtype)
        return jnp.einsum('bhqk,bkhv->bqhv', p, v)

    o = lax.map(one_block, qb)
    return jnp.moveaxis(o, 0, 1).reshape(B, n, H, v.shape[-1])


def neighbourhood_attention(q, k, v, k_ctx, v_ctx, rpb):
    B, n, H, d = q.shape
    rows = n // GRID_W
    kr = min(NA_KR, rows)
    scale = d ** -0.5
    grid = lambda t: t.reshape(B, rows, GRID_W, H, d)
    qg, kg, vg = grid(q), grid(k), grid(v)
    cols = jnp.arange(GRID_W)
    col_idx = jnp.clip(cols - NA_KC // 2, 0, GRID_W - NA_KC)[:, None] + jnp.arange(NA_KC)[None, :]
    dc = col_idx - cols[:, None]
    n_loc = kr * NA_KC

    def one_row(r):
        rs = jnp.clip(r - kr // 2, 0, rows - kr)
        q_r = lax.dynamic_index_in_dim(qg, r, axis=1, keepdims=False)
        k_win = lax.dynamic_slice_in_dim(kg, rs, kr, axis=1)[:, :, col_idx]
        v_win = lax.dynamic_slice_in_dim(vg, rs, kr, axis=1)[:, :, col_idx]
        dr = rs + jnp.arange(kr) - r
        bias = rpb[:, dr[:, None, None] + NA_KR - 1, dc[None] + NA_KC - 1]
        bias = jnp.transpose(bias, (0, 2, 1, 3)).reshape(H, GRID_W, n_loc).astype(jnp.float32)
        s_loc = jnp.einsum('bwhd,biwjhd->bhwij', q_r, k_win).reshape(B, H, GRID_W, n_loc)
        s_ctx = jnp.einsum('bwhd,bchd->bhwc', q_r, k_ctx)
        s = jnp.concatenate([s_loc.astype(jnp.float32) * scale + bias,
                             s_ctx.astype(jnp.float32) * scale], axis=-1)
        p = jax.nn.softmax(s, axis=-1).astype(v.dtype)
        p_loc = p[..., :n_loc].reshape(B, H, GRID_W, kr, NA_KC)
        return (jnp.einsum('bhwij,biwjhd->bwhd', p_loc, v_win)
                + jnp.einsum('bhwc,bchd->bwhd', p[..., n_loc:], v_ctx))

    o = lax.map(one_row, jnp.arange(rows))
    return jnp.moveaxis(o, 0, 1).reshape(B, n, H, d)


def mlstm_scan(q, k, v, i_pre, f_pre, state, with_out):
    B, H, n, _ = q.shape
    nc = n // ML_CHUNK
    chunks = lambda t: jnp.moveaxis(t.reshape((B, H, nc, ML_CHUNK) + t.shape[3:]), 2, 0)
    tri = jnp.tril(jnp.ones((ML_CHUNK, ML_CHUNK), dtype=bool))

    def step(carry, inp):
        C, nv, m = carry
        qc, kc, vc, ic, fc = inp
        b = jnp.cumsum(jax.nn.log_sigmoid(fc), axis=-1)
        bL = b[..., -1]
        w_log = bL[..., None] - b + ic
        m_new = jnp.maximum(bL + m, jnp.max(w_log, axis=-1))
        decay = jnp.exp(bL + m - m_new)
        w = jnp.exp(w_log - m_new[..., None])
        C_new = decay[..., None, None] * C + jnp.einsum('bhs,bhsv,bhsk->bhvk', w, vc, kc)
        n_new = decay[..., None] * nv + jnp.einsum('bhs,bhsk->bhk', w, kc)
        new = (C_new, n_new, m_new)
        if not with_out:
            return new, None
        d_log = jnp.where(tri, b[..., :, None] - b[..., None, :] + ic[..., None, :], -jnp.inf)
        inter = b + m[..., None]
        m_t = jnp.maximum(inter, jnp.max(d_log, axis=-1))
        a = jnp.exp(inter - m_t)
        s = jnp.einsum('bhtk,bhsk->bhts', qc, kc) * jnp.exp(d_log - m_t[..., None])
        num = a[..., None] * jnp.einsum('bhvk,bhtk->bhtv', C, qc) + jnp.einsum('bhts,bhsv->bhtv', s, vc)
        den = a * jnp.einsum('bhk,bhtk->bht', nv, qc) + jnp.sum(s, axis=-1)
        h = num / jnp.maximum(jnp.abs(den), jnp.exp(-m_t))[..., None]
        return new, h

    state, hs = lax.scan(step, state, (chunks(q), chunks(k), chunks(v), chunks(i_pre), chunks(f_pre)))
    if not with_out:
        return state, None
    return state, jnp.moveaxis(hs, 0, 2).reshape(B, H, n, ML_V)


def mlstm_bidir(mq, mk, mv, mg, mq_c, mk_c, mv_c, mg_c, b_ml, ctx_out):
    def prep(q, k, v, g):
        B, n, _ = q.shape
        th = lambda t, dd: jnp.transpose(t.reshape(B, n, ML_HEADS, dd), (0, 2, 1, 3)).astype(jnp.float32)
        gates = jnp.transpose((g + b_ml).astype(jnp.float32).reshape(B, n, 4, ML_HEADS), (2, 0, 3, 1))
        return th(q, ML_QK), th(k, ML_QK) * (ML_QK ** -0.5), th(v, ML_V), gates

    q, k, v, g = prep(mq, mk, mv, mg)
    qc, kc, vc, gc = prep(mq_c, mk_c, mv_c, mg_c)
    B = q.shape[0]
    zero = (jnp.zeros((B, ML_HEADS, ML_V, ML_QK), jnp.float32),
            jnp.zeros((B, ML_HEADS, ML_QK), jnp.float32),
            jnp.zeros((B, ML_HEADS), jnp.float32))
    fl = lambda t: jnp.flip(t, axis=2)
    s_f, hc_f = mlstm_scan(qc, kc, vc, gc[0], gc[1], zero, ctx_out)
    _, h_f = mlstm_scan(q, k, v, g[0], g[1], s_f, True)
    s_b, hc_b = mlstm_scan(fl(qc), fl(kc), fl(vc), fl(gc[2]), fl(gc[3]), zero, ctx_out)
    _, h_b = mlstm_scan(fl(q), fl(k), fl(v), fl(g[2]), fl(g[3]), s_b, True)
    h = jnp.transpose(h_f + fl(h_b), (0, 2, 1, 3))
    if not ctx_out:
        return h, None
    return h, jnp.transpose(hc_f + fl(hc_b), (0, 2, 1, 3))


def mlstm_output(h, o, g_ml):
    hn = rmsnorm(h, g_ml.reshape(ML_HEADS, ML_V)).reshape(o.shape[:2] + (ML_W,))
    return hn.astype(o.dtype) * jax.nn.sigmoid(o)


def merge(a, b, m, gates, w_br_mla, w_br_na, w_br_ml, w_out):
    g_a, g_b, g_m = jnp.split(jax.nn.sigmoid(gates), 3, axis=-1)
    flat = lambda t: t.reshape(t.shape[:2] + (-1,))
    y = g_a * (flat(a) @ w_br_mla) + g_b * (flat(b) @ w_br_na) + g_m * (m @ w_br_ml)
    return y @ w_out


def token_mixers(h, hc, w_in, g_q, g_kv, w_uq, w_ukv, rpb, b_ml, g_ml,
                 w_br_mla, w_br_na, w_br_ml, w_out, ctx_out):
    n = h.shape[1]
    (cq, ckv, kr, nq, nk, nv, mq, mk, mv, mo, mg, bg) = split_in(h @ w_in)
    (cq_c, ckv_c, kr_c, nq_c, nk_c, nv_c, mq_c, mk_c, mv_c, mo_c, mg_c, bg_c) = split_in(hc @ w_in)
    cos, sin = rope_2d(n)
    k_c, v_c = mla_keys_values(ckv_c, kr_c, g_kv, w_ukv, None, None)
    k_l, v_l = mla_keys_values(ckv, kr, g_kv, w_ukv, cos, sin)
    a = block_attention(mla_queries(cq, g_q, w_uq, cos, sin),
                        jnp.concatenate([k_c, k_l], axis=1), jnp.concatenate([v_c, v_l], axis=1))
    nh = lambda t: t.reshape(t.shape[:2] + (NA_HEADS, NA_DIM))
    b = neighbourhood_attention(nh(nq), nh(nk), nh(nv), nh(nk_c), nh(nv_c), rpb)
    m, m_c = mlstm_bidir(mq, mk, mv, mg, mq_c, mk_c, mv_c, mg_c, b_ml, ctx_out)
    y = merge(a, b, mlstm_output(m, mo, g_ml), bg, w_br_mla, w_br_na, w_br_ml, w_out)
    if not ctx_out:
        return y, None
    a_c = block_attention(mla_queries(cq_c, g_q, w_uq, None, None), k_c, v_c)
    b_c = block_attention(nh(nq_c), nh(nk_c), nh(nv_c))
    y_c = merge(a_c, b_c, mlstm_output(m_c, mo_c, g_ml), bg_c, w_br_mla, w_br_na, w_br_ml, w_out)
    return y, y_c


def expert_choice_ffn(h, w_router, w1, w3, w2):
    B, n, _ = h.shape
    cap = EC_FACTOR * n // N_EXPERTS
    aff = jax.nn.softmax((h @ w_router).astype(jnp.float32), axis=-1)
    g, idx = lax.top_k(jnp.swapaxes(aff, 1, 2), cap)
    bidx = jnp.arange(B)[:, None, None]
    xs = h[bidx, idx]
    hid = jax.nn.silu(jnp.einsum('becd,edf->becf', xs, w1)) * jnp.einsum('becd,edf->becf', xs, w3)
    y = jnp.einsum('becf,efd->becd', hid, w2) * g[..., None].astype(h.dtype)
    return jnp.zeros_like(h).at[bidx, idx].add(y)


def setup_inputs(seed: int = 0) -> dict:
    key = jax.random.key(seed)
    ks = iter(jax.random.split(key, 32))
    nrm = lambda shape, scale: jax.random.normal(next(ks), shape, jnp.float32) * scale
    L, D = DEPTH, D_MODEL
    f_bias = jnp.linspace(3.0, 6.0, ML_HEADS)
    gate_base = jnp.concatenate([jnp.zeros(ML_HEADS), f_bias, jnp.zeros(ML_HEADS), f_bias])
    return {
        'x': nrm((BATCH, SEQ, D), 1.0),
        'c': nrm((BATCH, D), 1.0),
        'ctx': nrm((BATCH, CTX_LEN, D), 1.0),
        'c_ctx': nrm((D,), 1.0),
        'w_mod': nrm((L, D, 6 * D), 0.5 * D ** -0.5),
        'b_mod': nrm((L, 6 * D), 0.02),
        'g_pre1': 1.0 + nrm((L, D), 0.02),
        'g_post1': 1.0 + nrm((L, D), 0.02),
        'g_pre2': 1.0 + nrm((L, D), 0.02),
        'g_post2': 1.0 + nrm((L, D), 0.02),
        'w_in': nrm((L, D, D_IN), D ** -0.5),
        'g_q': 1.0 + nrm((L, MLA_Q_RANK), 0.02),
        'g_kv': 1.0 + nrm((L, MLA_KV_RANK), 0.02),
        'w_uq': nrm((L, MLA_Q_RANK, MLA_HEADS, MLA_NOPE + MLA_ROPE), MLA_Q_RANK ** -0.5),
        'w_ukv': nrm((L, MLA_KV_RANK, MLA_HEADS, MLA_NOPE + MLA_V), MLA_KV_RANK ** -0.5),
        'rpb': nrm((L, NA_HEADS, 2 * NA_KR - 1, 2 * NA_KC - 1), 0.1),
        'b_ml_gates': gate_base + nrm((L, 4 * ML_HEADS), 0.1),
        'g_ml': 1.0 + nrm((L, ML_W), 0.02),
        'w_br_mla': nrm((L, MLA_W, D), MLA_W ** -0.5),
        'w_br_na': nrm((L, NA_W, D), NA_W ** -0.5),
        'w_br_ml': nrm((L, ML_W, D), ML_W ** -0.5),
        'w_out': nrm((L, D, D), D ** -0.5),
        'w_router': nrm((L, D, N_EXPERTS), D ** -0.5),
        'w_e1': nrm((L, N_EXPERTS, D, D_EXPERT), D ** -0.5),
        'w_e3': nrm((L, N_EXPERTS, D, D_EXPERT), D ** -0.5),
        'w_e2': nrm((L, N_EXPERTS, D_EXPERT, D), D_EXPERT ** -0.5),
    }


def reference(x, c, ctx, c_ctx, w_mod, b_mod, g_pre1, g_post1, g_pre2, g_post2, w_in, g_q, g_kv,
              w_uq, w_ukv, rpb, b_ml_gates, g_ml, w_br_mla, w_br_na, w_br_ml, w_out,
              w_router, w_e1, w_e3, w_e2):
    xc = ctx
    for l in range(DEPTH):
        ctx_out = l < DEPTH - 1
        mod = jax.nn.silu(c) @ w_mod[l] + b_mod[l]
        mod_c = jax.nn.silu(c_ctx) @ w_mod[l] + b_mod[l]
        sh1, sc1, gt1, sh2, sc2, gt2 = jnp.split(mod[:, None, :], 6, axis=-1)
        shc1, scc1, gtc1, shc2, scc2, gtc2 = jnp.split(mod_c, 6, axis=-1)
        h = rmsnorm(x, g_pre1[l]) * (1.0 + sc1) + sh1
        hc = rmsnorm(xc, g_pre1[l]) * (1.0 + scc1) + shc1
        y, y_c = token_mixers(h, hc, w_in[l], g_q[l], g_kv[l], w_uq[l], w_ukv[l], rpb[l],
                              b_ml_gates[l], g_ml[l], w_br_mla[l], w_br_na[l], w_br_ml[l],
                              w_out[l], ctx_out)
        x = x + gt1 * rmsnorm(y, g_post1[l])
        h2 = rmsnorm(x, g_pre2[l]) * (1.0 + sc2) + sh2
        x = x + gt2 * rmsnorm(expert_choice_ffn(h2, w_router[l], w_e1[l], w_e3[l], w_e2[l]), g_post2[l])
        if ctx_out:
            xc = xc + gtc1 * rmsnorm(y_c, g_post1[l])
            hc2 = rmsnorm(xc, g_pre2[l]) * (1.0 + scc2) + shc2
            xc = xc + gtc2 * rmsnorm(expert_choice_ffn(hc2, w_router[l], w_e1[l], w_e3[l], w_e2[l]), g_post2[l])
    return x
```

```python
import functools

import numpy as np
import jax
import jax.numpy as jnp
from jax import lax
from jax.experimental import pallas as pl
from jax.experimental.pallas import tpu as pltpu

F32 = jnp.float32
BF16 = jnp.bfloat16

D_MODEL = 1024
CTX_LEN = 256
GRID_W = 64
EPS = 1e-6

MLA_HEADS = 4
MLA_Q_RANK = 256
MLA_KV_RANK = 128
MLA_NOPE = 64
MLA_ROPE = 32
MLA_V = 64
ROPE_BASE = 10000.0

NA_HEADS = 4
NA_DIM = 64
NA_KR = 8
NA_KC = 16
NA_QROWS = 4
NA_KROWS = 12

ML_HEADS = 4
ML_QK = 64
ML_V = 128
ML_CHUNK = 128

N_EXPERTS = 16
D_EXPERT = 1024
EC_FACTOR = 2

LANES = 128
ROW_TILE = 256
HEAD_PAD = 128
NEG_BIG = -1e30

_IN_GROUPS = (("cq", 256), ("ckv", 128), ("kr1", 128), ("kr2", 128), ("nq", 256), ("nk", 256), ("nv", 256),
              ("mq", 256), ("mk", 256), ("mv", 512), ("mo", 512), ("mg", 128), ("bg", 3 * D_MODEL))
_OFF = {}
_o = 0
for _n, _w in _IN_GROUPS:
    _OFF[_n] = (_o, _o + _w)
    _o += _w
D_IN_PAD = _o


def _cparams(sem, vmem_mb):
    return pltpu.CompilerParams(dimension_semantics=sem, vmem_limit_bytes=vmem_mb * 1024 * 1024)


def _dot(a, b):
    return jnp.dot(a, b, preferred_element_type=F32)


def _dot_nt(a, b):
    return lax.dot_general(a, b, (((1,), (1,)), ((), ())), preferred_element_type=F32)


def _rms(x, g):
    return x * lax.rsqrt(jnp.mean(x * x, axis=-1, keepdims=True) + EPS) * g


def _split3(x):
    hi = x.astype(BF16)
    r1 = x - hi.astype(F32)
    mid = r1.astype(BF16)
    lo = (r1 - mid.astype(F32)).astype(BF16)
    return hi, mid, lo


def _log_sigmoid(x):
    return jnp.minimum(x, 0.0) - jnp.log(1.0 + jnp.exp(-jnp.abs(x)))


def _mod_kernel(c_ref, w_ref, b_ref, o_ref):
    cs = c_ref[...]
    cs = cs * jax.nn.sigmoid(cs)
    o_ref[0] = _dot(cs.astype(BF16), w_ref[0].astype(BF16)) + b_ref[0]


def _modulation(c_all, w_mod, b_mod):
    depth, d, n6 = w_mod.shape
    rows = c_all.shape[0]
    tn = 512
    return pl.pallas_call(
        _mod_kernel,
        grid=(depth, n6 // tn),
        in_specs=[pl.BlockSpec((rows, d), lambda l, j: (0, 0)),
                  pl.BlockSpec((1, d, tn), lambda l, j: (l, 0, j)),
                  pl.BlockSpec((1, 1, tn), lambda l, j: (l, 0, j))],
        out_specs=pl.BlockSpec((1, rows, tn), lambda l, j: (l, 0, j)),
        out_shape=jax.ShapeDtypeStruct((depth, rows, n6), F32),
        compiler_params=_cparams(("parallel", "parallel"), 32),
        name="modulation",
    )(c_all, w_mod, b_mod.reshape(depth, 1, n6))


def _inproj_kernel(x_ref, sh_ref, sc_ref, g1_ref, w_ref, gq_ref, gkv_ref, wqa_ref, wqb_ref, wk_ref, wv_ref,
                   ta_ref, tb_ref, tc_ref, ts_ref,
                   q_ref, k_ref, v_ref, nq_ref, nk_ref, nv_ref, mq_ref, mk_ref, mv_ref, mo_ref, mg_ref, bg_ref):
    x = x_ref[0]
    h = _rms(x, g1_ref[...]) * (1.0 + sc_ref[0]) + sh_ref[0]
    hb = h.astype(BF16)

    def proj(name, lo=None, hi=None):
        a, b = _OFF[name]
        if lo is not None:
            a, b = a + lo, a + hi
        return _dot(hb, w_ref[:, a:b])

    nq_ref[0] = proj("nq").astype(BF16)
    nk_ref[0] = proj("nk").astype(BF16)
    nv_ref[0] = proj("nv").astype(BF16)
    mq_ref[0] = proj("mq").astype(BF16)
    mk_ref[0] = proj("mk").astype(BF16)
    mv_ref[0] = proj("mv").astype(BF16)
    mo_ref[0] = proj("mo").astype(BF16)
    mg_ref[0] = proj("mg")
    for j in range(3):
        bg_ref[0, :, j * D_MODEL:(j + 1) * D_MODEL] = proj("bg", j * D_MODEL, (j + 1) * D_MODEL).astype(BF16)

    cqn = _rms(proj("cq"), gq_ref[...]).astype(BF16)
    qa = _dot(cqn, wqa_ref[...])
    qb = _dot(cqn, wqb_ref[...])
    ta = ta_ref[...]
    tb = tb_ref[...]
    ckvn = _rms(proj("ckv"), gkv_ref[...]).astype(BF16)
    kn = _dot(ckvn, wk_ref[...])
    vv = _dot(ckvn, wv_ref[...])
    krr = proj("kr1") * tc_ref[...] + proj("kr2") * ts_ref[...]
    one_col = (lax.broadcasted_iota(jnp.int32, (1, HEAD_PAD), 1) == MLA_V).astype(F32)
    for hh in range(MLA_HEADS):
        sl = slice(hh * HEAD_PAD, (hh + 1) * HEAD_PAD)
        q_ref[0, hh] = (qa[:, sl] * ta + qb[:, sl] * tb).astype(BF16)
        k_ref[0, hh] = (kn[:, sl] + krr).astype(BF16)
        v_ref[0, hh] = (vv[:, sl] + one_col).astype(BF16)


def _inproj(X, mod3, g_pre1, w_pad, g_q, g_kv, wqa, wqb, wk, wv, tabs, n_lat):
    B, T, d = X.shape
    tm = ROW_TILE
    nt = T // tm
    ctx_tile = n_lat // tm
    ctx_row = B

    def mrow(b, i):
        return jnp.where(i >= ctx_tile, ctx_row, b)

    full = lambda shape: pl.BlockSpec(shape, lambda b, i: (0,) * len(shape))
    tab = pl.BlockSpec((tm, HEAD_PAD), lambda b, i: (i, 0))
    row_out = lambda w: pl.BlockSpec((1, tm, w), lambda b, i: (b, i, 0))
    head_out = pl.BlockSpec((1, MLA_HEADS, tm, HEAD_PAD), lambda b, i: (b, 0, i, 0))
    sd = jax.ShapeDtypeStruct
    outs = pl.pallas_call(
        _inproj_kernel,
        grid=(B, nt),
        in_specs=[pl.BlockSpec((1, tm, d), lambda b, i: (b, i, 0)),
                  pl.BlockSpec((1, 1, d), lambda b, i: (mrow(b, i), 0, 0)),
                  pl.BlockSpec((1, 1, d), lambda b, i: (mrow(b, i), 0, 1)),
                  full((1, d)), full((d, D_IN_PAD)), full((1, MLA_Q_RANK)), full((1, MLA_KV_RANK)),
                  full((MLA_Q_RANK, MLA_HEADS * HEAD_PAD)), full((MLA_Q_RANK, MLA_HEADS * HEAD_PAD)),
                  full((MLA_KV_RANK, MLA_HEADS * HEAD_PAD)), full((MLA_KV_RANK, MLA_HEADS * HEAD_PAD)),
                  tab, tab, tab, tab],
        out_specs=[head_out, head_out, head_out,
                   row_out(256), row_out(256), row_out(256), row_out(256), row_out(256),
                   row_out(512), row_out(512), row_out(128), row_out(3 * d)],
        out_shape=[sd((B, MLA_HEADS, T, HEAD_PAD), BF16)] * 3
                  + [sd((B, T, 256), BF16)] * 5 + [sd((B, T, 512), BF16)] * 2
                  + [sd((B, T, 128), F32), sd((B, T, 3 * d), BF16)],
        compiler_params=_cparams(("parallel", "parallel"), 56),
        name="inproj",
    )(X, mod3, mod3, g_pre1, w_pad, g_q, g_kv, wqa, wqb, wk, wv, *tabs)
    return outs


def _flash_kernel(q_ref, k_ref, v_ref, o_ref, s_ref, *, nk, kc):
    tq = q_ref.shape[2]
    nchunks = nk // kc
    for hh in range(MLA_HEADS):
        q = q_ref[0, hh]

        def scores(c, mx):
            ks = k_ref[0, hh, pl.ds(pl.multiple_of(c * kc, kc), kc), :]
            s = _dot_nt(q, ks)
            s_ref[c] = s
            for j in range(kc // LANES):
                mx = jnp.maximum(mx, s[:, j * LANES:(j + 1) * LANES])
            return mx

        mx = lax.fori_loop(0, nchunks, scores, jnp.full((tq, LANES), -jnp.inf, F32))
        m = jnp.max(mx, axis=1, keepdims=True)

        def weighted(c, acc):
            p = jnp.exp(s_ref[c] - m).astype(BF16)
            vs = v_ref[0, hh, pl.ds(pl.multiple_of(c * kc, kc), kc), :]
            return acc + _dot(p, vs)

        acc = lax.fori_loop(0, nchunks, weighted, jnp.zeros((tq, HEAD_PAD), F32))
        o = acc[:, :MLA_V] / acc[:, MLA_V:MLA_V + 1]
        o_ref[0, :, hh * MLA_V:(hh + 1) * MLA_V] = o.astype(BF16)


def _mla_attention(Q, K, V, n_lat, prev=None):
    B, H, T, E = Q.shape
    kc = 256
    sd = jax.ShapeDtypeStruct
    if prev is None:
        tq = 512
        nk = T
        grid = (B, n_lat // tq)
        q_spec = pl.BlockSpec((1, H, tq, E), lambda b, i: (b, 0, i, 0))
        kv_spec = pl.BlockSpec((1, H, T, E), lambda b, i: (b, 0, 0, 0))
        o_spec = pl.BlockSpec((1, tq, H * MLA_V), lambda b, i: (b, i, 0))
        extra_in, extra_specs, aliases = (), [], {}
    else:
        tq = CTX_LEN
        nk = CTX_LEN
        cblk = n_lat // CTX_LEN
        grid = (B, 1)
        q_spec = pl.BlockSpec((1, H, tq, E), lambda b, i: (b, 0, cblk, 0))
        kv_spec = pl.BlockSpec((1, H, nk, E), lambda b, i: (b, 0, cblk, 0))
        o_spec = pl.BlockSpec((1, tq, H * MLA_V), lambda b, i: (b, cblk, 0))
        extra_in, extra_specs, aliases = (prev,), [pl.BlockSpec(memory_space=pl.ANY)], {3: 0}

    def body(q_ref, k_ref, v_ref, *rest):
        o_ref, s_ref = rest[-2], rest[-1]
        _flash_kernel(q_ref, k_ref, v_ref, o_ref, s_ref, nk=nk, kc=kc)

    return pl.pallas_call(
        body,
        grid=grid,
        in_specs=[q_spec, kv_spec, kv_spec] + extra_specs,
        out_specs=o_spec,
        out_shape=sd((B, T, H * MLA_V), BF16),
        scratch_shapes=[pltpu.VMEM((nk // kc, tq, kc), F32)],
        input_output_aliases=aliases,
        compiler_params=_cparams(("parallel", "parallel"), 56),
        name="mla_attention" if prev is None else "mla_attention_ctx",
    )(Q, K, V, *extra_in)


def _na_kernel(q_ref, k_ref, v_ref, bias_ref, o_ref, *, n_lat, rows, with_ctx):
    j = pl.program_id(1)
    nblk = rows // NA_QROWS
    scale = NA_DIM ** -0.5
    q = q_ref[0]
    kctx = k_ref[0, n_lat:n_lat + CTX_LEN, :]
    vctx = v_ref[0, n_lat:n_lat + CTX_LEN, :]

    def local_step():
        kb = jnp.clip(j * NA_QROWS - NA_KR // 2, 0, rows - NA_KROWS)
        start = pl.multiple_of(kb * GRID_W, GRID_W)
        kloc = k_ref[0, pl.ds(start, NA_KROWS * GRID_W), :]
        vloc = v_ref[0, pl.ds(start, NA_KROWS * GRID_W), :]
        for hh in range(NA_HEADS):
            sl = slice(hh * NA_DIM, (hh + 1) * NA_DIM)
            qh = q[:, sl]
            s_loc = _dot_nt(qh, kloc[:, sl]) * scale + bias_ref[0, hh]
            s_ctx = _dot_nt(qh, kctx[:, sl]) * scale
            m = jnp.maximum(jnp.max(s_loc, axis=1, keepdims=True), jnp.max(s_ctx, axis=1, keepdims=True))
            p_loc = jnp.exp(s_loc - m)
            p_ctx = jnp.exp(s_ctx - m)
            den = jnp.sum(p_loc, axis=1, keepdims=True) + jnp.sum(p_ctx, axis=1, keepdims=True)
            o = _dot(p_loc.astype(BF16), vloc[:, sl]) + _dot(p_ctx.astype(BF16), vctx[:, sl])
            o_ref[0, :, sl] = (o / den).astype(BF16)

    def ctx_step():
        for hh in range(NA_HEADS):
            sl = slice(hh * NA_DIM, (hh + 1) * NA_DIM)
            s = _dot_nt(q[:, sl], kctx[:, sl]) * scale
            p = jnp.exp(s - jnp.max(s, axis=1, keepdims=True))
            den = jnp.sum(p, axis=1, keepdims=True)
            o_ref[0, :, sl] = (_dot(p.astype(BF16), vctx[:, sl]) / den).astype(BF16)

    if with_ctx:
        pl.when(j < nblk)(local_step)
        pl.when(j == nblk)(ctx_step)
    else:
        local_step()


def _na_bias_table(rpb, rows):
    W = GRID_W
    pats = []
    for r0 in (0, NA_QROWS, rows - NA_QROWS):
        kb = int(np.clip(r0 - NA_KR // 2, 0, rows - NA_KROWS))
        r = r0 + np.arange(NA_QROWS)[:, None, None, None]
        w = np.arange(W)[None, :, None, None]
        kr = kb + np.arange(NA_KROWS)[None, None, :, None]
        c = np.arange(W)[None, None, None, :]
        rs = np.clip(r - NA_KR // 2, 0, rows - NA_KR)
        cs = np.clip(w - NA_KC // 2, 0, W - NA_KC)
        valid = (kr >= rs) & (kr < rs + NA_KR) & (c >= cs) & (c < cs + NA_KC)
        dr = np.clip(kr - r + NA_KR - 1, 0, 2 * NA_KR - 2) + 0 * c + 0 * w
        dc = np.clip(c - w + NA_KC - 1, 0, 2 * NA_KC - 2) + 0 * kr + 0 * r
        valid = np.broadcast_to(valid, dr.shape).reshape(NA_QROWS * W, NA_KROWS * W)
        dr = dr.reshape(NA_QROWS * W, NA_KROWS * W)
        dc = dc.reshape(NA_QROWS * W, NA_KROWS * W)
        pats.append(jnp.where(jnp.asarray(valid)[None], rpb[:, dr, dc].astype(F32), NEG_BIG))
    return jnp.stack(pats, axis=0)


def _neighbourhood_attention(nq, nk, nv, bias_tab, n_lat, with_ctx):
    B, T, w = nq.shape
    rows = n_lat // GRID_W
    nblk = rows // NA_QROWS
    tq = NA_QROWS * GRID_W
    steps = nblk + (1 if with_ctx else 0)

    def pat(j):
        return jnp.where(j == 0, 0, jnp.where(j >= nblk - 1, 2, 1))

    return pl.pallas_call(
        functools.partial(_na_kernel, n_lat=n_lat, rows=rows, with_ctx=with_ctx),
        grid=(B, steps),
        in_specs=[pl.BlockSpec((1, tq, w), lambda b, j: (b, j, 0)),
                  pl.BlockSpec((1, T, w), lambda b, j: (b, 0, 0)),
                  pl.BlockSpec((1, T, w), lambda b, j: (b, 0, 0)),
                  pl.BlockSpec((1, NA_HEADS, tq, NA_KROWS * GRID_W), lambda b, j: (pat(j), 0, 0, 0))],
        out_specs=pl.BlockSpec((1, tq, w), lambda b, j: (b, j, 0)),
        out_shape=jax.ShapeDtypeStruct((B, T, w), BF16),
        compiler_params=_cparams(("parallel", "arbitrary"), 48),
        name="neighbourhood_attention",
    )(nq, nk, nv, bias_tab)


def _mlstm_kernel(qf_ref, kf_ref, vf_ref, gf_ref, gtf_ref, qb_ref, kb_ref, vb_ref, gb_ref, gtb_ref,
                  bcol_ref, brow_ref, hf_ref, hb_ref, st_ref, m_ref):
    L = ML_CHUNK

    @pl.when(pl.program_id(1) == 0)
    def _():
        st_ref[...] = jnp.zeros_like(st_ref)
        m_ref[...] = jnp.zeros_like(m_ref)

    r_i = lax.broadcasted_iota(jnp.int32, (L, L), 0)
    c_i = lax.broadcasted_iota(jnp.int32, (L, L), 1)
    ones_v = jnp.ones((L, ML_V), BF16)
    dirs = ((qf_ref, kf_ref, vf_ref, gf_ref, gtf_ref, hf_ref), (qb_ref, kb_ref, vb_ref, gb_ref, gtb_ref, hb_ref))
    for d, (q_ref, k_ref, v_ref, g_ref, gt_ref, h_ref) in enumerate(dirs):
        causal = (c_i <= r_i) if d == 0 else (c_i >= r_i)
        m_col = jnp.where(causal, 1.0, 0.0).astype(BF16)
        m_row = jnp.where((r_i <= c_i) if d == 0 else (r_i >= c_i), 1.0, 0.0).astype(BF16)
        gt = gt_ref[0] + bcol_ref[...]
        lf_r = _log_sigmoid(gt)
        br = sum(_dot(p, m_row) for p in _split3(lf_r))
        tot = jnp.sum(lf_r, axis=1, keepdims=True)
        gc = g_ref[0] + brow_ref[...]
        lf_c = _log_sigmoid(gc)
        bc = sum(_dot(m_col, p) for p in _split3(lf_c))
        kf32 = k_ref[0].astype(F32) * (ML_QK ** -0.5)
        k16 = kf32.astype(BF16)
        k_t = kf32.T
        q16 = q_ref[0]
        v16 = v_ref[0]
        for hh in range(ML_HEADS):
            c = d * ML_HEADS + hh
            ji = (2 * d) * ML_HEADS + hh
            jf = (2 * d + 1) * ML_HEADS + hh
            i_row = gt[ji:ji + 1, :]
            b_row = br[jf:jf + 1, :]
            b_last = tot[jf:jf + 1, :]
            b_col = bc[:, jf:jf + 1]
            m_old = m_ref[c:c + 1, 0:1]
            st_old = st_ref[c]
            qh = q16[:, hh * ML_QK:(hh + 1) * ML_QK]
            kh = k16[:, hh * ML_QK:(hh + 1) * ML_QK]
            v_aug = jnp.concatenate([v16[:, hh * ML_V:(hh + 1) * ML_V], ones_v], axis=1)
            d_log = jnp.where(causal, b_col - b_row + i_row, -jnp.inf)
            inter = b_col + m_old
            m_t = jnp.maximum(inter, jnp.max(d_log, axis=1, keepdims=True))
            a = jnp.exp(inter - m_t)
            smat = (_dot_nt(qh, kh) * jnp.exp(d_log - m_t)).astype(BF16)
            out = a * _dot(qh, st_old.astype(BF16)) + _dot(smat, v_aug)
            den = jnp.maximum(jnp.abs(out[:, ML_V:]), jnp.exp(-m_t))
            h_ref[0, :, hh * ML_V:(hh + 1) * ML_V] = out[:, :ML_V] / den
            w_log = b_last - b_row + i_row
            m_new = jnp.maximum(b_last + m_old, jnp.max(w_log, axis=1, keepdims=True))
            decay = jnp.exp(b_last + m_old - m_new)
            kw = (k_t[hh * ML_QK:(hh + 1) * ML_QK, :] * jnp.exp(w_log - m_new)).astype(BF16)
            st_ref[c] = decay * st_old + _dot(kw, v_aug)
            m_ref[c:c + 1, :] = jnp.broadcast_to(m_new, (1, LANES))


def _mlstm(mq, mk, mv, mg, mgt, b_ml, n_lat):
    B, T, _ = mq.shape
    L = ML_CHUNK
    nl = n_lat // L
    nc = CTX_LEN // L
    steps = nl + nc

    def cf(s):
        return jnp.where(s < nc, nl + s, s - nc)

    def cb(s):
        return steps - 1 - s

    def specs(cm):
        return [pl.BlockSpec((1, L, ML_HEADS * ML_QK), lambda b, s: (b, cm(s), 0)),
                pl.BlockSpec((1, L, ML_HEADS * ML_QK), lambda b, s: (b, cm(s), 0)),
                pl.BlockSpec((1, L, ML_HEADS * ML_V), lambda b, s: (b, cm(s), 0)),
                pl.BlockSpec((1, L, LANES), lambda b, s: (b, cm(s), 0)),
                pl.BlockSpec((1, 4 * ML_HEADS, L), lambda b, s: (b, 0, cm(s)))]

    b_row = jnp.concatenate([b_ml.astype(F32), jnp.zeros((LANES - 4 * ML_HEADS,), F32)])[None, :]
    b_col = b_ml.astype(F32)[:, None]
    sd = jax.ShapeDtypeStruct
    return pl.pallas_call(
        _mlstm_kernel,
        grid=(B, steps),
        in_specs=specs(cf) + specs(cb) + [pl.BlockSpec((4 * ML_HEADS, 1), lambda b, s: (0, 0)),
                                          pl.BlockSpec((1, LANES), lambda b, s: (0, 0))],
        out_specs=[pl.BlockSpec((1, L, ML_HEADS * ML_V), lambda b, s: (b, cf(s), 0)),
                   pl.BlockSpec((1, L, ML_HEADS * ML_V), lambda b, s: (b, cb(s), 0))],
        out_shape=[sd((B, T, ML_HEADS * ML_V), F32)] * 2,
        scratch_shapes=[pltpu.VMEM((2 * ML_HEADS, ML_QK, 2 * ML_V), F32), pltpu.VMEM((2 * ML_HEADS, LANES), F32)],
        compiler_params=_cparams(("parallel", "arbitrary"), 32),
        name="mlstm",
    )(mq, mk, mv, mg, mgt, mq, mk, mv, mg, mgt, b_col, b_row)


def _merge_kernel(a_ref, bn_ref, hf_ref, hb_ref, mo_ref, bg_ref, x_ref, gt1_ref, sh2_ref, sc2_ref,
                  gml_ref, gpost_ref, gpre2_ref, wa_ref, wb_ref, wm_ref, wo_ref, wrh_ref, wrl_ref,
                  xo_ref, h2_ref, aff_ref):
    d = D_MODEL
    hm = hf_ref[0] + hb_ref[0]
    gml = gml_ref[...]
    parts = []
    for hh in range(ML_HEADS):
        sl = slice(hh * ML_V, (hh + 1) * ML_V)
        parts.append(_rms(hm[:, sl], gml[:, sl]))
    mlo = (jnp.concatenate(parts, axis=1) * jax.nn.sigmoid(mo_ref[0].astype(F32))).astype(BF16)
    ya = _dot(a_ref[0], wa_ref[...])
    yb = _dot(bn_ref[0], wb_ref[...])
    ym = _dot(mlo, wm_ref[...])
    y = (jax.nn.sigmoid(bg_ref[0, :, 0:d].astype(F32)) * ya
         + jax.nn.sigmoid(bg_ref[0, :, d:2 * d].astype(F32)) * yb
         + jax.nn.sigmoid(bg_ref[0, :, 2 * d:3 * d].astype(F32)) * ym)
    out = _dot(y.astype(BF16), wo_ref[...])
    xn = x_ref[0] + gt1_ref[0] * _rms(out, gpost_ref[...])
    xo_ref[0] = xn
    h2 = _rms(xn, gpre2_ref[...]) * (1.0 + sc2_ref[0]) + sh2_ref[0]
    h2_ref[0] = h2.astype(BF16)
    hi = h2.astype(BF16)
    lo = (h2 - hi.astype(F32)).astype(BF16)
    logits = _dot(hi, wrh_ref[...]) + _dot(lo, wrh_ref[...]) + _dot(hi, wrl_ref[...])
    lane = lax.broadcasted_iota(jnp.int32, logits.shape, 1)
    logits = jnp.where(lane < N_EXPERTS, logits, -jnp.inf)
    e = jnp.exp(logits - jnp.max(logits, axis=1, keepdims=True))
    aff_ref[0] = e / jnp.sum(e, axis=1, keepdims=True)


def _merge(a, bn, hf, hb, mo, bg, X, mod3, g_ml, g_post1, g_pre2, wa, wb, wm, wo, wrh, wrl, n_lat, with_ctx):
    B, T, d = X.shape
    tm = ROW_TILE
    ctx_tile = n_lat // tm
    nt = ctx_tile + (1 if with_ctx else 0)

    def mrow(b, i):
        return jnp.where(i >= ctx_tile, B, b)

    full = lambda shape: pl.BlockSpec(shape, lambda b, i: (0,) * len(shape))
    row = lambda w: pl.BlockSpec((1, tm, w), lambda b, i: (b, i, 0))
    modc = lambda ch: pl.BlockSpec((1, 1, d), lambda b, i: (mrow(b, i), 0, ch))
    sd = jax.ShapeDtypeStruct
    return pl.pallas_call(
        _merge_kernel,
        grid=(B, nt),
        in_specs=[row(256), row(256), row(512), row(512), row(512), row(3 * d), row(d),
                  modc(2), modc(3), modc(4),
                  full((1, ML_HEADS * ML_V)), full((1, d)), full((1, d)),
                  full((256, d)), full((256, d)), full((512, d)), full((d, d)), full((d, LANES)), full((d, LANES))],
        out_specs=[row(d), row(d), row(LANES)],
        out_shape=[sd((B, T, d), F32), sd((B, T, d), BF16), sd((B, T, LANES), F32)],
        compiler_params=_cparams(("parallel", "parallel"), 48),
        name="merge",
    )(a, bn, hf, hb, mo, bg, X, mod3, mod3, mod3, g_ml, g_post1, g_pre2, wa, wb, wm, wo, wrh, wrl)


def _select_kernel(aff_ref, slot_ref, *, cap):
    aff = aff_ref[0]
    n = aff.shape[1]
    bits = pltpu.bitcast(aff, jnp.int32)
    capf = jnp.float32(cap)

    def count(mask):
        return jnp.sum(jnp.where(mask, 1.0, 0.0), axis=1, keepdims=True)

    def refine(it, thr):
        cand = thr | jnp.left_shift(jnp.int32(1), 30 - it)
        return jnp.where(count(bits >= cand) >= capf, cand, thr)

    thr = lax.fori_loop(0, 31, refine, jnp.zeros((N_EXPERTS, 1), jnp.int32))
    gt = bits > thr
    eq = bits == thr
    need = capf - count(gt)
    r_i = lax.broadcasted_iota(jnp.int32, (LANES, LANES), 0)
    c_i = lax.broadcasted_iota(jnp.int32, (LANES, LANES), 1)
    before = jnp.where(r_i < c_i, 1.0, 0.0).astype(BF16)

    def excl_prefix(mask):
        carry = jnp.zeros((N_EXPERTS, 1), F32)
        outs = []
        for c in range(n // LANES):
            mc = jnp.where(mask[:, c * LANES:(c + 1) * LANES], 1.0, 0.0)
            outs.append(_dot(mc.astype(BF16), before) + carry)
            carry = carry + jnp.sum(mc, axis=1, keepdims=True)
        return jnp.concatenate(outs, axis=1)

    sel = gt | (eq & (excl_prefix(eq) < need))
    slot_ref[0] = jnp.where(sel, excl_prefix(sel), -1.0)


def _select(aff_t, cap):
    B, E, n = aff_t.shape
    return pl.pallas_call(
        functools.partial(_select_kernel, cap=cap),
        grid=(B,),
        in_specs=[pl.BlockSpec((1, E, n), lambda b: (b, 0, 0))],
        out_specs=pl.BlockSpec((1, E, n), lambda b: (b, 0, 0)),
        out_shape=jax.ShapeDtypeStruct((B, E, n), F32),
        compiler_params=_cparams(("parallel",), 32),
        name="route_select",
    )(aff_t)


def _dispatch_kernel(slot_ref, h_ref, xs_ref, *, cap, tc):
    n = h_ref.shape[1]
    srow = slot_ref[0, 0]
    sid = lax.broadcasted_iota(jnp.int32, (cap, tc), 0).astype(F32)
    acc = jnp.zeros((cap, D_MODEL), F32)
    for c in range(n // tc):
        sel = jnp.where(sid == srow[:, c * tc:(c + 1) * tc], 1.0, 0.0).astype(BF16)
        acc = acc + _dot(sel, h_ref[0, c * tc:(c + 1) * tc, :])
    xs_ref[0, 0] = acc.astype(BF16)


def _dispatch(slot4, h2, cap, n, row_blk):
    B, E = slot4.shape[:2]
    d = h2.shape[2]
    return pl.pallas_call(
        functools.partial(_dispatch_kernel, cap=cap, tc=min(n, 512)),
        grid=(B, E),
        in_specs=[pl.BlockSpec((1, 1, 1, n), lambda b, e: (b, e, 0, 0)),
                  pl.BlockSpec((1, n, d), lambda b, e: (b, row_blk, 0))],
        out_specs=pl.BlockSpec((1, 1, cap, d), lambda b, e: (b, e, 0, 0)),
        out_shape=jax.ShapeDtypeStruct((B, E, cap, d), BF16),
        compiler_params=_cparams(("parallel", "arbitrary"), 48),
        name="moe_dispatch",
    )(slot4, h2)


def _ffn_kernel(xs_ref, w1_ref, w3_ref, w2_ref, y_ref):
    x = xs_ref[0, 0]
    h1 = _dot(x, w1_ref[0])
    h3 = _dot(x, w3_ref[0])
    hid = (h1 * jax.nn.sigmoid(h1) * h3).astype(BF16)
    y_ref[0, 0] = _dot(hid, w2_ref[0]).astype(BF16)


def _expert_ffn(xs, w1, w3, w2):
    B, E, cap, d = xs.shape
    f = w1.shape[2]
    return pl.pallas_call(
        _ffn_kernel,
        grid=(E, B),
        in_specs=[pl.BlockSpec((1, 1, cap, d), lambda e, b: (b, e, 0, 0)),
                  pl.BlockSpec((1, d, f), lambda e, b: (e, 0, 0)),
                  pl.BlockSpec((1, d, f), lambda e, b: (e, 0, 0)),
                  pl.BlockSpec((1, f, d), lambda e, b: (e, 0, 0))],
        out_specs=pl.BlockSpec((1, 1, cap, d), lambda e, b: (b, e, 0, 0)),
        out_shape=jax.ShapeDtypeStruct((B, E, cap, d), BF16),
        compiler_params=_cparams(("parallel", "arbitrary"), 48),
        name="moe_expert_ffn",
    )(xs, w1, w3, w2)


def _combine_kernel(slotc_ref, aff_ref, y_ref, x_ref, gt2_ref, gpost_ref, xo_ref, acc_ref, *, cap):
    e = pl.program_id(2)

    @pl.when(e == 0)
    def _():
        acc_ref[...] = jnp.zeros_like(acc_ref)

    tt = acc_ref.shape[0]
    lane_e = lax.broadcasted_iota(jnp.int32, (tt, slotc_ref.shape[2]), 1)
    scol = jnp.sum(jnp.where(lane_e == e, slotc_ref[0], 0.0), axis=1, keepdims=True)
    lane_a = lax.broadcasted_iota(jnp.int32, (tt, LANES), 1)
    gcol = jnp.sum(jnp.where(lane_a == e, aff_ref[0], 0.0), axis=1, keepdims=True)
    sid = lax.broadcasted_iota(jnp.int32, (tt, cap), 1).astype(F32)
    sel_t = jnp.where(sid == scol, 1.0, 0.0).astype(BF16)
    acc_ref[...] += gcol * _dot(sel_t, y_ref[0, 0])

    @pl.when(e == N_EXPERTS - 1)
    def _():
        xo_ref[0] = x_ref[0] + gt2_ref[0] * _rms(acc_ref[...], gpost_ref[...])


def _combine(slotc, aff, y, X, mod3, g_post2, n, tok_off, is_ctx):
    B, T, d = X.shape
    E, cap = y.shape[1], y.shape[2]
    tt = min(n, 512)
    blk0 = tok_off // tt
    mrow = (lambda b: B) if is_ctx else (lambda b: b)
    return pl.pallas_call(
        functools.partial(_combine_kernel, cap=cap),
        grid=(B, n // tt, E),
        in_specs=[pl.BlockSpec((1, tt, E), lambda b, t, e: (b, t, 0)),
                  pl.BlockSpec((1, tt, LANES), lambda b, t, e: (b, blk0 + t, 0)),
                  pl.BlockSpec((1, 1, cap, d), lambda b, t, e: (b, e, 0, 0)),
                  pl.BlockSpec((1, tt, d), lambda b, t, e: (b, blk0 + t, 0)),
                  pl.BlockSpec((1, 1, d), lambda b, t, e: (mrow(b), 0, 5)),
                  pl.BlockSpec((1, d), lambda b, t, e: (0, 0))],
        out_specs=pl.BlockSpec((1, tt, d), lambda b, t, e: (b, blk0 + t, 0)),
        out_shape=jax.ShapeDtypeStruct((B, T, d), F32),
        scratch_shapes=[pltpu.VMEM((tt, d), F32)],
        input_output_aliases={3: 0},
        compiler_params=_cparams(("parallel", "parallel", "arbitrary"), 48),
        name="moe_combine",
    )(slotc, aff, y, X, mod3, g_post2)


def _moe(X, h2, aff, mod3, g_post2, w1, w3, w2, n, tok_off, is_ctx):
    cap = EC_FACTOR * n // N_EXPERTS
    aff_t = jnp.transpose(aff[:, tok_off:tok_off + n, :N_EXPERTS], (0, 2, 1))
    slot = _select(aff_t, cap)
    xs = _dispatch(slot[:, :, None, :], h2, cap, n, tok_off // n)
    y = _expert_ffn(xs, w1, w3, w2)
    return _combine(jnp.transpose(slot, (0, 2, 1)), aff, y, X, mod3, g_post2, n, tok_off, is_ctx)


def _rope_partner(w):
    lead = w.shape[:-1]
    ws = w.reshape(lead + (2, 2, MLA_ROPE // 4))
    return jnp.stack([-ws[..., 1, :], ws[..., 0, :]], axis=-2).reshape(lead + (MLA_ROPE,))


def _pad_cols(w, left, total):
    return jnp.pad(w, [(0, 0)] * (w.ndim - 1) + [(left, total - left - w.shape[-1])])


def _prep_in_weights(w_in):
    sizes = (MLA_Q_RANK, MLA_KV_RANK, MLA_ROPE, 256, 256, 256, 256, 256, 512, 512, 4 * ML_HEADS, 3 * D_MODEL)
    offs = np.cumsum(sizes)[:-1].tolist()
    cq, ckv, kr, nq, nk, nv, mq, mk, mv, mo, mg, bg = jnp.split(w_in, offs, axis=-1)
    kr1 = _pad_cols(kr, MLA_NOPE, HEAD_PAD)
    kr2 = _pad_cols(_rope_partner(kr), MLA_NOPE, HEAD_PAD)
    mgp = _pad_cols(mg, 0, LANES)
    return jnp.concatenate([cq, ckv, kr1, kr2, nq, nk, nv, mq, mk, mv, mo, mgp, bg], axis=-1).astype(BF16)


def _prep_mla_weights(w_uq, w_ukv):
    nope, rope = w_uq[..., :MLA_NOPE], w_uq[..., MLA_NOPE:]
    wqa = _pad_cols(jnp.concatenate([nope, rope], axis=-1), 0, HEAD_PAD)
    wqb = _pad_cols(_rope_partner(rope), MLA_NOPE, HEAD_PAD)
    wk = _pad_cols(w_ukv[..., :MLA_NOPE], 0, HEAD_PAD)
    wv = _pad_cols(w_ukv[..., MLA_NOPE:], 0, HEAD_PAD)
    flat = lambda w: w.reshape(w.shape[0], MLA_HEADS * HEAD_PAD).astype(BF16)
    return flat(wqa), flat(wqb), flat(wk), flat(wv)


def _rope_tables(n_lat):
    t = np.arange(n_lat)
    pos = np.stack([t // GRID_W, t % GRID_W], axis=-1).astype(np.float64)
    nf = MLA_ROPE // 4
    inv = np.float32(ROPE_BASE) ** (-np.arange(nf, dtype=np.float32) / nf)
    ang = (pos[..., None].astype(np.float32) * inv).astype(np.float64)
    cos = np.repeat(np.cos(ang)[:, :, None, :], 2, axis=2).reshape(n_lat, MLA_ROPE)
    sin = np.repeat(np.sin(ang)[:, :, None, :], 2, axis=2).reshape(n_lat, MLA_ROPE)
    cos = np.concatenate([cos, np.ones((CTX_LEN, MLA_ROPE))], axis=0)
    sin = np.concatenate([sin, np.zeros((CTX_LEN, MLA_ROPE))], axis=0)
    T = n_lat + CTX_LEN
    z = lambda w: np.zeros((T, w))
    scale = (MLA_NOPE + MLA_ROPE) ** -0.5
    ta = scale * np.concatenate([np.ones((T, MLA_NOPE)), cos, z(HEAD_PAD - MLA_NOPE - MLA_ROPE)], axis=1)
    tb = scale * np.concatenate([z(MLA_NOPE), sin, z(HEAD_PAD - MLA_NOPE - MLA_ROPE)], axis=1)
    tc = np.concatenate([z(MLA_NOPE), cos, z(HEAD_PAD - MLA_NOPE - MLA_ROPE)], axis=1)
    ts = np.concatenate([z(MLA_NOPE), sin, z(HEAD_PAD - MLA_NOPE - MLA_ROPE)], axis=1)
    return tuple(jnp.asarray(a, F32) for a in (ta, tb, tc, ts))


def kernel(x, c, ctx, c_ctx, w_mod, b_mod, g_pre1, g_post1, g_pre2, g_post2, w_in, g_q, g_kv, w_uq, w_ukv, rpb,
           b_ml_gates, g_ml, w_br_mla, w_br_na, w_br_ml, w_out, w_router, w_e1, w_e3, w_e2):
    B, n_lat, d = x.shape
    depth = w_mod.shape[0]
    rows = n_lat // GRID_W
    assert d == D_MODEL and ctx.shape[1] == CTX_LEN and n_lat % 512 == 0 and rows >= NA_KROWS + NA_QROWS

    X = jnp.concatenate([x, ctx], axis=1)
    mod_rows = -(-(B + 1) // 8) * 8
    c_all = jnp.concatenate([c, c_ctx[None, :], jnp.zeros((mod_rows - B - 1, d), F32)], axis=0)
    mod = _modulation(c_all, w_mod, b_mod)
    tabs = _rope_tables(n_lat)
    row2 = lambda v: v.astype(F32)[None, :]

    for l in range(depth):
        ctx_out = l < depth - 1
        mod3 = mod[l][:, None, :]
        w_pad = _prep_in_weights(w_in[l])
        wqa, wqb, wk, wv = _prep_mla_weights(w_uq[l], w_ukv[l])
        (Q, K, V, nq, nk, nv, mq, mk, mv, mo, mg, bg) = _inproj(
            X, mod3, row2(g_pre1[l]), w_pad, row2(g_q[l]), row2(g_kv[l]), wqa, wqb, wk, wv, tabs, n_lat)
        a = _mla_attention(Q, K, V, n_lat)
        if ctx_out:
            a = _mla_attention(Q, K, V, n_lat, prev=a)
        bn = _neighbourhood_attention(nq, nk, nv, _na_bias_table(rpb[l], rows), n_lat, ctx_out)
        mgt = jnp.transpose(mg[:, :, :4 * ML_HEADS], (0, 2, 1))
        hf, hb = _mlstm(mq, mk, mv, mg, mgt, b_ml_gates[l], n_lat)
        wr = _pad_cols(w_router[l].astype(F32), 0, LANES)
        wrh = wr.astype(BF16)
        wrl = (wr - wrh.astype(F32)).astype(BF16)
        X, h2, aff = _merge(a, bn, hf, hb, mo, bg, X, mod3, row2(g_ml[l]), row2(g_post1[l]), row2(g_pre2[l]),
                            w_br_mla[l].astype(BF16), w_br_na[l].astype(BF16), w_br_ml[l].astype(BF16),
                            w_out[l].astype(BF16), wrh, wrl, n_lat, ctx_out)
        w1, w3, w2 = w_e1[l].astype(BF16), w_e3[l].astype(BF16), w_e2[l].astype(BF16)
        X = _moe(X, h2, aff, mod3, row2(g_post2[l]), w1, w3, w2, n_lat, 0, False)
        if ctx_out:
            X = _moe(X, h2, aff, mod3, row2(g_post2[l]), w1, w3, w2, CTX_LEN, n_lat, True)
    return X[:, :n_lat]
```

```python
import functools

import numpy as np
import jax
import jax.numpy as jnp
from jax import lax
from jax.experimental import pallas as pl
from jax.experimental.pallas import tpu as pltpu

F32 = jnp.float32
BF16 = jnp.bfloat16

D_MODEL = 1024
CTX_LEN = 256
GRID_W = 64
EPS = 1e-6

MLA_HEADS = 4
MLA_Q_RANK = 256
MLA_KV_RANK = 128
MLA_NOPE = 64
MLA_ROPE = 32
MLA_V = 64
ROPE_BASE = 10000.0

NA_HEADS = 4
NA_DIM = 64
NA_KR = 8
NA_KC = 16
NA_QROWS = 4
NA_KROWS = 12

ML_HEADS = 4
ML_QK = 64
ML_V = 128
ML_CHUNK = 128

N_EXPERTS = 16
D_EXPERT = 1024
EC_FACTOR = 2

LANES = 128
ROW_TILE = 256
PAIR = 2
HEAD_PAD = 128
NEG_BIG = -1e30

_IN_GROUPS = (("cq", 256), ("ckv", 128), ("kr1", 128), ("kr2", 128), ("nq", 256), ("nk", 256), ("nv", 256),
              ("mq", 256), ("mk", 256), ("mv", 512), ("mo", 512), ("mg", 128), ("bg", 3 * D_MODEL))
_OFF = {}
_o = 0
for _n, _w in _IN_GROUPS:
    _OFF[_n] = (_o, _o + _w)
    _o += _w
D_IN_PAD = _o


def _cparams(sem, vmem_mb):
    return pltpu.CompilerParams(dimension_semantics=sem, vmem_limit_bytes=vmem_mb * 1024 * 1024)


def _dot(a, b):
    return jnp.dot(a, b, preferred_element_type=F32)


def _dot_nt(a, b):
    return lax.dot_general(a, b, (((1,), (1,)), ((), ())), preferred_element_type=F32)


def _rms(x, g):
    return x * lax.rsqrt(jnp.mean(x * x, axis=-1, keepdims=True) + EPS) * g


def _split3(x):
    hi = x.astype(BF16)
    r1 = x - hi.astype(F32)
    mid = r1.astype(BF16)
    lo = (r1 - mid.astype(F32)).astype(BF16)
    return hi, mid, lo


def _log_sigmoid(x):
    return jnp.minimum(x, 0.0) - jnp.log(1.0 + jnp.exp(-jnp.abs(x)))


def _mod_kernel(c_ref, w_ref, b_ref, o_ref):
    cs = c_ref[...]
    cs = cs * jax.nn.sigmoid(cs)
    o_ref[0] = _dot(cs.astype(BF16), w_ref[0].astype(BF16)) + b_ref[0]


def _modulation(c_all, w_mod, b_mod):
    depth, d, n6 = w_mod.shape
    rows = c_all.shape[0]
    tn = 512
    return pl.pallas_call(
        _mod_kernel,
        grid=(depth, n6 // tn),
        in_specs=[pl.BlockSpec((rows, d), lambda l, j: (0, 0)),
                  pl.BlockSpec((1, d, tn), lambda l, j: (l, 0, j)),
                  pl.BlockSpec((1, 1, tn), lambda l, j: (l, 0, j))],
        out_specs=pl.BlockSpec((1, rows, tn), lambda l, j: (l, 0, j)),
        out_shape=jax.ShapeDtypeStruct((depth, rows, n6), F32),
        compiler_params=_cparams(("parallel", "parallel"), 32),
        name="modulation",
    )(c_all, w_mod, b_mod.reshape(depth, 1, n6))


def _inproj_kernel(x_ref, sh_ref, sc_ref, g1_ref, w_ref, gq_ref, gkv_ref, wqa_ref, wqb_ref, wk_ref, wv_ref,
                   ta_ref, tb_ref, tc_ref, ts_ref,
                   q_ref, k_ref, v_ref, nq_ref, nk_ref, nv_ref, mq_ref, mk_ref, mv_ref, mo_ref, mg_ref, bg_ref):
    nb, tm, d = x_ref.shape
    rows = nb * tm
    split = lambda a: a.reshape(nb, tm, a.shape[-1])
    h = split(_rms(x_ref[...].reshape(rows, d), g1_ref[...])) * (1.0 + sc_ref[...]) + sh_ref[...]
    hb = h.reshape(rows, d).astype(BF16)

    def proj(name, lo=None, hi=None):
        a, b = _OFF[name]
        if lo is not None:
            a, b = a + lo, a + hi
        return _dot(hb, w_ref[:, a:b])

    nq_ref[...] = split(proj("nq").astype(BF16))
    nk_ref[...] = split(proj("nk").astype(BF16))
    nv_ref[...] = split(proj("nv").astype(BF16))
    mq_ref[...] = split(proj("mq").astype(BF16))
    mk_ref[...] = split(proj("mk").astype(BF16))
    mv_ref[...] = split(proj("mv").astype(BF16))
    mo_ref[...] = split(proj("mo").astype(BF16))
    mg_ref[...] = split(proj("mg"))
    for j in range(3):
        bg_ref[:, :, j * D_MODEL:(j + 1) * D_MODEL] = split(
            proj("bg", j * D_MODEL, (j + 1) * D_MODEL).astype(BF16))

    cqn = _rms(proj("cq"), gq_ref[...]).astype(BF16)
    qa = _dot(cqn, wqa_ref[...])
    qb = _dot(cqn, wqb_ref[...])
    ta = ta_ref[...][None]
    tb = tb_ref[...][None]
    ckvn = _rms(proj("ckv"), gkv_ref[...]).astype(BF16)
    kn = _dot(ckvn, wk_ref[...])
    vv = _dot(ckvn, wv_ref[...])
    krr = split(proj("kr1")) * tc_ref[...][None] + split(proj("kr2")) * ts_ref[...][None]
    one_col = (lax.broadcasted_iota(jnp.int32, (1, 1, HEAD_PAD), 2) == MLA_V).astype(F32)
    for hh in range(MLA_HEADS):
        sl = slice(hh * HEAD_PAD, (hh + 1) * HEAD_PAD)
        q_ref[:, hh] = (split(qa[:, sl]) * ta + split(qb[:, sl]) * tb).astype(BF16)
        k_ref[:, hh] = (split(kn[:, sl]) + krr).astype(BF16)
        v_ref[:, hh] = (split(vv[:, sl]) + one_col).astype(BF16)


def _inproj(X, mod3, g_pre1, w_pad, g_q, g_kv, wqa, wqb, wk, wv, tabs, n_lat):
    B, T, d = X.shape
    tm = ROW_TILE
    nb = PAIR
    nt = T // tm
    ctx_tile = n_lat // tm

    def mrow(b, i):
        return jnp.where(i >= ctx_tile, B // nb, b)

    full = lambda shape: pl.BlockSpec(shape, lambda b, i: (0,) * len(shape))
    tab = pl.BlockSpec((tm, HEAD_PAD), lambda b, i: (i, 0))
    row_out = lambda w: pl.BlockSpec((nb, tm, w), lambda b, i: (b, i, 0))
    head_out = pl.BlockSpec((nb, MLA_HEADS, tm, HEAD_PAD), lambda b, i: (b, 0, i, 0))
    sd = jax.ShapeDtypeStruct
    outs = pl.pallas_call(
        _inproj_kernel,
        grid=(B // nb, nt),
        in_specs=[pl.BlockSpec((nb, tm, d), lambda b, i: (b, i, 0)),
                  pl.BlockSpec((nb, 1, d), lambda b, i: (mrow(b, i), 0, 0)),
                  pl.BlockSpec((nb, 1, d), lambda b, i: (mrow(b, i), 0, 1)),
                  full((1, d)), full((d, D_IN_PAD)), full((1, MLA_Q_RANK)), full((1, MLA_KV_RANK)),
                  full((MLA_Q_RANK, MLA_HEADS * HEAD_PAD)), full((MLA_Q_RANK, MLA_HEADS * HEAD_PAD)),
                  full((MLA_KV_RANK, MLA_HEADS * HEAD_PAD)), full((MLA_KV_RANK, MLA_HEADS * HEAD_PAD)),
                  tab, tab, tab, tab],
        out_specs=[head_out, head_out, head_out,
                   row_out(256), row_out(256), row_out(256), row_out(256), row_out(256),
                   row_out(512), row_out(512), row_out(128), row_out(3 * d)],
        out_shape=[sd((B, MLA_HEADS, T, HEAD_PAD), BF16)] * 3
                  + [sd((B, T, 256), BF16)] * 5 + [sd((B, T, 512), BF16)] * 2
                  + [sd((B, T, 128), F32), sd((B, T, 3 * d), BF16)],
        compiler_params=_cparams(("parallel", "parallel"), 56),
        name="inproj",
    )(X, mod3, mod3, g_pre1, w_pad, g_q, g_kv, wqa, wqb, wk, wv, *tabs)
    return outs


def _flash_kernel(q_ref, k_ref, v_ref, o_ref, s_ref, *, nk, kc):
    tq = q_ref.shape[2]
    nchunks = nk // kc
    for hh in range(MLA_HEADS):
        q = q_ref[0, hh]

        def scores(c, mx):
            ks = k_ref[0, hh, pl.ds(pl.multiple_of(c * kc, kc), kc), :]
            s = _dot_nt(q, ks)
            s_ref[c] = s
            for j in range(kc // LANES):
                mx = jnp.maximum(mx, s[:, j * LANES:(j + 1) * LANES])
            return mx

        mx = lax.fori_loop(0, nchunks, scores, jnp.full((tq, LANES), -jnp.inf, F32), unroll=True)
        m = jnp.max(mx, axis=1, keepdims=True)

        def weighted(c, acc):
            p = jnp.exp2(s_ref[c] - m).astype(BF16)
            vs = v_ref[0, hh, pl.ds(pl.multiple_of(c * kc, kc), kc), :]
            return acc + _dot(p, vs)

        acc = lax.fori_loop(0, nchunks, weighted, jnp.zeros((tq, HEAD_PAD), F32), unroll=True)
        o = acc[:, :MLA_V] / acc[:, MLA_V:MLA_V + 1]
        o_ref[0, :, hh * MLA_V:(hh + 1) * MLA_V] = o.astype(BF16)


def _mla_attention(Q, K, V, n_lat, prev=None):
    B, H, T, E = Q.shape
    kc = 256
    sd = jax.ShapeDtypeStruct
    if prev is None:
        tq = 512
        nk = T
        grid = (B, n_lat // tq)
        q_spec = pl.BlockSpec((1, H, tq, E), lambda b, i: (b, 0, i, 0))
        kv_spec = pl.BlockSpec((1, H, T, E), lambda b, i: (b, 0, 0, 0))
        o_spec = pl.BlockSpec((1, tq, H * MLA_V), lambda b, i: (b, i, 0))
        extra_in, extra_specs, aliases = (), [], {}
    else:
        tq = CTX_LEN
        nk = CTX_LEN
        cblk = n_lat // CTX_LEN
        grid = (B, 1)
        q_spec = pl.BlockSpec((1, H, tq, E), lambda b, i: (b, 0, cblk, 0))
        kv_spec = pl.BlockSpec((1, H, nk, E), lambda b, i: (b, 0, cblk, 0))
        o_spec = pl.BlockSpec((1, tq, H * MLA_V), lambda b, i: (b, cblk, 0))
        extra_in, extra_specs, aliases = (prev,), [pl.BlockSpec(memory_space=pl.ANY)], {3: 0}

    def body(q_ref, k_ref, v_ref, *rest):
        o_ref, s_ref = rest[-2], rest[-1]
        _flash_kernel(q_ref, k_ref, v_ref, o_ref, s_ref, nk=nk, kc=kc)

    return pl.pallas_call(
        body,
        grid=grid,
        in_specs=[q_spec, kv_spec, kv_spec] + extra_specs,
        out_specs=o_spec,
        out_shape=sd((B, T, H * MLA_V), BF16),
        scratch_shapes=[pltpu.VMEM((nk // kc, tq, kc), F32)],
        input_output_aliases=aliases,
        compiler_params=_cparams(("parallel", "parallel"), 56),
        name="mla_attention" if prev is None else "mla_attention_ctx",
    )(Q, K, V, *extra_in)


def _na_kernel(q_ref, k_ref, v_ref, bias_ref, o_ref, *, n_lat, rows, with_ctx):
    j = pl.program_id(1)
    nblk = rows // NA_QROWS
    scale = NA_DIM ** -0.5
    q = q_ref[0]
    kctx = k_ref[0, n_lat:n_lat + CTX_LEN, :]
    vctx = v_ref[0, n_lat:n_lat + CTX_LEN, :]

    def local_step():
        kb = jnp.clip(j * NA_QROWS - NA_KR // 2, 0, rows - NA_KROWS)
        start = pl.multiple_of(kb * GRID_W, GRID_W)
        kloc = k_ref[0, pl.ds(start, NA_KROWS * GRID_W), :]
        vloc = v_ref[0, pl.ds(start, NA_KROWS * GRID_W), :]
        for hh in range(NA_HEADS):
            sl = slice(hh * NA_DIM, (hh + 1) * NA_DIM)
            qh = q[:, sl]
            s_loc = _dot_nt(qh, kloc[:, sl]) * scale + bias_ref[0, hh]
            s_ctx = _dot_nt(qh, kctx[:, sl]) * scale
            m = jnp.maximum(jnp.max(s_loc, axis=1, keepdims=True), jnp.max(s_ctx, axis=1, keepdims=True))
            p_loc = jnp.exp(s_loc - m)
            p_ctx = jnp.exp(s_ctx - m)
            den = jnp.sum(p_loc, axis=1, keepdims=True) + jnp.sum(p_ctx, axis=1, keepdims=True)
            o = _dot(p_loc.astype(BF16), vloc[:, sl]) + _dot(p_ctx.astype(BF16), vctx[:, sl])
            o_ref[0, :, sl] = (o / den).astype(BF16)

    def ctx_step():
        for hh in range(NA_HEADS):
            sl = slice(hh * NA_DIM, (hh + 1) * NA_DIM)
            s = _dot_nt(q[:, sl], kctx[:, sl]) * scale
            p = jnp.exp(s - jnp.max(s, axis=1, keepdims=True))
            den = jnp.sum(p, axis=1, keepdims=True)
            o_ref[0, :, sl] = (_dot(p.astype(BF16), vctx[:, sl]) / den).astype(BF16)

    if with_ctx:
        pl.when(j < nblk)(local_step)
        pl.when(j == nblk)(ctx_step)
    else:
        local_step()


def _na_bias_table(rpb, rows):
    W = GRID_W
    nr, ncol = 2 * NA_KR - 1, 2 * NA_KC - 1
    w = np.arange(W)[:, None]
    c = np.arange(W)[None, :]
    cs = np.clip(w - NA_KC // 2, 0, W - NA_KC)
    col_ok = (c >= cs) & (c < cs + NA_KC)
    sel_c = (col_ok[:, :, None] & ((c - w + NA_KC - 1)[:, :, None] == np.arange(ncol))).astype(np.float32)
    sel_r = np.zeros((3, NA_QROWS, NA_KROWS, nr), np.float32)
    for p, r0 in enumerate((0, NA_QROWS, rows - NA_QROWS)):
        kb = int(np.clip(r0 - NA_KR // 2, 0, rows - NA_KROWS))
        for q in range(NA_QROWS):
            rs = int(np.clip(r0 + q - NA_KR // 2, 0, rows - NA_KR))
            for k in range(NA_KROWS):
                if rs <= kb + k < rs + NA_KR:
                    sel_r[p, q, k, kb + k - (r0 + q) + NA_KR - 1] = 1.0
    hi = lax.Precision.HIGHEST
    by_col = jnp.einsum("hrd,wcd->hrwc", rpb.astype(F32), jnp.asarray(sel_c), precision=hi)
    bias = jnp.einsum("pqkr,hrwc->phqwkc", jnp.asarray(sel_r), by_col, precision=hi)
    valid = np.einsum("pqk,wc->pqwkc", sel_r.sum(-1), col_ok.astype(np.float32)) > 0
    bias = jnp.where(jnp.asarray(valid)[:, None], bias, NEG_BIG)
    return bias.reshape(3, NA_HEADS, NA_QROWS * W, NA_KROWS * W)


def _neighbourhood_attention(nq, nk, nv, bias_tab, n_lat, with_ctx):
    B, T, w = nq.shape
    rows = n_lat // GRID_W
    nblk = rows // NA_QROWS
    tq = NA_QROWS * GRID_W
    steps = nblk + (1 if with_ctx else 0)

    def pat(j):
        return jnp.where(j == 0, 0, jnp.where(j >= nblk - 1, 2, 1))

    return pl.pallas_call(
        functools.partial(_na_kernel, n_lat=n_lat, rows=rows, with_ctx=with_ctx),
        grid=(B, steps),
        in_specs=[pl.BlockSpec((1, tq, w), lambda b, j: (b, j, 0)),
                  pl.BlockSpec((1, T, w), lambda b, j: (b, 0, 0)),
                  pl.BlockSpec((1, T, w), lambda b, j: (b, 0, 0)),
                  pl.BlockSpec((1, NA_HEADS, tq, NA_KROWS * GRID_W), lambda b, j: (pat(j), 0, 0, 0))],
        out_specs=pl.BlockSpec((1, tq, w), lambda b, j: (b, j, 0)),
        out_shape=jax.ShapeDtypeStruct((B, T, w), BF16),
        compiler_params=_cparams(("parallel", "arbitrary"), 48),
        name="neighbourhood_attention",
    )(nq, nk, nv, bias_tab)


def _mlstm_kernel(qf_ref, kf_ref, vf_ref, gf_ref, gtf_ref, qb_ref, kb_ref, vb_ref, gb_ref, gtb_ref,
                  bcol_ref, brow_ref, hf_ref, hb_ref, st_ref, m_ref):
    L = ML_CHUNK

    @pl.when(pl.program_id(1) == 0)
    def _():
        st_ref[...] = jnp.zeros_like(st_ref)
        m_ref[...] = jnp.zeros_like(m_ref)

    r_i = lax.broadcasted_iota(jnp.int32, (L, L), 0)
    c_i = lax.broadcasted_iota(jnp.int32, (L, L), 1)
    ones_v = jnp.ones((L, ML_V), BF16)
    dirs = ((qf_ref, kf_ref, vf_ref, gf_ref, gtf_ref, hf_ref), (qb_ref, kb_ref, vb_ref, gb_ref, gtb_ref, hb_ref))
    nb = qf_ref.shape[0]
    for bi, d in [(bi, d) for bi in range(nb) for d in range(2)]:
        q_ref, k_ref, v_ref, g_ref, gt_ref, h_ref = dirs[d]
        causal = (c_i <= r_i) if d == 0 else (c_i >= r_i)
        m_col = jnp.where(causal, 1.0, 0.0).astype(BF16)
        m_row = jnp.where((r_i <= c_i) if d == 0 else (r_i >= c_i), 1.0, 0.0).astype(BF16)
        gt = gt_ref[bi] + bcol_ref[...]
        lf_r = _log_sigmoid(gt)
        br = sum(_dot(p, m_row) for p in _split3(lf_r))
        tot = jnp.sum(lf_r, axis=1, keepdims=True)
        gc = g_ref[bi] + brow_ref[...]
        lf_c = _log_sigmoid(gc)
        bc = sum(_dot(m_col, p) for p in _split3(lf_c))
        kf32 = k_ref[bi].astype(F32) * (ML_QK ** -0.5)
        k16 = kf32.astype(BF16)
        k_t = kf32.T
        q16 = q_ref[bi]
        v16 = v_ref[bi]
        for hh in range(ML_HEADS):
            c = (bi * 2 + d) * ML_HEADS + hh
            ji = (2 * d) * ML_HEADS + hh
            jf = (2 * d + 1) * ML_HEADS + hh
            i_row = gt[ji:ji + 1, :]
            b_row = br[jf:jf + 1, :]
            b_last = tot[jf:jf + 1, :]
            b_col = bc[:, jf:jf + 1]
            m_old = m_ref[c:c + 1, 0:1]
            st_old = st_ref[c]
            qh = q16[:, hh * ML_QK:(hh + 1) * ML_QK]
            kh = k16[:, hh * ML_QK:(hh + 1) * ML_QK]
            v_aug = jnp.concatenate([v16[:, hh * ML_V:(hh + 1) * ML_V], ones_v], axis=1)
            d_log = jnp.where(causal, b_col - b_row + i_row, -jnp.inf)
            inter = b_col + m_old
            m_t = jnp.maximum(inter, jnp.max(d_log, axis=1, keepdims=True))
            a = jnp.exp(inter - m_t)
            smat = (_dot_nt(qh, kh) * jnp.exp(d_log - m_t)).astype(BF16)
            out = a * _dot(qh, st_old.astype(BF16)) + _dot(smat, v_aug)
            den = jnp.maximum(jnp.abs(out[:, ML_V:]), jnp.exp(-m_t))
            h_ref[bi, :, hh * ML_V:(hh + 1) * ML_V] = out[:, :ML_V] / den
            w_log = b_last - b_row + i_row
            m_new = jnp.maximum(b_last + m_old, jnp.max(w_log, axis=1, keepdims=True))
            decay = jnp.exp(b_last + m_old - m_new)
            kw = (k_t[hh * ML_QK:(hh + 1) * ML_QK, :] * jnp.exp(w_log - m_new)).astype(BF16)
            st_ref[c] = decay * st_old + _dot(kw, v_aug)
            m_ref[c:c + 1, :] = jnp.broadcast_to(m_new, (1, LANES))


def _mlstm(mq, mk, mv, mg, mgt, b_ml, n_lat):
    B, T, _ = mq.shape
    L = ML_CHUNK
    nl = n_lat // L
    nc = CTX_LEN // L
    steps = nl + nc

    def cf(s):
        return jnp.where(s < nc, nl + s, s - nc)

    def cb(s):
        return steps - 1 - s

    nb = PAIR

    def specs(cm):
        return [pl.BlockSpec((nb, L, ML_HEADS * ML_QK), lambda b, s: (b, cm(s), 0)),
                pl.BlockSpec((nb, L, ML_HEADS * ML_QK), lambda b, s: (b, cm(s), 0)),
                pl.BlockSpec((nb, L, ML_HEADS * ML_V), lambda b, s: (b, cm(s), 0)),
                pl.BlockSpec((nb, L, LANES), lambda b, s: (b, cm(s), 0)),
                pl.BlockSpec((nb, 4 * ML_HEADS, L), lambda b, s: (b, 0, cm(s)))]

    b_row = jnp.concatenate([b_ml.astype(F32), jnp.zeros((LANES - 4 * ML_HEADS,), F32)])[None, :]
    b_col = b_ml.astype(F32)[:, None]
    sd = jax.ShapeDtypeStruct
    n_chain = nb * 2 * ML_HEADS
    return pl.pallas_call(
        _mlstm_kernel,
        grid=(B // nb, steps),
        in_specs=specs(cf) + specs(cb) + [pl.BlockSpec((4 * ML_HEADS, 1), lambda b, s: (0, 0)),
                                          pl.BlockSpec((1, LANES), lambda b, s: (0, 0))],
        out_specs=[pl.BlockSpec((nb, L, ML_HEADS * ML_V), lambda b, s: (b, cf(s), 0)),
                   pl.BlockSpec((nb, L, ML_HEADS * ML_V), lambda b, s: (b, cb(s), 0))],
        out_shape=[sd((B, T, ML_HEADS * ML_V), F32)] * 2,
        scratch_shapes=[pltpu.VMEM((n_chain, ML_QK, 2 * ML_V), F32), pltpu.VMEM((n_chain, LANES), F32)],
        compiler_params=_cparams(("parallel", "arbitrary"), 32),
        name="mlstm",
    )(mq, mk, mv, mg, mgt, mq, mk, mv, mg, mgt, b_col, b_row)


def _merge_kernel(a_ref, bn_ref, hf_ref, hb_ref, mo_ref, bg_ref, x_ref, gt1_ref, sh2_ref, sc2_ref,
                  gml_ref, gpost_ref, gpre2_ref, wa_ref, wb_ref, wm_ref, wo_ref, wrh_ref, wrl_ref,
                  xo_ref, h2_ref, aff_ref):
    nb, tm, d = x_ref.shape
    rows = nb * tm
    flat = lambda ref, lo=None, hi=None: (ref[...] if lo is None else ref[:, :, lo:hi]).reshape(rows, -1)
    split = lambda v: v.reshape(nb, tm, v.shape[-1])
    hm = flat(hf_ref) + flat(hb_ref)
    gml = gml_ref[...]
    parts = []
    for hh in range(ML_HEADS):
        sl = slice(hh * ML_V, (hh + 1) * ML_V)
        parts.append(_rms(hm[:, sl], gml[:, sl]))
    mlo = (jnp.concatenate(parts, axis=1) * jax.nn.sigmoid(flat(mo_ref).astype(F32))).astype(BF16)
    ya = _dot(flat(a_ref), wa_ref[...])
    yb = _dot(flat(bn_ref), wb_ref[...])
    ym = _dot(mlo, wm_ref[...])
    y = (jax.nn.sigmoid(flat(bg_ref, 0, d).astype(F32)) * ya
         + jax.nn.sigmoid(flat(bg_ref, d, 2 * d).astype(F32)) * yb
         + jax.nn.sigmoid(flat(bg_ref, 2 * d, 3 * d).astype(F32)) * ym)
    out = _dot(y.astype(BF16), wo_ref[...])
    xn = x_ref[...] + gt1_ref[...] * split(_rms(out, gpost_ref[...]))
    xo_ref[...] = xn
    h2 = split(_rms(xn.reshape(rows, d), gpre2_ref[...])) * (1.0 + sc2_ref[...]) + sh2_ref[...]
    h2_ref[...] = h2.astype(BF16)
    h2 = h2.reshape(rows, d)
    hi = h2.astype(BF16)
    lo = (h2 - hi.astype(F32)).astype(BF16)
    logits = _dot(hi, wrh_ref[...]) + _dot(lo, wrh_ref[...]) + _dot(hi, wrl_ref[...])
    lane = lax.broadcasted_iota(jnp.int32, logits.shape, 1)
    logits = jnp.where(lane < N_EXPERTS, logits, -jnp.inf)
    e = jnp.exp(logits - jnp.max(logits, axis=1, keepdims=True))
    aff_ref[...] = split(e / jnp.sum(e, axis=1, keepdims=True))


def _merge(a, bn, hf, hb, mo, bg, X, mod3, g_ml, g_post1, g_pre2, wa, wb, wm, wo, wrh, wrl, n_lat, with_ctx):
    B, T, d = X.shape
    tm = ROW_TILE
    ctx_tile = n_lat // tm
    nt = ctx_tile + (1 if with_ctx else 0)

    nb = PAIR

    def mrow(b, i):
        return jnp.where(i >= ctx_tile, B // nb, b)

    full = lambda shape: pl.BlockSpec(shape, lambda b, i: (0,) * len(shape))
    row = lambda w: pl.BlockSpec((nb, tm, w), lambda b, i: (b, i, 0))
    modc = lambda ch: pl.BlockSpec((nb, 1, d), lambda b, i: (mrow(b, i), 0, ch))
    sd = jax.ShapeDtypeStruct
    return pl.pallas_call(
        _merge_kernel,
        grid=(B // nb, nt),
        in_specs=[row(256), row(256), row(512), row(512), row(512), row(3 * d), row(d),
                  modc(2), modc(3), modc(4),
                  full((1, ML_HEADS * ML_V)), full((1, d)), full((1, d)),
                  full((256, d)), full((256, d)), full((512, d)), full((d, d)), full((d, LANES)), full((d, LANES))],
        out_specs=[row(d), row(d), row(LANES)],
        out_shape=[sd((B, T, d), F32), sd((B, T, d), BF16), sd((B, T, LANES), F32)],
        compiler_params=_cparams(("parallel", "parallel"), 48),
        name="merge",
    )(a, bn, hf, hb, mo, bg, X, mod3, mod3, mod3, g_ml, g_post1, g_pre2, wa, wb, wm, wo, wrh, wrl)


def _select_kernel(aff_ref, slot_ref, *, cap):
    aff = aff_ref[0]
    n = aff.shape[1]
    bits = pltpu.bitcast(aff, jnp.int32)
    capf = jnp.float32(cap)

    def count(mask):
        return jnp.sum(jnp.where(mask, 1.0, 0.0), axis=1, keepdims=True)

    def refine(it, thr):
        cand = thr | jnp.left_shift(jnp.int32(1), 30 - it)
        return jnp.where(count(bits >= cand) >= capf, cand, thr)

    thr = lax.fori_loop(0, 31, refine, jnp.zeros((N_EXPERTS, 1), jnp.int32))
    gt = bits > thr
    eq = bits == thr
    need = capf - count(gt)
    r_i = lax.broadcasted_iota(jnp.int32, (LANES, LANES), 0)
    c_i = lax.broadcasted_iota(jnp.int32, (LANES, LANES), 1)
    before = jnp.where(r_i < c_i, 1.0, 0.0).astype(BF16)

    def excl_prefix(mask):
        carry = jnp.zeros((N_EXPERTS, 1), F32)
        outs = []
        for c in range(n // LANES):
            mc = jnp.where(mask[:, c * LANES:(c + 1) * LANES], 1.0, 0.0)
            outs.append(_dot(mc.astype(BF16), before) + carry)
            carry = carry + jnp.sum(mc, axis=1, keepdims=True)
        return jnp.concatenate(outs, axis=1)

    sel = gt | (eq & (excl_prefix(eq) < need))
    slot_ref[0] = jnp.where(sel, excl_prefix(sel), -1.0)


def _select(aff_t, cap):
    B, E, n = aff_t.shape
    return pl.pallas_call(
        functools.partial(_select_kernel, cap=cap),
        grid=(B,),
        in_specs=[pl.BlockSpec((1, E, n), lambda b: (b, 0, 0))],
        out_specs=pl.BlockSpec((1, E, n), lambda b: (b, 0, 0)),
        out_shape=jax.ShapeDtypeStruct((B, E, n), F32),
        compiler_params=_cparams(("parallel",), 32),
        name="route_select",
    )(aff_t)


def _dispatch_kernel(slot_ref, h_ref, xs_ref, *, cap, tc):
    n = h_ref.shape[1]
    srow = slot_ref[0, 0]
    sid = lax.broadcasted_iota(jnp.int32, (cap, tc), 0).astype(F32)
    acc = jnp.zeros((cap, D_MODEL), F32)
    for c in range(n // tc):
        sel = jnp.where(sid == srow[:, c * tc:(c + 1) * tc], 1.0, 0.0).astype(BF16)
        acc = acc + _dot(sel, h_ref[0, c * tc:(c + 1) * tc, :])
    xs_ref[0, 0] = acc.astype(BF16)


def _dispatch(slot4, h2, cap, n, row_blk):
    B, E = slot4.shape[:2]
    d = h2.shape[2]
    return pl.pallas_call(
        functools.partial(_dispatch_kernel, cap=cap, tc=min(n, 512)),
        grid=(B, E),
        in_specs=[pl.BlockSpec((1, 1, 1, n), lambda b, e: (b, e, 0, 0)),
                  pl.BlockSpec((1, n, d), lambda b, e: (b, row_blk, 0))],
        out_specs=pl.BlockSpec((1, 1, cap, d), lambda b, e: (b, e, 0, 0)),
        out_shape=jax.ShapeDtypeStruct((B, E, cap, d), BF16),
        compiler_params=_cparams(("parallel", "arbitrary"), 48),
        name="moe_dispatch",
    )(slot4, h2)


def _ffn_kernel(xs_ref, w1_ref, w3_ref, w2_ref, y_ref):
    x = xs_ref[0, 0]
    h1 = _dot(x, w1_ref[0])
    h3 = _dot(x, w3_ref[0])
    hid = (h1 * jax.nn.sigmoid(h1) * h3).astype(BF16)
    y_ref[0, 0] = _dot(hid, w2_ref[0]).astype(BF16)


def _expert_ffn(xs, w1, w3, w2):
    B, E, cap, d = xs.shape
    f = w1.shape[2]
    return pl.pallas_call(
        _ffn_kernel,
        grid=(E, B),
        in_specs=[pl.BlockSpec((1, 1, cap, d), lambda e, b: (b, e, 0, 0)),
                  pl.BlockSpec((1, d, f), lambda e, b: (e, 0, 0)),
                  pl.BlockSpec((1, d, f), lambda e, b: (e, 0, 0)),
                  pl.BlockSpec((1, f, d), lambda e, b: (e, 0, 0))],
        out_specs=pl.BlockSpec((1, 1, cap, d), lambda e, b: (b, e, 0, 0)),
        out_shape=jax.ShapeDtypeStruct((B, E, cap, d), BF16),
        compiler_params=_cparams(("parallel", "arbitrary"), 48),
        name="moe_expert_ffn",
    )(xs, w1, w3, w2)


def _combine_kernel(slotc_ref, aff_ref, y_ref, x_ref, gt2_ref, gpost_ref, xo_ref, acc_ref, *, cap):
    tt = acc_ref.shape[0]
    slotc = slotc_ref[0]
    aff = aff_ref[0]
    sid = lax.broadcasted_iota(jnp.int32, (tt, cap), 1).astype(F32)
    for e in range(N_EXPERTS):
        sel_t = jnp.where(sid == slotc[:, e:e + 1], 1.0, 0.0).astype(BF16)
        part = aff[:, e:e + 1] * _dot(sel_t, y_ref[0, e])
        if e == 0:
            acc_ref[...] = part
        else:
            acc_ref[...] += part
    xo_ref[0] = x_ref[0] + gt2_ref[0] * _rms(acc_ref[...], gpost_ref[...])


def _combine(slotc, aff, y, X, mod3, g_post2, n, tok_off, is_ctx, in_place):
    B, T, d = X.shape
    E, cap = y.shape[1], y.shape[2]
    tt = min(n, 512)
    blk0 = tok_off // tt
    mrow = (lambda b: B) if is_ctx else (lambda b: b)
    if in_place:
        out_spec = pl.BlockSpec((1, tt, d), lambda b, t: (b, blk0 + t, 0))
        out_shape, aliases = jax.ShapeDtypeStruct((B, T, d), F32), {3: 0}
    else:
        out_spec = pl.BlockSpec((1, tt, d), lambda b, t: (b, t, 0))
        out_shape, aliases = jax.ShapeDtypeStruct((B, n, d), F32), {}
    return pl.pallas_call(
        functools.partial(_combine_kernel, cap=cap),
        grid=(B, n // tt),
        in_specs=[pl.BlockSpec((1, tt, E), lambda b, t: (b, t, 0)),
                  pl.BlockSpec((1, tt, LANES), lambda b, t: (b, blk0 + t, 0)),
                  pl.BlockSpec((1, E, cap, d), lambda b, t: (b, 0, 0, 0)),
                  pl.BlockSpec((1, tt, d), lambda b, t: (b, blk0 + t, 0)),
                  pl.BlockSpec((1, 1, d), lambda b, t: (mrow(b), 0, 5)),
                  pl.BlockSpec((1, d), lambda b, t: (0, 0))],
        out_specs=out_spec,
        out_shape=out_shape,
        scratch_shapes=[pltpu.VMEM((tt, d), F32)],
        input_output_aliases=aliases,
        compiler_params=_cparams(("parallel", "arbitrary"), 56),
        name="moe_combine",
    )(slotc, aff, y, X, mod3, g_post2)


def _moe(X, h2, aff, mod3, g_post2, w1, w3, w2, n, tok_off, is_ctx, in_place=True):
    cap = EC_FACTOR * n // N_EXPERTS
    aff_t = jnp.transpose(aff[:, tok_off:tok_off + n, :N_EXPERTS], (0, 2, 1))
    slot = _select(aff_t, cap)
    xs = _dispatch(slot[:, :, None, :], h2, cap, n, tok_off // n)
    y = _expert_ffn(xs, w1, w3, w2)
    return _combine(jnp.transpose(slot, (0, 2, 1)), aff, y, X, mod3, g_post2, n, tok_off, is_ctx, in_place)


def _rope_partner(w):
    lead = w.shape[:-1]
    ws = w.reshape(lead + (2, 2, MLA_ROPE // 4))
    return jnp.stack([-ws[..., 1, :], ws[..., 0, :]], axis=-2).reshape(lead + (MLA_ROPE,))


def _pad_cols(w, left, total):
    return jnp.pad(w, [(0, 0)] * (w.ndim - 1) + [(left, total - left - w.shape[-1])])


def _prep_in_weights(w_in):
    sizes = (MLA_Q_RANK, MLA_KV_RANK, MLA_ROPE, 256, 256, 256, 256, 256, 512, 512, 4 * ML_HEADS, 3 * D_MODEL)
    offs = np.cumsum(sizes)[:-1].tolist()
    cq, ckv, kr, nq, nk, nv, mq, mk, mv, mo, mg, bg = jnp.split(w_in, offs, axis=-1)
    kr1 = _pad_cols(kr, MLA_NOPE, HEAD_PAD)
    kr2 = _pad_cols(_rope_partner(kr), MLA_NOPE, HEAD_PAD)
    mgp = _pad_cols(mg, 0, LANES)
    return jnp.concatenate([cq, ckv, kr1, kr2, nq, nk, nv, mq, mk, mv, mo, mgp, bg], axis=-1).astype(BF16)


def _prep_mla_weights(w_uq, w_ukv):
    nope, rope = w_uq[..., :MLA_NOPE], w_uq[..., MLA_NOPE:]
    wqa = _pad_cols(jnp.concatenate([nope, rope], axis=-1), 0, HEAD_PAD)
    wqb = _pad_cols(_rope_partner(rope), MLA_NOPE, HEAD_PAD)
    wk = _pad_cols(w_ukv[..., :MLA_NOPE], 0, HEAD_PAD)
    wv = _pad_cols(w_ukv[..., MLA_NOPE:], 0, HEAD_PAD)
    flat = lambda w: w.reshape(w.shape[0], MLA_HEADS * HEAD_PAD).astype(BF16)
    return flat(wqa), flat(wqb), flat(wk), flat(wv)


def _rope_tables(n_lat):
    t = np.arange(n_lat)
    pos = np.stack([t // GRID_W, t % GRID_W], axis=-1).astype(np.float64)
    nf = MLA_ROPE // 4
    inv = np.float32(ROPE_BASE) ** (-np.arange(nf, dtype=np.float32) / nf)
    ang = (pos[..., None].astype(np.float32) * inv).astype(np.float64)
    cos = np.repeat(np.cos(ang)[:, :, None, :], 2, axis=2).reshape(n_lat, MLA_ROPE)
    sin = np.repeat(np.sin(ang)[:, :, None, :], 2, axis=2).reshape(n_lat, MLA_ROPE)
    cos = np.concatenate([cos, np.ones((CTX_LEN, MLA_ROPE))], axis=0)
    sin = np.concatenate([sin, np.zeros((CTX_LEN, MLA_ROPE))], axis=0)
    T = n_lat + CTX_LEN
    z = lambda w: np.zeros((T, w))
    scale = (MLA_NOPE + MLA_ROPE) ** -0.5 * np.log2(np.e)
    ta = scale * np.concatenate([np.ones((T, MLA_NOPE)), cos, z(HEAD_PAD - MLA_NOPE - MLA_ROPE)], axis=1)
    tb = scale * np.concatenate([z(MLA_NOPE), sin, z(HEAD_PAD - MLA_NOPE - MLA_ROPE)], axis=1)
    tc = np.concatenate([z(MLA_NOPE), cos, z(HEAD_PAD - MLA_NOPE - MLA_ROPE)], axis=1)
    ts = np.concatenate([z(MLA_NOPE), sin, z(HEAD_PAD - MLA_NOPE - MLA_ROPE)], axis=1)
    return tuple(jnp.asarray(a, F32) for a in (ta, tb, tc, ts))


def kernel(x, c, ctx, c_ctx, w_mod, b_mod, g_pre1, g_post1, g_pre2, g_post2, w_in, g_q, g_kv, w_uq, w_ukv, rpb,
           b_ml_gates, g_ml, w_br_mla, w_br_na, w_br_ml, w_out, w_router, w_e1, w_e3, w_e2):
    B, n_lat, d = x.shape
    depth = w_mod.shape[0]
    rows = n_lat // GRID_W
    assert d == D_MODEL and ctx.shape[1] == CTX_LEN and n_lat % 512 == 0 and rows >= NA_KROWS + NA_QROWS and B % PAIR == 0

    X = jnp.concatenate([x, ctx], axis=1)
    mod_rows = -(-(B + PAIR) // 8) * 8
    c_all = jnp.concatenate([c, jnp.broadcast_to(c_ctx[None, :], (PAIR, d)),
                             jnp.zeros((mod_rows - B - PAIR, d), F32)], axis=0)
    mod = _modulation(c_all, w_mod, b_mod)
    tabs = _rope_tables(n_lat)
    row2 = lambda v: v.astype(F32)[None, :]

    for l in range(depth):
        ctx_out = l < depth - 1
        mod3 = mod[l][:, None, :]
        w_pad = _prep_in_weights(w_in[l])
        wqa, wqb, wk, wv = _prep_mla_weights(w_uq[l], w_ukv[l])
        (Q, K, V, nq, nk, nv, mq, mk, mv, mo, mg, bg) = _inproj(
            X, mod3, row2(g_pre1[l]), w_pad, row2(g_q[l]), row2(g_kv[l]), wqa, wqb, wk, wv, tabs, n_lat)
        a = _mla_attention(Q, K, V, n_lat)
        if ctx_out:
            a = _mla_attention(Q, K, V, n_lat, prev=a)
        bn = _neighbourhood_attention(nq, nk, nv, _na_bias_table(rpb[l], rows), n_lat, ctx_out)
        mgt = jnp.transpose(mg[:, :, :4 * ML_HEADS], (0, 2, 1))
        hf, hb = _mlstm(mq, mk, mv, mg, mgt, b_ml_gates[l], n_lat)
        wr = _pad_cols(w_router[l].astype(F32), 0, LANES)
        wrh = wr.astype(BF16)
        wrl = (wr - wrh.astype(F32)).astype(BF16)
        X, h2, aff = _merge(a, bn, hf, hb, mo, bg, X, mod3, row2(g_ml[l]), row2(g_post1[l]), row2(g_pre2[l]),
                            w_br_mla[l].astype(BF16), w_br_na[l].astype(BF16), w_br_ml[l].astype(BF16),
                            w_out[l].astype(BF16), wrh, wrl, n_lat, ctx_out)
        w1, w3, w2 = w_e1[l].astype(BF16), w_e3[l].astype(BF16), w_e2[l].astype(BF16)
        X = _moe(X, h2, aff, mod3, row2(g_post2[l]), w1, w3, w2, n_lat, 0, False, in_place=ctx_out)
        if ctx_out:
            X = _moe(X, h2, aff, mod3, row2(g_post2[l]), w1, w3, w2, CTX_LEN, n_lat, True)
    return X
```

```python
import functools

import numpy as np
import jax
import jax.numpy as jnp
from jax import lax
from jax.experimental import pallas as pl
from jax.experimental.pallas import tpu as pltpu

F32 = jnp.float32
BF16 = jnp.bfloat16

D_MODEL = 1024
CTX_LEN = 256
GRID_W = 64
EPS = 1e-6

MLA_HEADS = 4
MLA_Q_RANK = 256
MLA_KV_RANK = 128
MLA_NOPE = 64
MLA_ROPE = 32
MLA_V = 64
ROPE_BASE = 10000.0

NA_HEADS = 4
NA_DIM = 64
NA_KR = 8
NA_KC = 16
NA_QROWS = 4
NA_KROWS = 12

ML_HEADS = 4
ML_QK = 64
ML_V = 128
ML_CHUNK = 128

N_EXPERTS = 16
D_EXPERT = 1024
EC_FACTOR = 2

LANES = 128
ROW_TILE = 256
PAIR = 2
TOK_TILE = 256
SLOT_WIN = 64
BF16_ROWS = 16
HEAD_PAD = 128
NEG_BIG = -1e30

_IN_GROUPS = (("cq", 256), ("ckv", 128), ("kr1", 128), ("kr2", 128), ("nq", 256), ("nk", 256), ("nv", 256),
              ("mq", 256), ("mk", 256), ("mv", 512), ("mo", 512), ("mg", 128), ("bg", 3 * D_MODEL))
_OFF = {}
_o = 0
for _n, _w in _IN_GROUPS:
    _OFF[_n] = (_o, _o + _w)
    _o += _w
D_IN_PAD = _o


def _cparams(sem, vmem_mb):
    return pltpu.CompilerParams(dimension_semantics=sem, vmem_limit_bytes=vmem_mb * 1024 * 1024)


def _dot(a, b):
    return jnp.dot(a, b, preferred_element_type=F32)


def _dot_nt(a, b):
    return lax.dot_general(a, b, (((1,), (1,)), ((), ())), preferred_element_type=F32)


def _rms(x, g):
    return x * lax.rsqrt(jnp.mean(x * x, axis=-1, keepdims=True) + EPS) * g


def _split3(x):
    hi = x.astype(BF16)
    r1 = x - hi.astype(F32)
    mid = r1.astype(BF16)
    lo = (r1 - mid.astype(F32)).astype(BF16)
    return hi, mid, lo


def _log_sigmoid(x):
    return jnp.minimum(x, 0.0) - jnp.log(1.0 + jnp.exp(-jnp.abs(x)))


def _mod_kernel(c_ref, w_ref, b_ref, o_ref):
    cs = c_ref[...]
    cs = cs * jax.nn.sigmoid(cs)
    o_ref[0] = _dot(cs.astype(BF16), w_ref[0].astype(BF16)) + b_ref[0]


def _modulation(c_all, w_mod, b_mod):
    depth, d, n6 = w_mod.shape
    rows = c_all.shape[0]
    tn = 512
    return pl.pallas_call(
        _mod_kernel,
        grid=(depth, n6 // tn),
        in_specs=[pl.BlockSpec((rows, d), lambda l, j: (0, 0)),
                  pl.BlockSpec((1, d, tn), lambda l, j: (l, 0, j)),
                  pl.BlockSpec((1, 1, tn), lambda l, j: (l, 0, j))],
        out_specs=pl.BlockSpec((1, rows, tn), lambda l, j: (l, 0, j)),
        out_shape=jax.ShapeDtypeStruct((depth, rows, n6), F32),
        compiler_params=_cparams(("parallel", "parallel"), 32),
        name="modulation",
    )(c_all, w_mod, b_mod.reshape(depth, 1, n6))


def _inproj_kernel(x_ref, sh_ref, sc_ref, g1_ref, w_ref, gq_ref, gkv_ref, wqa_ref, wqb_ref, wk_ref, wv_ref,
                   ta_ref, tb_ref, tc_ref, ts_ref,
                   q_ref, k_ref, v_ref, nq_ref, nk_ref, nv_ref, mq_ref, mk_ref, mv_ref, mo_ref, mg_ref, bg_ref):
    nb, tm, d = x_ref.shape
    rows = nb * tm
    split = lambda a: a.reshape(nb, tm, a.shape[-1])
    h = split(_rms(x_ref[...].reshape(rows, d), g1_ref[...])) * (1.0 + sc_ref[...]) + sh_ref[...]
    hb = h.reshape(rows, d).astype(BF16)

    def proj(name, lo=None, hi=None):
        a, b = _OFF[name]
        if lo is not None:
            a, b = a + lo, a + hi
        return _dot(hb, w_ref[:, a:b])

    nq_ref[...] = split(proj("nq").astype(BF16))
    nk_ref[...] = split(proj("nk").astype(BF16))
    nv_ref[...] = split(proj("nv").astype(BF16))
    mq_ref[...] = split(proj("mq").astype(BF16))
    mk_ref[...] = split(proj("mk").astype(BF16))
    mv_ref[...] = split(proj("mv").astype(BF16))
    mo_ref[...] = split(proj("mo").astype(BF16))
    mg_ref[...] = split(proj("mg"))
    for j in range(3):
        bg_ref[:, :, j * D_MODEL:(j + 1) * D_MODEL] = split(
            proj("bg", j * D_MODEL, (j + 1) * D_MODEL).astype(BF16))

    cqn = _rms(proj("cq"), gq_ref[...]).astype(BF16)
    qa = _dot(cqn, wqa_ref[...])
    qb = _dot(cqn, wqb_ref[...])
    ta = ta_ref[...][None]
    tb = tb_ref[...][None]
    ckvn = _rms(proj("ckv"), gkv_ref[...]).astype(BF16)
    kn = _dot(ckvn, wk_ref[...])
    vv = _dot(ckvn, wv_ref[...])
    krr = split(proj("kr1")) * tc_ref[...][None] + split(proj("kr2")) * ts_ref[...][None]
    one_col = (lax.broadcasted_iota(jnp.int32, (1, 1, HEAD_PAD), 2) == MLA_V).astype(F32)
    for hh in range(MLA_HEADS):
        sl = slice(hh * HEAD_PAD, (hh + 1) * HEAD_PAD)
        q_ref[:, hh] = (split(qa[:, sl]) * ta + split(qb[:, sl]) * tb).astype(BF16)
        k_ref[:, hh] = (split(kn[:, sl]) + krr).astype(BF16)
        v_ref[:, hh] = (split(vv[:, sl]) + one_col).astype(BF16)


def _inproj(X, mod3, g_pre1, w_pad, g_q, g_kv, wqa, wqb, wk, wv, tabs, n_lat):
    B, T, d = X.shape
    tm = ROW_TILE
    nb = PAIR
    nt = T // tm
    ctx_tile = n_lat // tm

    def mrow(b, i):
        return jnp.where(i >= ctx_tile, B // nb, b)

    full = lambda shape: pl.BlockSpec(shape, lambda b, i: (0,) * len(shape))
    tab = pl.BlockSpec((tm, HEAD_PAD), lambda b, i: (i, 0))
    row_out = lambda w: pl.BlockSpec((nb, tm, w), lambda b, i: (b, i, 0))
    head_out = pl.BlockSpec((nb, MLA_HEADS, tm, HEAD_PAD), lambda b, i: (b, 0, i, 0))
    sd = jax.ShapeDtypeStruct
    outs = pl.pallas_call(
        _inproj_kernel,
        grid=(B // nb, nt),
        in_specs=[pl.BlockSpec((nb, tm, d), lambda b, i: (b, i, 0)),
                  pl.BlockSpec((nb, 1, d), lambda b, i: (mrow(b, i), 0, 0)),
                  pl.BlockSpec((nb, 1, d), lambda b, i: (mrow(b, i), 0, 1)),
                  full((1, d)), full((d, D_IN_PAD)), full((1, MLA_Q_RANK)), full((1, MLA_KV_RANK)),
                  full((MLA_Q_RANK, MLA_HEADS * HEAD_PAD)), full((MLA_Q_RANK, MLA_HEADS * HEAD_PAD)),
                  full((MLA_KV_RANK, MLA_HEADS * HEAD_PAD)), full((MLA_KV_RANK, MLA_HEADS * HEAD_PAD)),
                  tab, tab, tab, tab],
        out_specs=[head_out, head_out, head_out,
                   row_out(256), row_out(256), row_out(256), row_out(256), row_out(256),
                   row_out(512), row_out(512), row_out(128), row_out(3 * d)],
        out_shape=[sd((B, MLA_HEADS, T, HEAD_PAD), BF16)] * 3
                  + [sd((B, T, 256), BF16)] * 5 + [sd((B, T, 512), BF16)] * 2
                  + [sd((B, T, 128), F32), sd((B, T, 3 * d), BF16)],
        compiler_params=_cparams(("parallel", "parallel"), 56),
        name="inproj",
    )(X, mod3, mod3, g_pre1, w_pad, g_q, g_kv, wqa, wqb, wk, wv, *tabs)
    return outs


def _flash_kernel(q_ref, k_ref, v_ref, o_ref, s_ref, *, nk, kc):
    tq = q_ref.shape[2]
    nchunks = nk // kc
    for hh in range(MLA_HEADS):
        q = q_ref[0, hh]

        def scores(c, mx):
            ks = k_ref[0, hh, pl.ds(pl.multiple_of(c * kc, kc), kc), :]
            s = _dot_nt(q, ks)
            s_ref[c] = s
            for j in range(kc // LANES):
                mx = jnp.maximum(mx, s[:, j * LANES:(j + 1) * LANES])
            return mx

        mx = lax.fori_loop(0, nchunks, scores, jnp.full((tq, LANES), -jnp.inf, F32), unroll=True)
        m = jnp.max(mx, axis=1, keepdims=True)

        def weighted(c, acc):
            p = jnp.exp2(s_ref[c] - m).astype(BF16)
            vs = v_ref[0, hh, pl.ds(pl.multiple_of(c * kc, kc), kc), :]
            return acc + _dot(p, vs)

        acc = lax.fori_loop(0, nchunks, weighted, jnp.zeros((tq, HEAD_PAD), F32), unroll=True)
        o = acc[:, :MLA_V] / acc[:, MLA_V:MLA_V + 1]
        o_ref[0, :, hh * MLA_V:(hh + 1) * MLA_V] = o.astype(BF16)


def _mla_attention(Q, K, V, n_lat, context):
    B, H, T, E = Q.shape
    kc = 256
    if not context:
        tq, nk, nq = 512, T, n_lat
        q_spec = pl.BlockSpec((1, H, tq, E), lambda b, i: (b, 0, i, 0))
        kv_spec = pl.BlockSpec((1, H, T, E), lambda b, i: (b, 0, 0, 0))
    else:
        tq, nk, nq = CTX_LEN, CTX_LEN, CTX_LEN
        cblk = n_lat // CTX_LEN
        q_spec = pl.BlockSpec((1, H, tq, E), lambda b, i: (b, 0, cblk, 0))
        kv_spec = pl.BlockSpec((1, H, nk, E), lambda b, i: (b, 0, cblk, 0))
    return pl.pallas_call(
        functools.partial(_flash_kernel, nk=nk, kc=kc),
        grid=(B, nq // tq),
        in_specs=[q_spec, kv_spec, kv_spec],
        out_specs=pl.BlockSpec((1, tq, H * MLA_V), lambda b, i: (b, i, 0)),
        out_shape=jax.ShapeDtypeStruct((B, nq, H * MLA_V), BF16),
        scratch_shapes=[pltpu.VMEM((nk // kc, tq, kc), F32)],
        compiler_params=_cparams(("parallel", "parallel"), 56),
        name="mla_attention_ctx" if context else "mla_attention",
    )(Q, K, V)


def _na_kernel(q_ref, k_ref, v_ref, bias_ref, o_ref, *, n_lat, rows, with_ctx):
    j = pl.program_id(1)
    nblk = rows // NA_QROWS
    scale = NA_DIM ** -0.5
    q = q_ref[0]
    kctx = k_ref[0, n_lat:n_lat + CTX_LEN, :]
    vctx = v_ref[0, n_lat:n_lat + CTX_LEN, :]

    def local_step():
        kb = jnp.clip(j * NA_QROWS - NA_KR // 2, 0, rows - NA_KROWS)
        start = pl.multiple_of(kb * GRID_W, GRID_W)
        kloc = k_ref[0, pl.ds(start, NA_KROWS * GRID_W), :]
        vloc = v_ref[0, pl.ds(start, NA_KROWS * GRID_W), :]
        for hh in range(NA_HEADS):
            sl = slice(hh * NA_DIM, (hh + 1) * NA_DIM)
            qh = q[:, sl]
            s_loc = _dot_nt(qh, kloc[:, sl]) * scale + bias_ref[0, hh]
            s_ctx = _dot_nt(qh, kctx[:, sl]) * scale
            m = jnp.maximum(jnp.max(s_loc, axis=1, keepdims=True), jnp.max(s_ctx, axis=1, keepdims=True))
            p_loc = jnp.exp(s_loc - m)
            p_ctx = jnp.exp(s_ctx - m)
            den = jnp.sum(p_loc, axis=1, keepdims=True) + jnp.sum(p_ctx, axis=1, keepdims=True)
            o = _dot(p_loc.astype(BF16), vloc[:, sl]) + _dot(p_ctx.astype(BF16), vctx[:, sl])
            o_ref[0, :, sl] = (o / den).astype(BF16)

    def ctx_step():
        for hh in range(NA_HEADS):
            sl = slice(hh * NA_DIM, (hh + 1) * NA_DIM)
            s = _dot_nt(q[:, sl], kctx[:, sl]) * scale
            p = jnp.exp(s - jnp.max(s, axis=1, keepdims=True))
            den = jnp.sum(p, axis=1, keepdims=True)
            o_ref[0, :, sl] = (_dot(p.astype(BF16), vctx[:, sl]) / den).astype(BF16)

    if with_ctx:
        pl.when(j < nblk)(local_step)
        pl.when(j == nblk)(ctx_step)
    else:
        local_step()


def _na_bias_table(rpb, rows):
    W = GRID_W
    nr, ncol = 2 * NA_KR - 1, 2 * NA_KC - 1
    w = np.arange(W)[:, None]
    c = np.arange(W)[None, :]
    cs = np.clip(w - NA_KC // 2, 0, W - NA_KC)
    col_ok = (c >= cs) & (c < cs + NA_KC)
    sel_c = (col_ok[:, :, None] & ((c - w + NA_KC - 1)[:, :, None] == np.arange(ncol))).astype(np.float32)
    sel_r = np.zeros((3, NA_QROWS, NA_KROWS, nr), np.float32)
    for p, r0 in enumerate((0, NA_QROWS, rows - NA_QROWS)):
        kb = int(np.clip(r0 - NA_KR // 2, 0, rows - NA_KROWS))
        for q in range(NA_QROWS):
            rs = int(np.clip(r0 + q - NA_KR // 2, 0, rows - NA_KR))
            for k in range(NA_KROWS):
                if rs <= kb + k < rs + NA_KR:
                    sel_r[p, q, k, kb + k - (r0 + q) + NA_KR - 1] = 1.0
    hi = lax.Precision.HIGHEST
    by_col = jnp.einsum("hrd,wcd->hrwc", rpb.astype(F32), jnp.asarray(sel_c), precision=hi)
    bias = jnp.einsum("pqkr,hrwc->phqwkc", jnp.asarray(sel_r), by_col, precision=hi)
    valid = np.einsum("pqk,wc->pqwkc", sel_r.sum(-1), col_ok.astype(np.float32)) > 0
    bias = jnp.where(jnp.asarray(valid)[:, None], bias, NEG_BIG)
    return bias.reshape(3, NA_HEADS, NA_QROWS * W, NA_KROWS * W)


def _neighbourhood_attention(nq, nk, nv, bias_tab, n_lat, with_ctx):
    B, T, w = nq.shape
    rows = n_lat // GRID_W
    nblk = rows // NA_QROWS
    tq = NA_QROWS * GRID_W
    steps = nblk + (1 if with_ctx else 0)

    def pat(j):
        return jnp.where(j == 0, 0, jnp.where(j >= nblk - 1, 2, 1))

    return pl.pallas_call(
        functools.partial(_na_kernel, n_lat=n_lat, rows=rows, with_ctx=with_ctx),
        grid=(B, steps),
        in_specs=[pl.BlockSpec((1, tq, w), lambda b, j: (b, j, 0)),
                  pl.BlockSpec((1, T, w), lambda b, j: (b, 0, 0)),
                  pl.BlockSpec((1, T, w), lambda b, j: (b, 0, 0)),
                  pl.BlockSpec((1, NA_HEADS, tq, NA_KROWS * GRID_W), lambda b, j: (pat(j), 0, 0, 0))],
        out_specs=pl.BlockSpec((1, tq, w), lambda b, j: (b, j, 0)),
        out_shape=jax.ShapeDtypeStruct((B, T, w), BF16),
        compiler_params=_cparams(("parallel", "arbitrary"), 48),
        name="neighbourhood_attention",
    )(nq, nk, nv, bias_tab)


def _mlstm_kernel(qf_ref, kf_ref, vf_ref, gf_ref, gtf_ref, qb_ref, kb_ref, vb_ref, gb_ref, gtb_ref,
                  bcol_ref, brow_ref, hf_ref, hb_ref, st_ref, m_ref):
    L = ML_CHUNK

    @pl.when(pl.program_id(1) == 0)
    def _():
        st_ref[...] = jnp.zeros_like(st_ref)
        m_ref[...] = jnp.zeros_like(m_ref)

    r_i = lax.broadcasted_iota(jnp.int32, (L, L), 0)
    c_i = lax.broadcasted_iota(jnp.int32, (L, L), 1)
    ones_v = jnp.ones((L, ML_V), BF16)
    dirs = ((qf_ref, kf_ref, vf_ref, gf_ref, gtf_ref, hf_ref), (qb_ref, kb_ref, vb_ref, gb_ref, gtb_ref, hb_ref))
    nb = qf_ref.shape[0]
    for bi, d in [(bi, d) for bi in range(nb) for d in range(2)]:
        q_ref, k_ref, v_ref, g_ref, gt_ref, h_ref = dirs[d]
        causal = (c_i <= r_i) if d == 0 else (c_i >= r_i)
        m_col = jnp.where(causal, 1.0, 0.0).astype(BF16)
        m_row = jnp.where((r_i <= c_i) if d == 0 else (r_i >= c_i), 1.0, 0.0).astype(BF16)
        gt = gt_ref[bi] + bcol_ref[...]
        lf_r = _log_sigmoid(gt)
        br = sum(_dot(p, m_row) for p in _split3(lf_r))
        tot = jnp.sum(lf_r, axis=1, keepdims=True)
        gc = g_ref[bi] + brow_ref[...]
        lf_c = _log_sigmoid(gc)
        bc = sum(_dot(m_col, p) for p in _split3(lf_c))
        kf32 = k_ref[bi].astype(F32) * (ML_QK ** -0.5)
        k16 = kf32.astype(BF16)
        k_t = kf32.T
        q16 = q_ref[bi]
        v16 = v_ref[bi]
        for hh in range(ML_HEADS):
            c = (bi * 2 + d) * ML_HEADS + hh
            ji = (2 * d) * ML_HEADS + hh
            jf = (2 * d + 1) * ML_HEADS + hh
            i_row = gt[ji:ji + 1, :]
            b_row = br[jf:jf + 1, :]
            b_last = tot[jf:jf + 1, :]
            b_col = bc[:, jf:jf + 1]
            m_old = m_ref[c:c + 1, 0:1]
            st_old = st_ref[c]
            qh = q16[:, hh * ML_QK:(hh + 1) * ML_QK]
            kh = k16[:, hh * ML_QK:(hh + 1) * ML_QK]
            v_aug = jnp.concatenate([v16[:, hh * ML_V:(hh + 1) * ML_V], ones_v], axis=1)
            d_log = jnp.where(causal, b_col - b_row + i_row, -jnp.inf)
            inter = b_col + m_old
            m_t = jnp.maximum(inter, jnp.max(d_log, axis=1, keepdims=True))
            a = jnp.exp(inter - m_t)
            smat = (_dot_nt(qh, kh) * jnp.exp(d_log - m_t)).astype(BF16)
            out = a * _dot(qh, st_old.astype(BF16)) + _dot(smat, v_aug)
            den = jnp.maximum(jnp.abs(out[:, ML_V:]), jnp.exp(-m_t))
            h_ref[bi, :, hh * ML_V:(hh + 1) * ML_V] = out[:, :ML_V] / den
            w_log = b_last - b_row + i_row
            m_new = jnp.maximum(b_last + m_old, jnp.max(w_log, axis=1, keepdims=True))
            decay = jnp.exp(b_last + m_old - m_new)
            kw = (k_t[hh * ML_QK:(hh + 1) * ML_QK, :] * jnp.exp(w_log - m_new)).astype(BF16)
            st_ref[c] = decay * st_old + _dot(kw, v_aug)
            m_ref[c:c + 1, :] = jnp.broadcast_to(m_new, (1, LANES))


def _mlstm(mq, mk, mv, mg, mgt, b_ml, n_lat):
    B, T, _ = mq.shape
    L = ML_CHUNK
    nl = n_lat // L
    nc = CTX_LEN // L
    steps = nl + nc

    def cf(s):
        return jnp.where(s < nc, nl + s, s - nc)

    def cb(s):
        return steps - 1 - s

    nb = PAIR

    def specs(cm):
        return [pl.BlockSpec((nb, L, ML_HEADS * ML_QK), lambda b, s: (b, cm(s), 0)),
                pl.BlockSpec((nb, L, ML_HEADS * ML_QK), lambda b, s: (b, cm(s), 0)),
                pl.BlockSpec((nb, L, ML_HEADS * ML_V), lambda b, s: (b, cm(s), 0)),
                pl.BlockSpec((nb, L, LANES), lambda b, s: (b, cm(s), 0)),
                pl.BlockSpec((nb, 4 * ML_HEADS, L), lambda b, s: (b, 0, cm(s)))]

    b_row = jnp.concatenate([b_ml.astype(F32), jnp.zeros((LANES - 4 * ML_HEADS,), F32)])[None, :]
    b_col = b_ml.astype(F32)[:, None]
    sd = jax.ShapeDtypeStruct
    n_chain = nb * 2 * ML_HEADS
    return pl.pallas_call(
        _mlstm_kernel,
        grid=(B // nb, steps),
        in_specs=specs(cf) + specs(cb) + [pl.BlockSpec((4 * ML_HEADS, 1), lambda b, s: (0, 0)),
                                          pl.BlockSpec((1, LANES), lambda b, s: (0, 0))],
        out_specs=[pl.BlockSpec((nb, L, ML_HEADS * ML_V), lambda b, s: (b, cf(s), 0)),
                   pl.BlockSpec((nb, L, ML_HEADS * ML_V), lambda b, s: (b, cb(s), 0))],
        out_shape=[sd((B, T, ML_HEADS * ML_V), F32)] * 2,
        scratch_shapes=[pltpu.VMEM((n_chain, ML_QK, 2 * ML_V), F32), pltpu.VMEM((n_chain, LANES), F32)],
        compiler_params=_cparams(("parallel", "arbitrary"), 32),
        name="mlstm",
    )(mq, mk, mv, mg, mgt, mq, mk, mv, mg, mgt, b_col, b_row)


def _merge_kernel(a_ref, bn_ref, hf_ref, hb_ref, mo_ref, bg_ref, x_ref, gt1_ref, sh2_ref, sc2_ref,
                  gml_ref, gpost_ref, gpre2_ref, wa_ref, wb_ref, wm_ref, wo_ref, wrh_ref, wrl_ref,
                  xo_ref, h2_ref, aff_ref):
    nb, tm, d = x_ref.shape
    rows = nb * tm
    flat = lambda ref, lo=None, hi=None: (ref[...] if lo is None else ref[:, :, lo:hi]).reshape(rows, -1)
    split = lambda v: v.reshape(nb, tm, v.shape[-1])
    hm = flat(hf_ref) + flat(hb_ref)
    gml = gml_ref[...]
    parts = []
    for hh in range(ML_HEADS):
        sl = slice(hh * ML_V, (hh + 1) * ML_V)
        parts.append(_rms(hm[:, sl], gml[:, sl]))
    mlo = (jnp.concatenate(parts, axis=1) * jax.nn.sigmoid(flat(mo_ref).astype(F32))).astype(BF16)
    ya = _dot(flat(a_ref), wa_ref[...])
    yb = _dot(flat(bn_ref), wb_ref[...])
    ym = _dot(mlo, wm_ref[...])
    y = (jax.nn.sigmoid(flat(bg_ref, 0, d).astype(F32)) * ya
         + jax.nn.sigmoid(flat(bg_ref, d, 2 * d).astype(F32)) * yb
         + jax.nn.sigmoid(flat(bg_ref, 2 * d, 3 * d).astype(F32)) * ym)
    out = _dot(y.astype(BF16), wo_ref[...])
    xn = x_ref[...] + gt1_ref[...] * split(_rms(out, gpost_ref[...]))
    xo_ref[...] = xn
    h2 = split(_rms(xn.reshape(rows, d), gpre2_ref[...])) * (1.0 + sc2_ref[...]) + sh2_ref[...]
    h2_ref[...] = h2.astype(BF16)
    h2 = h2.reshape(rows, d)
    hi = h2.astype(BF16)
    lo = (h2 - hi.astype(F32)).astype(BF16)
    logits = _dot(hi, wrh_ref[...]) + _dot(lo, wrh_ref[...]) + _dot(hi, wrl_ref[...])
    lane = lax.broadcasted_iota(jnp.int32, logits.shape, 1)
    logits = jnp.where(lane < N_EXPERTS, logits, -jnp.inf)
    e = jnp.exp(logits - jnp.max(logits, axis=1, keepdims=True))
    aff_ref[...] = split(e / jnp.sum(e, axis=1, keepdims=True))


def _merge(a, bn, hf, hb, mo, bg, X, mod3, g_ml, g_post1, g_pre2, wa, wb, wm, wo, wrh, wrl, n_lat, with_ctx):
    B, T, d = X.shape
    tm = ROW_TILE
    ctx_tile = n_lat // tm
    nt = ctx_tile + (1 if with_ctx else 0)

    nb = PAIR

    def mrow(b, i):
        return jnp.where(i >= ctx_tile, B // nb, b)

    full = lambda shape: pl.BlockSpec(shape, lambda b, i: (0,) * len(shape))
    row = lambda w: pl.BlockSpec((nb, tm, w), lambda b, i: (b, i, 0))
    modc = lambda ch: pl.BlockSpec((nb, 1, d), lambda b, i: (mrow(b, i), 0, ch))
    sd = jax.ShapeDtypeStruct
    return pl.pallas_call(
        _merge_kernel,
        grid=(B // nb, nt),
        in_specs=[row(256), row(256), row(512), row(512), row(512), row(3 * d), row(d),
                  modc(2), modc(3), modc(4),
                  full((1, ML_HEADS * ML_V)), full((1, d)), full((1, d)),
                  full((256, d)), full((256, d)), full((512, d)), full((d, d)), full((d, LANES)), full((d, LANES))],
        out_specs=[row(d), row(d), row(LANES)],
        out_shape=[sd((B, T, d), F32), sd((B, T, d), BF16), sd((B, T, LANES), F32)],
        compiler_params=_cparams(("parallel", "parallel"), 48),
        name="merge",
    )(a, bn, hf, hb, mo, bg, X, mod3, mod3, mod3, g_ml, g_post1, g_pre2, wa, wb, wm, wo, wrh, wrl)


def _select_kernel(aff_ref, slot_ref, start_ref, *, cap):
    aff = aff_ref[0]
    n = aff.shape[1]
    bits = pltpu.bitcast(aff, jnp.int32)
    capf = jnp.float32(cap)

    def count(mask):
        return jnp.sum(jnp.where(mask, 1.0, 0.0), axis=1, keepdims=True)

    def refine(it, thr):
        cand = thr | jnp.left_shift(jnp.int32(1), 30 - it)
        return jnp.where(count(bits >= cand) >= capf, cand, thr)

    thr = lax.fori_loop(0, 31, refine, jnp.zeros((N_EXPERTS, 1), jnp.int32))
    gt = bits > thr
    eq = bits == thr
    need = capf - count(gt)
    r_i = lax.broadcasted_iota(jnp.int32, (LANES, LANES), 0)
    c_i = lax.broadcasted_iota(jnp.int32, (LANES, LANES), 1)
    before = jnp.where(r_i < c_i, 1.0, 0.0).astype(BF16)

    def excl_prefix(mask):
        carry = jnp.zeros((N_EXPERTS, 1), F32)
        outs, carries = [], []
        for c in range(n // LANES):
            carries.append(carry)
            mc = jnp.where(mask[:, c * LANES:(c + 1) * LANES], 1.0, 0.0)
            outs.append(_dot(mc.astype(BF16), before) + carry)
            carry = carry + jnp.sum(mc, axis=1, keepdims=True)
        return jnp.concatenate(outs, axis=1), carries + [carry]

    sel = gt | (eq & (excl_prefix(eq)[0] < need))
    rank, carries = excl_prefix(sel)
    slot_ref[0] = jnp.where(sel, rank, -1.0)
    lane = lax.broadcasted_iota(jnp.int32, (N_EXPERTS, LANES), 1)
    per_tile = TOK_TILE // LANES
    starts = jnp.zeros((N_EXPERTS, LANES), F32)
    for k in range(n // TOK_TILE + 1):
        starts = jnp.where(lane == k, carries[k * per_tile], starts)
    start_ref[0] = starts


def _select(aff_t, cap):
    B, E, n = aff_t.shape
    slot, starts = pl.pallas_call(
        functools.partial(_select_kernel, cap=cap),
        grid=(B,),
        in_specs=[pl.BlockSpec((1, E, n), lambda b: (b, 0, 0))],
        out_specs=[pl.BlockSpec((1, E, n), lambda b: (b, 0, 0)), pl.BlockSpec((1, E, LANES), lambda b: (b, 0, 0))],
        out_shape=[jax.ShapeDtypeStruct((B, E, n), F32), jax.ShapeDtypeStruct((B, E, LANES), F32)],
        compiler_params=_cparams(("parallel",), 32),
        name="route_select",
    )(aff_t)
    return slot, starts[:, :, :n // TOK_TILE + 1].astype(jnp.int32)


def _log2(v):
    assert v & (v - 1) == 0
    return v.bit_length() - 1


def _window_plan(start_ref, b, kt, cap, win):
    firsts, rounds = [], jnp.int32(0)
    for e in range(N_EXPERTS):
        lo = start_ref[b, e, kt]
        hi = start_ref[b, e, kt + 1]
        first = jnp.minimum(lax.shift_right_logical(lo, _log2(BF16_ROWS)) * BF16_ROWS, cap - win)
        firsts.append(first)
        rounds = jnp.maximum(rounds, lax.shift_right_logical(hi - first + (win - 1), _log2(win)))
    return firsts, rounds


def _dispatch_kernel(start_ref, slot_ref, h_ref, xs_ref, *, cap, win):
    b, kt = pl.program_id(0), pl.program_id(1)

    @pl.when(kt == 0)
    def _():
        xs_ref[...] = jnp.zeros_like(xs_ref)

    slot = slot_ref[0]
    tile = slot.shape[1]
    h = h_ref[0]
    firsts, rounds = _window_plan(start_ref, b, kt, cap, win)
    sub = lax.broadcasted_iota(jnp.int32, (win, tile), 0)

    def one_round(r, carry):
        rows, pieces = [], []
        for e in range(N_EXPERTS):
            base = firsts[e] + r * win
            row0 = pl.multiple_of(jnp.minimum(base, cap - win), BF16_ROWS)
            abs_slot = row0 + sub
            hit = (abs_slot.astype(F32) == slot[e:e + 1, :]) & (abs_slot >= base)
            pieces.append(jnp.where(hit, 1.0, 0.0).astype(BF16))
            rows.append(row0)
        res = _dot(jnp.concatenate(pieces, axis=0), h)
        for e in range(N_EXPERTS):
            cur = xs_ref[0, e, pl.ds(rows[e], win), :]
            xs_ref[0, e, pl.ds(rows[e], win), :] = cur + res[e * win:(e + 1) * win].astype(BF16)
        return carry

    lax.fori_loop(0, rounds, one_round, 0)


def _dispatch(starts, slot, h2, cap, n, tok_off):
    B, E, _ = slot.shape
    d = h2.shape[2]
    blk0 = tok_off // TOK_TILE
    grid_spec = pltpu.PrefetchScalarGridSpec(
        num_scalar_prefetch=1,
        grid=(B, n // TOK_TILE),
        in_specs=[pl.BlockSpec((1, E, TOK_TILE), lambda b, k, st: (b, 0, k)),
                  pl.BlockSpec((1, TOK_TILE, d), lambda b, k, st: (b, blk0 + k, 0))],
        out_specs=pl.BlockSpec((1, E, cap, d), lambda b, k, st: (b, 0, 0, 0)))
    return pl.pallas_call(
        functools.partial(_dispatch_kernel, cap=cap, win=min(SLOT_WIN, cap)),
        grid_spec=grid_spec,
        out_shape=jax.ShapeDtypeStruct((B, E, cap, d), BF16),
        compiler_params=_cparams(("parallel", "arbitrary"), 56),
        name="moe_dispatch",
    )(starts, slot, h2)


def _ffn_kernel(*refs, with_ctx):
    if with_ctx:
        xs_ref, xc_ref, w1_ref, w3_ref, w2_ref, y_ref, yc_ref, w1b, w3b, w2b = refs
    else:
        xs_ref, w1_ref, w3_ref, w2_ref, y_ref, w1b, w3b, w2b = refs

    @pl.when(pl.program_id(1) == 0)
    def _():
        step = 256
        for src, dst in ((w1_ref, w1b), (w3_ref, w3b), (w2_ref, w2b)):
            for r in range(0, src.shape[1], step):
                dst[r:r + step, :] = src[0, r:r + step, :].astype(BF16)

    x = xs_ref[0, 0]
    cap = x.shape[0]
    if with_ctx:
        x = jnp.concatenate([x, xc_ref[0, 0]], axis=0)
    h1 = _dot(x, w1b[...])
    h3 = _dot(x, w3b[...])
    hid = (h1 * jax.nn.sigmoid(h1) * h3).astype(BF16)
    y = _dot(hid, w2b[...]).astype(BF16)
    y_ref[0, 0] = y[:cap]
    if with_ctx:
        yc_ref[0, 0] = y[cap:]


def _expert_ffn(xs, xs_ctx, w1, w3, w2):
    B, E, cap, d = xs.shape
    f = w1.shape[2]
    slots = lambda c: pl.BlockSpec((1, 1, c, d), lambda e, b: (b, e, 0, 0))
    x_in, x_specs = [xs], [slots(cap)]
    out_specs, out_shape = [slots(cap)], [jax.ShapeDtypeStruct((B, E, cap, d), BF16)]
    if xs_ctx is not None:
        cap_c = xs_ctx.shape[2]
        x_in.append(xs_ctx)
        x_specs.append(slots(cap_c))
        out_specs.append(slots(cap_c))
        out_shape.append(jax.ShapeDtypeStruct((B, E, cap_c, d), BF16))
    outs = pl.pallas_call(
        functools.partial(_ffn_kernel, with_ctx=xs_ctx is not None),
        grid=(E, B),
        in_specs=x_specs + [pl.BlockSpec((1, d, f), lambda e, b: (e, 0, 0)),
                            pl.BlockSpec((1, d, f), lambda e, b: (e, 0, 0)),
                            pl.BlockSpec((1, f, d), lambda e, b: (e, 0, 0))],
        out_specs=out_specs,
        out_shape=out_shape,
        scratch_shapes=[pltpu.VMEM((d, f), BF16), pltpu.VMEM((d, f), BF16), pltpu.VMEM((f, d), BF16)],
        compiler_params=_cparams(("arbitrary", "arbitrary"), 56),
        name="moe_expert_ffn",
    )(*x_in, w1, w3, w2)
    return outs if xs_ctx is not None else (outs[0], None)


def _combine_kernel(start_ref, slotc_ref, aff_ref, y_ref, x_ref, gt2_ref, gpost_ref, xo_ref, acc_ref, *, cap, win):
    b, kt = pl.program_id(0), pl.program_id(1)
    slotc = slotc_ref[0]
    aff = aff_ref[0]
    tile = slotc.shape[0]
    firsts, rounds = _window_plan(start_ref, b, kt, cap, win)
    group = LANES // win
    lane = lax.broadcasted_iota(jnp.int32, (tile, LANES), 1)
    seg = lax.shift_right_logical(lane, _log2(win))
    off = lane - seg * win
    acc_ref[...] = jnp.zeros_like(acc_ref)

    def one_round(r, carry):
        hi_p, lo_p, rhs_p = [], [], []
        for j in range(N_EXPERTS // group):
            row0v = jnp.zeros((tile, LANES), jnp.int32)
            basev = jnp.zeros((tile, LANES), jnp.int32)
            scol = jnp.zeros((tile, LANES), F32)
            gate = jnp.zeros((tile, LANES), F32)
            for i in range(group):
                e = j * group + i
                base = firsts[e] + r * win
                row0 = pl.multiple_of(jnp.minimum(base, cap - win), BF16_ROWS)
                mine = seg == i
                row0v = jnp.where(mine, row0, row0v)
                basev = jnp.where(mine, base, basev)
                scol = jnp.where(mine, slotc[:, e:e + 1], scol)
                gate = jnp.where(mine, aff[:, e:e + 1], gate)
                rhs_p.append(y_ref[0, e, pl.ds(row0, win), :])
            abs_slot = row0v + off
            hit = (abs_slot.astype(F32) == scol) & (abs_slot >= basev)
            g = jnp.where(hit, gate, 0.0)
            g_hi = g.astype(BF16)
            hi_p.append(g_hi)
            lo_p.append((g - g_hi.astype(F32)).astype(BF16))
        lhs = jnp.concatenate([jnp.concatenate(hi_p, axis=1), jnp.concatenate(lo_p, axis=1)], axis=0)
        res = _dot(lhs, jnp.concatenate(rhs_p, axis=0))
        acc_ref[...] += res[:tile] + res[tile:]
        return carry

    lax.fori_loop(0, rounds, one_round, 0)
    xo_ref[0] = x_ref[0] + gt2_ref[0] * _rms(acc_ref[...], gpost_ref[...])


def _combine(starts, slotc, aff, y, X, mod3, g_post2, n, tok_off, is_ctx, in_place):
    B, T, d = X.shape
    E, cap = y.shape[1], y.shape[2]
    tt = TOK_TILE
    blk0 = tok_off // tt
    mrow = (lambda b: B) if is_ctx else (lambda b: b)
    if in_place:
        out_spec = pl.BlockSpec((1, tt, d), lambda b, t, st: (b, blk0 + t, 0))
        out_shape, aliases = jax.ShapeDtypeStruct((B, T, d), F32), {4: 0}
    else:
        out_spec = pl.BlockSpec((1, tt, d), lambda b, t, st: (b, t, 0))
        out_shape, aliases = jax.ShapeDtypeStruct((B, n, d), F32), {}
    grid_spec = pltpu.PrefetchScalarGridSpec(
        num_scalar_prefetch=1,
        grid=(B, n // tt),
        in_specs=[pl.BlockSpec((1, tt, E), lambda b, t, st: (b, t, 0)),
                  pl.BlockSpec((1, tt, LANES), lambda b, t, st: (b, blk0 + t, 0)),
                  pl.BlockSpec((1, E, cap, d), lambda b, t, st: (b, 0, 0, 0)),
                  pl.BlockSpec((1, tt, d), lambda b, t, st: (b, blk0 + t, 0)),
                  pl.BlockSpec((1, 1, d), lambda b, t, st: (mrow(b), 0, 5)),
                  pl.BlockSpec((1, d), lambda b, t, st: (0, 0))],
        out_specs=out_spec,
        scratch_shapes=[pltpu.VMEM((tt, d), F32)])
    return pl.pallas_call(
        functools.partial(_combine_kernel, cap=cap, win=min(SLOT_WIN, cap)),
        grid_spec=grid_spec,
        out_shape=out_shape,
        input_output_aliases=aliases,
        compiler_params=_cparams(("parallel", "arbitrary"), 56),
        name="moe_combine",
    )(starts, slotc, aff, y, X, mod3, g_post2)


def _moe(X, h2, aff, mod3, g_post2, w1, w3, w2, n_lat, with_ctx, in_place):
    def route(n, off):
        cap = EC_FACTOR * n // N_EXPERTS
        aff_t = jnp.transpose(aff[:, off:off + n, :N_EXPERTS], (0, 2, 1))
        slot, starts = _select(aff_t, cap)
        return jnp.transpose(slot, (0, 2, 1)), starts, _dispatch(starts, slot, h2, cap, n, off)

    slotc, starts, xs = route(n_lat, 0)
    slotc_c, starts_c, xs_c = route(CTX_LEN, n_lat) if with_ctx else (None, None, None)
    y, y_c = _expert_ffn(xs, xs_c, w1, w3, w2)
    X = _combine(starts, slotc, aff, y, X, mod3, g_post2, n_lat, 0, False, in_place)
    if with_ctx:
        X = _combine(starts_c, slotc_c, aff, y_c, X, mod3, g_post2, CTX_LEN, n_lat, True, True)
    return X


def _rope_partner(w):
    lead = w.shape[:-1]
    ws = w.reshape(lead + (2, 2, MLA_ROPE // 4))
    return jnp.stack([-ws[..., 1, :], ws[..., 0, :]], axis=-2).reshape(lead + (MLA_ROPE,))


def _pad_cols(w, left, total):
    return jnp.pad(w, [(0, 0)] * (w.ndim - 1) + [(left, total - left - w.shape[-1])])


def _prep_in_weights(w_in):
    sizes = (MLA_Q_RANK, MLA_KV_RANK, MLA_ROPE, 256, 256, 256, 256, 256, 512, 512, 4 * ML_HEADS, 3 * D_MODEL)
    offs = np.cumsum(sizes)[:-1].tolist()
    cq, ckv, kr, nq, nk, nv, mq, mk, mv, mo, mg, bg = jnp.split(w_in, offs, axis=-1)
    kr1 = _pad_cols(kr, MLA_NOPE, HEAD_PAD)
    kr2 = _pad_cols(_rope_partner(kr), MLA_NOPE, HEAD_PAD)
    mgp = _pad_cols(mg, 0, LANES)
    return jnp.concatenate([cq, ckv, kr1, kr2, nq, nk, nv, mq, mk, mv, mo, mgp, bg], axis=-1).astype(BF16)


def _prep_mla_weights(w_uq, w_ukv):
    nope, rope = w_uq[..., :MLA_NOPE], w_uq[..., MLA_NOPE:]
    wqa = _pad_cols(jnp.concatenate([nope, rope], axis=-1), 0, HEAD_PAD)
    wqb = _pad_cols(_rope_partner(rope), MLA_NOPE, HEAD_PAD)
    wk = _pad_cols(w_ukv[..., :MLA_NOPE], 0, HEAD_PAD)
    wv = _pad_cols(w_ukv[..., MLA_NOPE:], 0, HEAD_PAD)
    flat = lambda w: w.reshape(w.shape[0], MLA_HEADS * HEAD_PAD).astype(BF16)
    return flat(wqa), flat(wqb), flat(wk), flat(wv)


def _rope_tables(n_lat):
    t = np.arange(n_lat)
    pos = np.stack([t // GRID_W, t % GRID_W], axis=-1).astype(np.float64)
    nf = MLA_ROPE // 4
    inv = np.float32(ROPE_BASE) ** (-np.arange(nf, dtype=np.float32) / nf)
    ang = (pos[..., None].astype(np.float32) * inv).astype(np.float64)
    cos = np.repeat(np.cos(ang)[:, :, None, :], 2, axis=2).reshape(n_lat, MLA_ROPE)
    sin = np.repeat(np.sin(ang)[:, :, None, :], 2, axis=2).reshape(n_lat, MLA_ROPE)
    cos = np.concatenate([cos, np.ones((CTX_LEN, MLA_ROPE))], axis=0)
    sin = np.concatenate([sin, np.zeros((CTX_LEN, MLA_ROPE))], axis=0)
    T = n_lat + CTX_LEN
    z = lambda w: np.zeros((T, w))
    scale = (MLA_NOPE + MLA_ROPE) ** -0.5 * np.log2(np.e)
    ta = scale * np.concatenate([np.ones((T, MLA_NOPE)), cos, z(HEAD_PAD - MLA_NOPE - MLA_ROPE)], axis=1)
    tb = scale * np.concatenate([z(MLA_NOPE), sin, z(HEAD_PAD - MLA_NOPE - MLA_ROPE)], axis=1)
    tc = np.concatenate([z(MLA_NOPE), cos, z(HEAD_PAD - MLA_NOPE - MLA_ROPE)], axis=1)
    ts = np.concatenate([z(MLA_NOPE), sin, z(HEAD_PAD - MLA_NOPE - MLA_ROPE)], axis=1)
    return tuple(jnp.asarray(a, F32) for a in (ta, tb, tc, ts))


def kernel(x, c, ctx, c_ctx, w_mod, b_mod, g_pre1, g_post1, g_pre2, g_post2, w_in, g_q, g_kv, w_uq, w_ukv, rpb,
           b_ml_gates, g_ml, w_br_mla, w_br_na, w_br_ml, w_out, w_router, w_e1, w_e3, w_e2):
    B, n_lat, d = x.shape
    depth = w_mod.shape[0]
    rows = n_lat // GRID_W
    assert d == D_MODEL and ctx.shape[1] == CTX_LEN and n_lat % 512 == 0 and rows >= NA_KROWS + NA_QROWS and B % PAIR == 0

    X = jnp.concatenate([x, ctx], axis=1)
    mod_rows = -(-(B + PAIR) // 8) * 8
    c_all = jnp.concatenate([c, jnp.broadcast_to(c_ctx[None, :], (PAIR, d)),
                             jnp.zeros((mod_rows - B - PAIR, d), F32)], axis=0)
    mod = _modulation(c_all, w_mod, b_mod)
    tabs = _rope_tables(n_lat)
    row2 = lambda v: v.astype(F32)[None, :]

    for l in range(depth):
        ctx_out = l < depth - 1
        mod3 = mod[l][:, None, :]
        w_pad = _prep_in_weights(w_in[l])
        wqa, wqb, wk, wv = _prep_mla_weights(w_uq[l], w_ukv[l])
        (Q, K, V, nq, nk, nv, mq, mk, mv, mo, mg, bg) = _inproj(
            X, mod3, row2(g_pre1[l]), w_pad, row2(g_q[l]), row2(g_kv[l]), wqa, wqb, wk, wv, tabs, n_lat)
        a = jnp.concatenate([_mla_attention(Q, K, V, n_lat, False), _mla_attention(Q, K, V, n_lat, True)], axis=1)
        bn = _neighbourhood_attention(nq, nk, nv, _na_bias_table(rpb[l], rows), n_lat, True)
        mgt = jnp.transpose(mg[:, :, :4 * ML_HEADS], (0, 2, 1))
        hf, hb = _mlstm(mq, mk, mv, mg, mgt, b_ml_gates[l], n_lat)
        wr = _pad_cols(w_router[l].astype(F32), 0, LANES)
        wrh = wr.astype(BF16)
        wrl = (wr - wrh.astype(F32)).astype(BF16)
        X, h2, aff = _merge(a, bn, hf, hb, mo, bg, X, mod3, row2(g_ml[l]), row2(g_post1[l]), row2(g_pre2[l]),
                            w_br_mla[l].astype(BF16), w_br_na[l].astype(BF16), w_br_ml[l].astype(BF16),
                            w_out[l].astype(BF16), wrh, wrl, n_lat, True)
        X = _moe(X, h2, aff, mod3, row2(g_post2[l]), w_e1[l], w_e3[l], w_e2[l], n_lat, ctx_out, in_place=ctx_out)
    return X
```

```python
import functools

import numpy as np
import jax
import jax.numpy as jnp
from jax import lax
from jax.experimental import pallas as pl
from jax.experimental.pallas import tpu as pltpu

F32 = jnp.float32
BF16 = jnp.bfloat16

D_MODEL = 1024
CTX_LEN = 256
GRID_W = 64
EPS = 1e-6

MLA_HEADS = 4
MLA_Q_RANK = 256
MLA_KV_RANK = 128
MLA_NOPE = 64
MLA_ROPE = 32
MLA_V = 64
ROPE_BASE = 10000.0

NA_HEADS = 4
NA_DIM = 64
NA_KR = 8
NA_KC = 16
NA_QROWS = 4
NA_KROWS = 12

ML_HEADS = 4
ML_QK = 64
ML_V = 128
ML_CHUNK = 128

N_EXPERTS = 16
D_EXPERT = 1024
EC_FACTOR = 2

LANES = 128
ROW_TILE = 256
PAIR = 2
TOK_TILE = 256
SLOT_WIN = 64
BF16_ROWS = 16
HEAD_PAD = 128
NEG_BIG = -1e30

_IN_GROUPS = (("cq", 256), ("ckv", 128), ("kr1", 128), ("kr2", 128), ("nq", 256), ("nk", 256), ("nv", 256),
              ("mq", 256), ("mk", 256), ("mv", 512), ("mo", 512), ("mg", 128), ("bg", 3 * D_MODEL))
_OFF = {}
_o = 0
for _n, _w in _IN_GROUPS:
    _OFF[_n] = (_o, _o + _w)
    _o += _w
D_IN_PAD = _o


def _cparams(sem, vmem_mb):
    return pltpu.CompilerParams(dimension_semantics=sem, vmem_limit_bytes=vmem_mb * 1024 * 1024)


def _dot(a, b):
    return jnp.dot(a, b, preferred_element_type=F32)


def _dot_nt(a, b):
    return lax.dot_general(a, b, (((1,), (1,)), ((), ())), preferred_element_type=F32)


def _rms(x, g):
    return x * lax.rsqrt(jnp.mean(x * x, axis=-1, keepdims=True) + EPS) * g


def _split3(x):
    hi = x.astype(BF16)
    r1 = x - hi.astype(F32)
    mid = r1.astype(BF16)
    lo = (r1 - mid.astype(F32)).astype(BF16)
    return hi, mid, lo


def _log_sigmoid(x):
    return jnp.minimum(x, 0.0) - jnp.log(1.0 + jnp.exp(-jnp.abs(x)))


def _mod_kernel(c_ref, w_ref, b_ref, o_ref):
    cs = c_ref[...]
    cs = cs * jax.nn.sigmoid(cs)
    o_ref[0] = _dot(cs.astype(BF16), w_ref[0].astype(BF16)) + b_ref[0]


def _modulation(c_all, w_mod, b_mod):
    depth, d, n6 = w_mod.shape
    rows = c_all.shape[0]
    tn = 512
    return pl.pallas_call(
        _mod_kernel,
        grid=(depth, n6 // tn),
        in_specs=[pl.BlockSpec((rows, d), lambda l, j: (0, 0)),
                  pl.BlockSpec((1, d, tn), lambda l, j: (l, 0, j)),
                  pl.BlockSpec((1, 1, tn), lambda l, j: (l, 0, j))],
        out_specs=pl.BlockSpec((1, rows, tn), lambda l, j: (l, 0, j)),
        out_shape=jax.ShapeDtypeStruct((depth, rows, n6), F32),
        compiler_params=_cparams(("parallel", "parallel"), 32),
        name="modulation",
    )(c_all, w_mod, b_mod.reshape(depth, 1, n6))


def _inproj_kernel(x_ref, sh_ref, sc_ref, g1_ref, w_ref, gq_ref, gkv_ref, wqa_ref, wqb_ref, wk_ref, wv_ref,
                   ta_ref, tb_ref, tc_ref, ts_ref,
                   q_ref, k_ref, v_ref, nq_ref, nk_ref, nv_ref, mq_ref, mk_ref, mv_ref, mo_ref, mg_ref, bg_ref):
    nb, tm, d = x_ref.shape
    rows = nb * tm
    split = lambda a: a.reshape(nb, tm, a.shape[-1])
    h = split(_rms(x_ref[...].reshape(rows, d), g1_ref[...])) * (1.0 + sc_ref[...]) + sh_ref[...]
    hb = h.reshape(rows, d).astype(BF16)

    def proj(name, lo=None, hi=None):
        a, b = _OFF[name]
        if lo is not None:
            a, b = a + lo, a + hi
        return _dot(hb, w_ref[:, a:b])

    nq_ref[...] = split(proj("nq")).astype(BF16)
    nk_ref[...] = split(proj("nk")).astype(BF16)
    nv_ref[...] = split(proj("nv")).astype(BF16)
    mq_ref[...] = split(proj("mq")).astype(BF16)
    mk_ref[...] = split(proj("mk")).astype(BF16)
    mv_ref[...] = split(proj("mv")).astype(BF16)
    mo_ref[...] = split(proj("mo")).astype(BF16)
    mg_ref[...] = split(proj("mg"))
    for j in range(3):
        bg_ref[:, :, j * D_MODEL:(j + 1) * D_MODEL] = split(
            proj("bg", j * D_MODEL, (j + 1) * D_MODEL)).astype(BF16)

    cqn = _rms(proj("cq"), gq_ref[...]).astype(BF16)
    qa = _dot(cqn, wqa_ref[...])
    qb = _dot(cqn, wqb_ref[...])
    ta = ta_ref[...][None]
    tb = tb_ref[...][None]
    ckvn = _rms(proj("ckv"), gkv_ref[...]).astype(BF16)
    kn = _dot(ckvn, wk_ref[...])
    vv = _dot(ckvn, wv_ref[...])
    krr = split(proj("kr1")) * tc_ref[...][None] + split(proj("kr2")) * ts_ref[...][None]
    one_col = (lax.broadcasted_iota(jnp.int32, (1, 1, HEAD_PAD), 2) == MLA_V).astype(F32)
    for hh in range(MLA_HEADS):
        sl = slice(hh * HEAD_PAD, (hh + 1) * HEAD_PAD)
        q_ref[:, hh] = (split(qa[:, sl]) * ta + split(qb[:, sl]) * tb).astype(BF16)
        k_ref[:, hh] = (split(kn[:, sl]) + krr).astype(BF16)
        v_ref[:, hh] = (split(vv[:, sl]) + one_col).astype(BF16)


def _inproj(X, mod3, g_pre1, w_pad, g_q, g_kv, wqa, wqb, wk, wv, tabs, n_lat):
    B, T, d = X.shape
    tm = ROW_TILE
    nb = PAIR
    nt = T // tm
    ctx_tile = n_lat // tm

    def mrow(b, i):
        return jnp.where(i >= ctx_tile, B // nb, b)

    full = lambda shape: pl.BlockSpec(shape, lambda b, i: (0,) * len(shape))
    tab = pl.BlockSpec((tm, HEAD_PAD), lambda b, i: (i, 0))
    row_out = lambda w: pl.BlockSpec((nb, tm, w), lambda b, i: (b, i, 0))
    head_out = pl.BlockSpec((nb, MLA_HEADS, tm, HEAD_PAD), lambda b, i: (b, 0, i, 0))
    sd = jax.ShapeDtypeStruct
    outs = pl.pallas_call(
        _inproj_kernel,
        grid=(B // nb, nt),
        in_specs=[pl.BlockSpec((nb, tm, d), lambda b, i: (b, i, 0)),
                  pl.BlockSpec((nb, 1, d), lambda b, i: (mrow(b, i), 0, 0)),
                  pl.BlockSpec((nb, 1, d), lambda b, i: (mrow(b, i), 0, 1)),
                  full((1, d)), full((d, D_IN_PAD)), full((1, MLA_Q_RANK)), full((1, MLA_KV_RANK)),
                  full((MLA_Q_RANK, MLA_HEADS * HEAD_PAD)), full((MLA_Q_RANK, MLA_HEADS * HEAD_PAD)),
                  full((MLA_KV_RANK, MLA_HEADS * HEAD_PAD)), full((MLA_KV_RANK, MLA_HEADS * HEAD_PAD)),
                  tab, tab, tab, tab],
        out_specs=[head_out, head_out, head_out,
                   row_out(256), row_out(256), row_out(256), row_out(256), row_out(256),
                   row_out(512), row_out(512), row_out(128), row_out(3 * d)],
        out_shape=[sd((B, MLA_HEADS, T, HEAD_PAD), BF16)] * 3
                  + [sd((B, T, 256), BF16)] * 5 + [sd((B, T, 512), BF16)] * 2
                  + [sd((B, T, 128), F32), sd((B, T, 3 * d), BF16)],
        compiler_params=_cparams(("parallel", "parallel"), 56),
        name="inproj",
    )(X, mod3, mod3, g_pre1, w_pad, g_q, g_kv, wqa, wqb, wk, wv, *tabs)
    return outs


def _flash_kernel(q_ref, k_ref, v_ref, o_ref, s_ref, *, nk, kc):
    tq = q_ref.shape[2]
    nchunks = nk // kc
    for hh in range(MLA_HEADS):
        q = q_ref[0, hh]

        def scores(c, mx, hh=hh, q=q):
            ks = k_ref[0, hh, pl.ds(pl.multiple_of(c * kc, kc), kc), :]
            s = _dot_nt(q, ks)
            s_ref[c] = s
            for j in range(kc // LANES):
                mx = jnp.maximum(mx, s[:, j * LANES:(j + 1) * LANES])
            return mx

        mx = lax.fori_loop(0, nchunks, scores, jnp.full((tq, LANES), -jnp.inf, F32), unroll=True)
        m = jnp.max(mx, axis=1, keepdims=True)

        def weighted(c, acc, hh=hh, m=m):
            p = jnp.exp2(s_ref[c] - m).astype(BF16)
            vs = v_ref[0, hh, pl.ds(pl.multiple_of(c * kc, kc), kc), :]
            return acc + _dot(p, vs)

        acc = lax.fori_loop(0, nchunks, weighted, jnp.zeros((tq, HEAD_PAD), F32), unroll=True)
        o = acc[:, :MLA_V] / acc[:, MLA_V:MLA_V + 1]
        o_ref[0, :, hh * MLA_V:(hh + 1) * MLA_V] = o.astype(BF16)


def _mla_attention(Q, K, V, n_lat, context):
    B, H, T, E = Q.shape
    kc = 256
    if not context:
        tq, nk, nq = 512, T, n_lat
        q_spec = pl.BlockSpec((1, H, tq, E), lambda b, i: (b, 0, i, 0))
        kv_spec = pl.BlockSpec((1, H, T, E), lambda b, i: (b, 0, 0, 0))
    else:
        tq, nk, nq = CTX_LEN, CTX_LEN, CTX_LEN
        cblk = n_lat // CTX_LEN
        q_spec = pl.BlockSpec((1, H, tq, E), lambda b, i: (b, 0, cblk, 0))
        kv_spec = pl.BlockSpec((1, H, nk, E), lambda b, i: (b, 0, cblk, 0))
    return pl.pallas_call(
        functools.partial(_flash_kernel, nk=nk, kc=kc),
        grid=(B, nq // tq),
        in_specs=[q_spec, kv_spec, kv_spec],
        out_specs=pl.BlockSpec((1, tq, H * MLA_V), lambda b, i: (b, i, 0)),
        out_shape=jax.ShapeDtypeStruct((B, nq, H * MLA_V), BF16),
        scratch_shapes=[pltpu.VMEM((nk // kc, tq, kc), F32)],
        compiler_params=_cparams(("parallel", "parallel"), 56),
        name="mla_attention_ctx" if context else "mla_attention",
    )(Q, K, V)


def _na_kernel(q_ref, k_ref, v_ref, bias_ref, o_ref, *, n_lat, rows, with_ctx):
    j = pl.program_id(1)
    nblk = rows // NA_QROWS
    scale = NA_DIM ** -0.5
    q = q_ref[0]
    kctx = k_ref[0, n_lat:n_lat + CTX_LEN, :]
    vctx = v_ref[0, n_lat:n_lat + CTX_LEN, :]

    def local_step():
        kb = jnp.clip(j * NA_QROWS - NA_KR // 2, 0, rows - NA_KROWS)
        start = pl.multiple_of(kb * GRID_W, GRID_W)
        kloc = k_ref[0, pl.ds(start, NA_KROWS * GRID_W), :]
        vloc = v_ref[0, pl.ds(start, NA_KROWS * GRID_W), :]
        for hh in range(NA_HEADS):
            sl = slice(hh * NA_DIM, (hh + 1) * NA_DIM)
            qh = q[:, sl]
            s_loc = _dot_nt(qh, kloc[:, sl]) * scale + bias_ref[0, hh]
            s_ctx = _dot_nt(qh, kctx[:, sl]) * scale
            m = jnp.maximum(jnp.max(s_loc, axis=1, keepdims=True), jnp.max(s_ctx, axis=1, keepdims=True))
            p_loc = jnp.exp(s_loc - m)
            p_ctx = jnp.exp(s_ctx - m)
            den = jnp.sum(p_loc, axis=1, keepdims=True) + jnp.sum(p_ctx, axis=1, keepdims=True)
            o = _dot(p_loc.astype(BF16), vloc[:, sl]) + _dot(p_ctx.astype(BF16), vctx[:, sl])
            o_ref[0, :, sl] = (o / den).astype(BF16)

    def ctx_step():
        for hh in range(NA_HEADS):
            sl = slice(hh * NA_DIM, (hh + 1) * NA_DIM)
            s = _dot_nt(q[:, sl], kctx[:, sl]) * scale
            p = jnp.exp(s - jnp.max(s, axis=1, keepdims=True))
            den = jnp.sum(p, axis=1, keepdims=True)
            o_ref[0, :, sl] = (_dot(p.astype(BF16), vctx[:, sl]) / den).astype(BF16)

    if with_ctx:
        pl.when(j < nblk)(local_step)
        pl.when(j == nblk)(ctx_step)
    else:
        local_step()


def _na_bias_table(rpb, rows):
    W = GRID_W
    nr, ncol = 2 * NA_KR - 1, 2 * NA_KC - 1
    w = np.arange(W)[:, None]
    c = np.arange(W)[None, :]
    cs = np.clip(w - NA_KC // 2, 0, W - NA_KC)
    col_ok = (c >= cs) & (c < cs + NA_KC)
    sel_c = (col_ok[:, :, None] & ((c - w + NA_KC - 1)[:, :, None] == np.arange(ncol))).astype(np.float32)
    sel_r = np.zeros((3, NA_QROWS, NA_KROWS, nr), np.float32)
    for p, r0 in enumerate((0, NA_QROWS, rows - NA_QROWS)):
        kb = int(np.clip(r0 - NA_KR // 2, 0, rows - NA_KROWS))
        for q in range(NA_QROWS):
            rs = int(np.clip(r0 + q - NA_KR // 2, 0, rows - NA_KR))
            for k in range(NA_KROWS):
                if rs <= kb + k < rs + NA_KR:
                    sel_r[p, q, k, kb + k - (r0 + q) + NA_KR - 1] = 1.0
    hi = lax.Precision.HIGHEST
    by_col = jnp.einsum("hrd,wcd->hrwc", rpb.astype(F32), jnp.asarray(sel_c), precision=hi)
    bias = jnp.einsum("pqkr,hrwc->phqwkc", jnp.asarray(sel_r), by_col, precision=hi)
    valid = np.einsum("pqk,wc->pqwkc", sel_r.sum(-1), col_ok.astype(np.float32)) > 0
    bias = jnp.where(jnp.asarray(valid)[:, None], bias, NEG_BIG)
    return bias.reshape(3, NA_HEADS, NA_QROWS * W, NA_KROWS * W)


def _neighbourhood_attention(nq, nk, nv, bias_tab, n_lat, with_ctx):
    B, T, w = nq.shape
    rows = n_lat // GRID_W
    nblk = rows // NA_QROWS
    tq = NA_QROWS * GRID_W
    steps = nblk + (1 if with_ctx else 0)

    def pat(j):
        return jnp.where(j == 0, 0, jnp.where(j >= nblk - 1, 2, 1))

    return pl.pallas_call(
        functools.partial(_na_kernel, n_lat=n_lat, rows=rows, with_ctx=with_ctx),
        grid=(B, steps),
        in_specs=[pl.BlockSpec((1, tq, w), lambda b, j: (b, j, 0)),
                  pl.BlockSpec((1, T, w), lambda b, j: (b, 0, 0)),
                  pl.BlockSpec((1, T, w), lambda b, j: (b, 0, 0)),
                  pl.BlockSpec((1, NA_HEADS, tq, NA_KROWS * GRID_W), lambda b, j: (pat(j), 0, 0, 0))],
        out_specs=pl.BlockSpec((1, tq, w), lambda b, j: (b, j, 0)),
        out_shape=jax.ShapeDtypeStruct((B, T, w), BF16),
        compiler_params=_cparams(("parallel", "arbitrary"), 48),
        name="neighbourhood_attention",
    )(nq, nk, nv, bias_tab)


def _mlstm_kernel(qf_ref, kf_ref, vf_ref, gf_ref, gtf_ref, qb_ref, kb_ref, vb_ref, gb_ref, gtb_ref,
                  bcol_ref, brow_ref, hf_ref, hb_ref, st_ref, m_ref):
    L = ML_CHUNK

    @pl.when(pl.program_id(1) == 0)
    def _():
        st_ref[...] = jnp.zeros_like(st_ref)
        m_ref[...] = jnp.zeros_like(m_ref)

    r_i = lax.broadcasted_iota(jnp.int32, (L, L), 0)
    c_i = lax.broadcasted_iota(jnp.int32, (L, L), 1)
    ones_t = jnp.ones((ML_V, L), BF16)
    dirs = ((qf_ref, kf_ref, vf_ref, gf_ref, gtf_ref, hf_ref), (qb_ref, kb_ref, vb_ref, gb_ref, gtb_ref, hb_ref))
    nb = qf_ref.shape[0]
    ng = 4 * ML_HEADS
    for bi, d in [(bi, d) for d in range(2) for bi in range(nb)]:
        q_ref, k_ref, v_ref, g_ref, gt_ref, h_ref = dirs[d]
        if bi == 0:
            causal_t = (r_i <= c_i) if d == 0 else (r_i >= c_i)
            m_row = jnp.where(causal_t, 1.0, 0.0).astype(BF16)
            m_col = jnp.where((c_i <= r_i) if d == 0 else (c_i >= r_i), 1.0, 0.0).astype(BF16)
            gt_all = jnp.concatenate([gt_ref[i] + bcol_ref[...] for i in range(nb)], axis=0)
            lf_r = _log_sigmoid(gt_all)
            br_all = sum(_dot(p, m_row) for p in _split3(lf_r))
            tot_all = jnp.sum(lf_r, axis=1, keepdims=True)
            gc_all = jnp.concatenate([g_ref[i] + brow_ref[...] for i in range(nb)], axis=1)
            bc_all = sum(_dot(m_col, p) for p in _split3(_log_sigmoid(gc_all)))
        gt = gt_all[bi * ng:(bi + 1) * ng, :]
        br = br_all[bi * ng:(bi + 1) * ng, :]
        tot = tot_all[bi * ng:(bi + 1) * ng, :]
        gc = gc_all[:, bi * LANES:(bi + 1) * LANES]
        bc = bc_all[:, bi * LANES:(bi + 1) * LANES]
        k16 = (k_ref[bi].astype(F32) * (ML_QK ** -0.5)).astype(BF16)
        q16 = q_ref[bi]
        q_t = q16.astype(F32).T.astype(BF16)
        v_t = v_ref[bi].astype(F32).T.astype(BF16)
        for hh in range(ML_HEADS):
            c = (bi * 2 + d) * ML_HEADS + hh
            ji = (2 * d) * ML_HEADS + hh
            jf = (2 * d + 1) * ML_HEADS + hh
            i_row = gt[ji:ji + 1, :]
            b_row = br[jf:jf + 1, :]
            b_last = tot[jf:jf + 1, :]
            src_col = gc[:, ji:ji + 1] - bc[:, jf:jf + 1]
            m_old = m_ref[c:c + 1, 0:1]
            st_old = st_ref[c]
            qh = q16[:, hh * ML_QK:(hh + 1) * ML_QK]
            kh = k16[:, hh * ML_QK:(hh + 1) * ML_QK]
            qh_t = q_t[hh * ML_QK:(hh + 1) * ML_QK, :]
            va_t = jnp.concatenate([v_t[hh * ML_V:(hh + 1) * ML_V, :], ones_t], axis=0)
            d_log = jnp.where(causal_t, src_col + b_row, -jnp.inf)
            inter = b_row + m_old
            m_t = jnp.maximum(inter, jnp.max(d_log, axis=0, keepdims=True))
            a = jnp.exp(inter - m_t)
            smat_t = (_dot_nt(kh, qh) * jnp.exp(d_log - m_t)).astype(BF16)
            out_t = a * _dot(st_old.astype(BF16), qh_t) + _dot(va_t, smat_t)
            den = jnp.maximum(jnp.abs(out_t[ML_V:, :]), jnp.exp(-m_t))
            h_ref[bi, :, hh * ML_V:(hh + 1) * ML_V] = (out_t[:ML_V, :] / den).T
            w_log = b_last - b_row + i_row
            m_new = jnp.maximum(b_last + m_old, jnp.max(w_log, axis=1, keepdims=True))
            decay = jnp.exp(b_last + m_old - m_new)
            vw_t = (va_t.astype(F32) * jnp.exp(w_log - m_new)).astype(BF16)
            st_ref[c] = decay * st_old + _dot(vw_t, kh)
            m_ref[c:c + 1, :] = jnp.broadcast_to(m_new, (1, LANES))


def _mlstm(mq, mk, mv, mg, mgt, b_ml, n_lat):
    B, T, _ = mq.shape
    L = ML_CHUNK
    nl = n_lat // L
    nc = CTX_LEN // L
    steps = nl + nc

    def cf(s):
        return jnp.where(s < nc, nl + s, s - nc)

    def cb(s):
        return steps - 1 - s

    nb = PAIR

    def specs(cm):
        return [pl.BlockSpec((nb, L, ML_HEADS * ML_QK), lambda b, s: (b, cm(s), 0)),
                pl.BlockSpec((nb, L, ML_HEADS * ML_QK), lambda b, s: (b, cm(s), 0)),
                pl.BlockSpec((nb, L, ML_HEADS * ML_V), lambda b, s: (b, cm(s), 0)),
                pl.BlockSpec((nb, L, LANES), lambda b, s: (b, cm(s), 0)),
                pl.BlockSpec((nb, 4 * ML_HEADS, L), lambda b, s: (b, 0, cm(s)))]

    b_row = jnp.concatenate([b_ml.astype(F32), jnp.zeros((LANES - 4 * ML_HEADS,), F32)])[None, :]
    b_col = b_ml.astype(F32)[:, None]
    sd = jax.ShapeDtypeStruct
    n_chain = nb * 2 * ML_HEADS
    return pl.pallas_call(
        _mlstm_kernel,
        grid=(B // nb, steps),
        in_specs=specs(cf) + specs(cb) + [pl.BlockSpec((4 * ML_HEADS, 1), lambda b, s: (0, 0)),
                                          pl.BlockSpec((1, LANES), lambda b, s: (0, 0))],
        out_specs=[pl.BlockSpec((nb, L, ML_HEADS * ML_V), lambda b, s: (b, cf(s), 0)),
                   pl.BlockSpec((nb, L, ML_HEADS * ML_V), lambda b, s: (b, cb(s), 0))],
        out_shape=[sd((B, T, ML_HEADS * ML_V), F32)] * 2,
        scratch_shapes=[pltpu.VMEM((n_chain, 2 * ML_V, ML_QK), F32), pltpu.VMEM((n_chain, LANES), F32)],
        compiler_params=_cparams(("parallel", "arbitrary"), 32),
        name="mlstm",
    )(mq, mk, mv, mg, mgt, mq, mk, mv, mg, mgt, b_col, b_row)


def _merge_kernel(a_ref, bn_ref, hf_ref, hb_ref, mo_ref, bg_ref, x_ref, gt1_ref, sh2_ref, sc2_ref,
                  gml_ref, gpost_ref, gpre2_ref, wa_ref, wb_ref, wm_ref, wo_ref, wrh_ref, wrl_ref,
                  xo_ref, h2_ref, aff_ref):
    nb, tm, d = x_ref.shape
    rows = nb * tm
    flat = lambda ref, lo=None, hi=None: (ref[...] if lo is None else ref[:, :, lo:hi]).reshape(rows, -1)
    split = lambda v: v.reshape(nb, tm, v.shape[-1])
    hm = flat(hf_ref) + flat(hb_ref)
    gml = gml_ref[...]
    parts = []
    for hh in range(ML_HEADS):
        sl = slice(hh * ML_V, (hh + 1) * ML_V)
        parts.append(_rms(hm[:, sl], gml[:, sl]))
    mlo = (jnp.concatenate(parts, axis=1) * jax.nn.sigmoid(flat(mo_ref).astype(F32))).astype(BF16)
    ya = _dot(flat(a_ref), wa_ref[...])
    yb = _dot(flat(bn_ref), wb_ref[...])
    ym = _dot(mlo, wm_ref[...])
    y = (jax.nn.sigmoid(flat(bg_ref, 0, d).astype(F32)) * ya
         + jax.nn.sigmoid(flat(bg_ref, d, 2 * d).astype(F32)) * yb
         + jax.nn.sigmoid(flat(bg_ref, 2 * d, 3 * d).astype(F32)) * ym)
    out = _dot(y.astype(BF16), wo_ref[...])
    xn = x_ref[...] + gt1_ref[...] * split(_rms(out, gpost_ref[...]))
    xo_ref[...] = xn
    h2 = split(_rms(xn.reshape(rows, d), gpre2_ref[...])) * (1.0 + sc2_ref[...]) + sh2_ref[...]
    h2_ref[...] = h2.astype(BF16)
    h2 = h2.reshape(rows, d)
    hi = h2.astype(BF16)
    lo = (h2 - hi.astype(F32)).astype(BF16)
    logits = _dot(hi, wrh_ref[...]) + _dot(lo, wrh_ref[...]) + _dot(hi, wrl_ref[...])
    lane = lax.broadcasted_iota(jnp.int32, logits.shape, 1)
    logits = jnp.where(lane < N_EXPERTS, logits, -jnp.inf)
    e = jnp.exp(logits - jnp.max(logits, axis=1, keepdims=True))
    aff_ref[...] = split(e / jnp.sum(e, axis=1, keepdims=True))


def _merge(a, bn, hf, hb, mo, bg, X, mod3, g_ml, g_post1, g_pre2, wa, wb, wm, wo, wrh, wrl, n_lat, with_ctx):
    B, T, d = X.shape
    tm = ROW_TILE
    ctx_tile = n_lat // tm
    nt = ctx_tile + (1 if with_ctx else 0)

    nb = PAIR

    def mrow(b, i):
        return jnp.where(i >= ctx_tile, B // nb, b)

    full = lambda shape: pl.BlockSpec(shape, lambda b, i: (0,) * len(shape))
    row = lambda w: pl.BlockSpec((nb, tm, w), lambda b, i: (b, i, 0))
    modc = lambda ch: pl.BlockSpec((nb, 1, d), lambda b, i: (mrow(b, i), 0, ch))
    sd = jax.ShapeDtypeStruct
    return pl.pallas_call(
        _merge_kernel,
        grid=(B // nb, nt),
        in_specs=[row(256), row(256), row(512), row(512), row(512), row(3 * d), row(d),
                  modc(2), modc(3), modc(4),
                  full((1, ML_HEADS * ML_V)), full((1, d)), full((1, d)),
                  full((256, d)), full((256, d)), full((512, d)), full((d, d)), full((d, LANES)), full((d, LANES))],
        out_specs=[row(d), row(d), row(LANES)],
        out_shape=[sd((B, T, d), F32), sd((B, T, d), BF16), sd((B, T, LANES), F32)],
        compiler_params=_cparams(("parallel", "parallel"), 48),
        name="merge",
    )(a, bn, hf, hb, mo, bg, X, mod3, mod3, mod3, g_ml, g_post1, g_pre2, wa, wb, wm, wo, wrh, wrl)


def _select_kernel(aff_ref, slot_ref, start_ref, *, cap):
    aff = aff_ref[0]
    n = aff.shape[1]
    bits = pltpu.bitcast(aff, jnp.int32)
    capf = jnp.float32(cap)

    def count(mask):
        return jnp.sum(jnp.where(mask, 1.0, 0.0), axis=1, keepdims=True)

    def refine(it, thr):
        cand = thr | jnp.left_shift(jnp.int32(1), 30 - it)
        return jnp.where(count(bits >= cand) >= capf, cand, thr)

    thr = lax.fori_loop(0, 31, refine, jnp.zeros((N_EXPERTS, 1), jnp.int32))
    gt = bits > thr
    eq = bits == thr
    need = capf - count(gt)
    r_i = lax.broadcasted_iota(jnp.int32, (LANES, LANES), 0)
    c_i = lax.broadcasted_iota(jnp.int32, (LANES, LANES), 1)
    before = jnp.where(r_i < c_i, 1.0, 0.0).astype(BF16)

    def excl_prefix(mask):
        carry = jnp.zeros((N_EXPERTS, 1), F32)
        outs, carries = [], []
        for c in range(n // LANES):
            carries.append(carry)
            mc = jnp.where(mask[:, c * LANES:(c + 1) * LANES], 1.0, 0.0)
            outs.append(_dot(mc.astype(BF16), before) + carry)
            carry = carry + jnp.sum(mc, axis=1, keepdims=True)
        return jnp.concatenate(outs, axis=1), carries + [carry]

    sel = gt | (eq & (excl_prefix(eq)[0] < need))
    rank, carries = excl_prefix(sel)
    slot_ref[0] = jnp.where(sel, rank, -1.0)
    lane = lax.broadcasted_iota(jnp.int32, (N_EXPERTS, LANES), 1)
    per_tile = TOK_TILE // LANES
    starts = jnp.zeros((N_EXPERTS, LANES), F32)
    for k in range(n // TOK_TILE + 1):
        starts = jnp.where(lane == k, carries[k * per_tile], starts)
    start_ref[0] = starts


def _select(aff_t, cap):
    B, E, n = aff_t.shape
    slot, starts = pl.pallas_call(
        functools.partial(_select_kernel, cap=cap),
        grid=(B,),
        in_specs=[pl.BlockSpec((1, E, n), lambda b: (b, 0, 0))],
        out_specs=[pl.BlockSpec((1, E, n), lambda b: (b, 0, 0)), pl.BlockSpec((1, E, LANES), lambda b: (b, 0, 0))],
        out_shape=[jax.ShapeDtypeStruct((B, E, n), F32), jax.ShapeDtypeStruct((B, E, LANES), F32)],
        compiler_params=_cparams(("parallel",), 32),
        name="route_select",
    )(aff_t)
    return slot, starts[:, :, :n // TOK_TILE + 1].astype(jnp.int32)


def _log2(v):
    assert v & (v - 1) == 0
    return v.bit_length() - 1


def _window_plan(start_ref, b, kt, cap, win):
    firsts, rounds = [], jnp.int32(0)
    for e in range(N_EXPERTS):
        lo = start_ref[b, e, kt]
        hi = start_ref[b, e, kt + 1]
        first = jnp.minimum(lax.shift_right_logical(lo, _log2(BF16_ROWS)) * BF16_ROWS, cap - win)
        firsts.append(first)
        rounds = jnp.maximum(rounds, lax.shift_right_logical(hi - first + (win - 1), _log2(win)))
    return firsts, rounds


def _dispatch_kernel(start_ref, slot_ref, h_ref, xs_ref, *, cap, win):
    b, kt = pl.program_id(0), pl.program_id(1)

    @pl.when(kt == 0)
    def _():
        xs_ref[...] = jnp.zeros_like(xs_ref)

    slot = slot_ref[0]
    tile = slot.shape[1]
    h = h_ref[0]
    firsts, rounds = _window_plan(start_ref, b, kt, cap, win)
    sub = lax.broadcasted_iota(jnp.int32, (win, tile), 0)

    def one_round(r, carry):
        rows, pieces = [], []
        for e in range(N_EXPERTS):
            base = firsts[e] + r * win
            row0 = pl.multiple_of(jnp.minimum(base, cap - win), BF16_ROWS)
            abs_slot = row0 + sub
            hit = (abs_slot.astype(F32) == slot[e:e + 1, :]) & (abs_slot >= base)
            pieces.append(jnp.where(hit, 1.0, 0.0).astype(BF16))
            rows.append(row0)
        res = _dot(jnp.concatenate(pieces, axis=0), h)
        for e in range(N_EXPERTS):
            cur = xs_ref[0, e, pl.ds(rows[e], win), :]
            xs_ref[0, e, pl.ds(rows[e], win), :] = cur + res[e * win:(e + 1) * win].astype(BF16)
        return carry

    lax.fori_loop(0, rounds, one_round, 0)


def _dispatch(starts, slot, h2, cap, n, tok_off):
    B, E, _ = slot.shape
    d = h2.shape[2]
    blk0 = tok_off // TOK_TILE
    grid_spec = pltpu.PrefetchScalarGridSpec(
        num_scalar_prefetch=1,
        grid=(B, n // TOK_TILE),
        in_specs=[pl.BlockSpec((1, E, TOK_TILE), lambda b, k, st: (b, 0, k)),
                  pl.BlockSpec((1, TOK_TILE, d), lambda b, k, st: (b, blk0 + k, 0))],
        out_specs=pl.BlockSpec((1, E, cap, d), lambda b, k, st: (b, 0, 0, 0)))
    return pl.pallas_call(
        functools.partial(_dispatch_kernel, cap=cap, win=min(SLOT_WIN, cap)),
        grid_spec=grid_spec,
        out_shape=jax.ShapeDtypeStruct((B, E, cap, d), BF16),
        compiler_params=_cparams(("parallel", "arbitrary"), 56),
        name="moe_dispatch",
    )(starts, slot, h2)


def _ffn_kernel(*refs, with_ctx):
    if with_ctx:
        xs_ref, xc_ref, w1_ref, w3_ref, w2_ref, y_ref, yc_ref, w1b, w3b, w2b = refs
    else:
        xs_ref, w1_ref, w3_ref, w2_ref, y_ref, w1b, w3b, w2b = refs

    @pl.when(pl.program_id(1) == 0)
    def _():
        step = 256
        for src, dst in ((w1_ref, w1b), (w3_ref, w3b), (w2_ref, w2b)):
            for r in range(0, src.shape[2], step):
                dst[r:r + step, :] = src[0, 0, r:r + step, :].astype(BF16)

    x = xs_ref[0, 0]
    cap = x.shape[0]
    if with_ctx:
        x = jnp.concatenate([x, xc_ref[0, 0]], axis=0)
    h1 = _dot(x, w1b[...])
    h3 = _dot(x, w3b[...])
    hid = (h1 * jax.nn.sigmoid(h1) * h3).astype(BF16)
    y = _dot(hid, w2b[...]).astype(BF16)
    y_ref[0, 0] = y[:cap]
    if with_ctx:
        yc_ref[0, 0] = y[cap:]


def _expert_ffn(xs, xs_ctx, w1, w3, w2, layer):
    B, E, cap, d = xs.shape
    f = w1.shape[3]
    slots = lambda c: pl.BlockSpec((1, 1, c, d), lambda e, b: (b, e, 0, 0))
    x_in, x_specs = [xs], [slots(cap)]
    out_specs, out_shape = [slots(cap)], [jax.ShapeDtypeStruct((B, E, cap, d), BF16)]
    if xs_ctx is not None:
        cap_c = xs_ctx.shape[2]
        x_in.append(xs_ctx)
        x_specs.append(slots(cap_c))
        out_specs.append(slots(cap_c))
        out_shape.append(jax.ShapeDtypeStruct((B, E, cap_c, d), BF16))
    outs = pl.pallas_call(
        functools.partial(_ffn_kernel, with_ctx=xs_ctx is not None),
        grid=(E, B),
        in_specs=x_specs + [pl.BlockSpec((1, 1, d, f), lambda e, b: (layer, e, 0, 0)),
                            pl.BlockSpec((1, 1, d, f), lambda e, b: (layer, e, 0, 0)),
                            pl.BlockSpec((1, 1, f, d), lambda e, b: (layer, e, 0, 0))],
        out_specs=out_specs,
        out_shape=out_shape,
        scratch_shapes=[pltpu.VMEM((d, f), BF16), pltpu.VMEM((d, f), BF16), pltpu.VMEM((f, d), BF16)],
        compiler_params=_cparams(("arbitrary", "arbitrary"), 56),
        name="moe_expert_ffn",
    )(*x_in, w1, w3, w2)
    return outs if xs_ctx is not None else (outs[0], None)


def _combine_kernel(start_ref, slotc_ref, aff_ref, y_ref, x_ref, gt2_ref, gpost_ref, xo_ref, acc_ref, *, cap, win):
    b, kt = pl.program_id(0), pl.program_id(1)
    slotc = slotc_ref[0]
    aff = aff_ref[0]
    tile = slotc.shape[0]
    firsts, rounds = _window_plan(start_ref, b, kt, cap, win)
    group = LANES // win
    lane = lax.broadcasted_iota(jnp.int32, (tile, LANES), 1)
    seg = lax.shift_right_logical(lane, _log2(win))
    off = lane - seg * win
    acc_ref[...] = jnp.zeros_like(acc_ref)

    def one_round(r, carry):
        hi_p, lo_p, rhs_p = [], [], []
        for j in range(N_EXPERTS // group):
            row0v = jnp.zeros((tile, LANES), jnp.int32)
            basev = jnp.zeros((tile, LANES), jnp.int32)
            scol = jnp.zeros((tile, LANES), F32)
            gate = jnp.zeros((tile, LANES), F32)
            for i in range(group):
                e = j * group + i
                base = firsts[e] + r * win
                row0 = pl.multiple_of(jnp.minimum(base, cap - win), BF16_ROWS)
                mine = seg == i
                row0v = jnp.where(mine, row0, row0v)
                basev = jnp.where(mine, base, basev)
                scol = jnp.where(mine, slotc[:, e:e + 1], scol)
                gate = jnp.where(mine, aff[:, e:e + 1], gate)
                rhs_p.append(y_ref[0, e, pl.ds(row0, win), :])
            abs_slot = row0v + off
            hit = (abs_slot.astype(F32) == scol) & (abs_slot >= basev)
            g = jnp.where(hit, gate, 0.0)
            g_hi = g.astype(BF16)
            hi_p.append(g_hi)
            lo_p.append((g - g_hi.astype(F32)).astype(BF16))
        lhs = jnp.concatenate([jnp.concatenate(hi_p, axis=1), jnp.concatenate(lo_p, axis=1)], axis=0)
        res = _dot(lhs, jnp.concatenate(rhs_p, axis=0))
        acc_ref[...] += res[:tile] + res[tile:]
        return carry

    lax.fori_loop(0, rounds, one_round, 0)
    xo_ref[0] = x_ref[0] + gt2_ref[0] * _rms(acc_ref[...], gpost_ref[...])


def _combine(starts, slotc, aff, y, X, mod3, g_post2, n, tok_off, is_ctx, in_place):
    B, T, d = X.shape
    E, cap = y.shape[1], y.shape[2]
    tt = TOK_TILE
    blk0 = tok_off // tt
    mrow = (lambda b: B) if is_ctx else (lambda b: b)
    if in_place:
        out_spec = pl.BlockSpec((1, tt, d), lambda b, t, st: (b, blk0 + t, 0))
        out_shape, aliases = jax.ShapeDtypeStruct((B, T, d), F32), {4: 0}
    else:
        out_spec = pl.BlockSpec((1, tt, d), lambda b, t, st: (b, t, 0))
        out_shape, aliases = jax.ShapeDtypeStruct((B, n, d), F32), {}
    grid_spec = pltpu.PrefetchScalarGridSpec(
        num_scalar_prefetch=1,
        grid=(B, n // tt),
        in_specs=[pl.BlockSpec((1, tt, E), lambda b, t, st: (b, t, 0)),
                  pl.BlockSpec((1, tt, LANES), lambda b, t, st: (b, blk0 + t, 0)),
                  pl.BlockSpec((1, E, cap, d), lambda b, t, st: (b, 0, 0, 0)),
                  pl.BlockSpec((1, tt, d), lambda b, t, st: (b, blk0 + t, 0)),
                  pl.BlockSpec((1, 1, d), lambda b, t, st: (mrow(b), 0, 5)),
                  pl.BlockSpec((1, d), lambda b, t, st: (0, 0))],
        out_specs=out_spec,
        scratch_shapes=[pltpu.VMEM((tt, d), F32)])
    return pl.pallas_call(
        functools.partial(_combine_kernel, cap=cap, win=min(SLOT_WIN, cap)),
        grid_spec=grid_spec,
        out_shape=out_shape,
        input_output_aliases=aliases,
        compiler_params=_cparams(("parallel", "arbitrary"), 56),
        name="moe_combine",
    )(starts, slotc, aff, y, X, mod3, g_post2)


def _moe(X, h2, aff, mod3, g_post2, w1, w3, w2, layer, n_lat, with_ctx, in_place):
    def route(n, off):
        cap = EC_FACTOR * n // N_EXPERTS
        aff_t = jnp.transpose(aff[:, off:off + n, :N_EXPERTS], (0, 2, 1))
        slot, starts = _select(aff_t, cap)
        return jnp.transpose(slot, (0, 2, 1)), starts, _dispatch(starts, slot, h2, cap, n, off)

    slotc, starts, xs = route(n_lat, 0)
    slotc_c, starts_c, xs_c = route(CTX_LEN, n_lat) if with_ctx else (None, None, None)
    y, y_c = _expert_ffn(xs, xs_c, w1, w3, w2, layer)
    X = _combine(starts, slotc, aff, y, X, mod3, g_post2, n_lat, 0, False, in_place)
    if with_ctx:
        X = _combine(starts_c, slotc_c, aff, y_c, X, mod3, g_post2, CTX_LEN, n_lat, True, True)
    return X


def _rope_partner(w):
    lead = w.shape[:-1]
    ws = w.reshape(lead + (2, 2, MLA_ROPE // 4))
    return jnp.stack([-ws[..., 1, :], ws[..., 0, :]], axis=-2).reshape(lead + (MLA_ROPE,))


def _pad_cols(w, left, total):
    return jnp.pad(w, [(0, 0)] * (w.ndim - 1) + [(left, total - left - w.shape[-1])])


def _prep_in_weights(w_in):
    sizes = (MLA_Q_RANK, MLA_KV_RANK, MLA_ROPE, 256, 256, 256, 256, 256, 512, 512, 4 * ML_HEADS, 3 * D_MODEL)
    offs = np.cumsum(sizes)[:-1].tolist()
    cq, ckv, kr, nq, nk, nv, mq, mk, mv, mo, mg, bg = jnp.split(w_in, offs, axis=-1)
    kr1 = _pad_cols(kr, MLA_NOPE, HEAD_PAD)
    kr2 = _pad_cols(_rope_partner(kr), MLA_NOPE, HEAD_PAD)
    mgp = _pad_cols(mg, 0, LANES)
    return jnp.concatenate([cq, ckv, kr1, kr2, nq, nk, nv, mq, mk, mv, mo, mgp, bg], axis=-1).astype(BF16)


def _prep_mla_weights(w_uq, w_ukv):
    nope, rope = w_uq[..., :MLA_NOPE], w_uq[..., MLA_NOPE:]
    wqa = _pad_cols(jnp.concatenate([nope, rope], axis=-1), 0, HEAD_PAD)
    wqb = _pad_cols(_rope_partner(rope), MLA_NOPE, HEAD_PAD)
    wk = _pad_cols(w_ukv[..., :MLA_NOPE], 0, HEAD_PAD)
    wv = _pad_cols(w_ukv[..., MLA_NOPE:], 0, HEAD_PAD)
    flat = lambda w: w.reshape(w.shape[0], MLA_HEADS * HEAD_PAD).astype(BF16)
    return flat(wqa), flat(wqb), flat(wk), flat(wv)


def _rope_tables(n_lat):
    t = np.arange(n_lat)
    pos = np.stack([t // GRID_W, t % GRID_W], axis=-1).astype(np.float64)
    nf = MLA_ROPE // 4
    inv = np.float32(ROPE_BASE) ** (-np.arange(nf, dtype=np.float32) / nf)
    ang = (pos[..., None].astype(np.float32) * inv).astype(np.float64)
    cos = np.repeat(np.cos(ang)[:, :, None, :], 2, axis=2).reshape(n_lat, MLA_ROPE)
    sin = np.repeat(np.sin(ang)[:, :, None, :], 2, axis=2).reshape(n_lat, MLA_ROPE)
    cos = np.concatenate([cos, np.ones((CTX_LEN, MLA_ROPE))], axis=0)
    sin = np.concatenate([sin, np.zeros((CTX_LEN, MLA_ROPE))], axis=0)
    T = n_lat + CTX_LEN
    z = lambda w: np.zeros((T, w))
    scale = (MLA_NOPE + MLA_ROPE) ** -0.5 * np.log2(np.e)
    ta = scale * np.concatenate([np.ones((T, MLA_NOPE)), cos, z(HEAD_PAD - MLA_NOPE - MLA_ROPE)], axis=1)
    tb = scale * np.concatenate([z(MLA_NOPE), sin, z(HEAD_PAD - MLA_NOPE - MLA_ROPE)], axis=1)
    tc = np.concatenate([z(MLA_NOPE), cos, z(HEAD_PAD - MLA_NOPE - MLA_ROPE)], axis=1)
    ts = np.concatenate([z(MLA_NOPE), sin, z(HEAD_PAD - MLA_NOPE - MLA_ROPE)], axis=1)
    return tuple(jnp.asarray(a, F32) for a in (ta, tb, tc, ts))


def kernel(x, c, ctx, c_ctx, w_mod, b_mod, g_pre1, g_post1, g_pre2, g_post2, w_in, g_q, g_kv, w_uq, w_ukv, rpb,
           b_ml_gates, g_ml, w_br_mla, w_br_na, w_br_ml, w_out, w_router, w_e1, w_e3, w_e2):
    B, n_lat, d = x.shape
    depth = w_mod.shape[0]
    rows = n_lat // GRID_W
    assert d == D_MODEL and ctx.shape[1] == CTX_LEN and n_lat % 512 == 0 and rows >= NA_KROWS + NA_QROWS and B % PAIR == 0

    X = jnp.concatenate([x, ctx], axis=1)
    mod_rows = -(-(B + PAIR) // 8) * 8
    c_all = jnp.concatenate([c, jnp.broadcast_to(c_ctx[None, :], (PAIR, d)),
                             jnp.zeros((mod_rows - B - PAIR, d), F32)], axis=0)
    mod = _modulation(c_all, w_mod, b_mod)
    tabs = _rope_tables(n_lat)
    row2 = lambda v: v.astype(F32)[None, :]

    for l in range(depth):
        ctx_out = l < depth - 1
        mod3 = mod[l][:, None, :]
        w_pad = _prep_in_weights(w_in[l])
        wqa, wqb, wk, wv = _prep_mla_weights(w_uq[l], w_ukv[l])
        (Q, K, V, nq, nk, nv, mq, mk, mv, mo, mg, bg) = _inproj(
            X, mod3, row2(g_pre1[l]), w_pad, row2(g_q[l]), row2(g_kv[l]), wqa, wqb, wk, wv, tabs, n_lat)
        a = jnp.concatenate([_mla_attention(Q, K, V, n_lat, False), _mla_attention(Q, K, V, n_lat, True)], axis=1)
        bn = _neighbourhood_attention(nq, nk, nv, _na_bias_table(rpb[l], rows), n_lat, True)
        mgt = jnp.transpose(mg[:, :, :4 * ML_HEADS], (0, 2, 1))
        hf, hb = _mlstm(mq, mk, mv, mg, mgt, b_ml_gates[l], n_lat)
        wr = _pad_cols(w_router[l].astype(F32), 0, LANES)
        wrh = wr.astype(BF16)
        wrl = (wr - wrh.astype(F32)).astype(BF16)
        X, h2, aff = _merge(a, bn, hf, hb, mo, bg, X, mod3, row2(g_ml[l]), row2(g_post1[l]), row2(g_pre2[l]),
                            w_br_mla[l].astype(BF16), w_br_na[l].astype(BF16), w_br_ml[l].astype(BF16),
                            w_out[l].astype(BF16), wrh, wrl, n_lat, True)
        X = _moe(X, h2, aff, mod3, row2(g_post2[l]), w_e1, w_e3, w_e2, l, n_lat, ctx_out, in_place=ctx_out)
    return X
```

```python
import functools

import numpy as np
import jax
import jax.numpy as jnp
from jax import lax
from jax.experimental import pallas as pl
from jax.experimental.pallas import tpu as pltpu

F32 = jnp.float32
BF16 = jnp.bfloat16

D_MODEL = 1024
CTX_LEN = 256
GRID_W = 64
EPS = 1e-6

MLA_HEADS = 4
MLA_Q_RANK = 256
MLA_KV_RANK = 128
MLA_NOPE = 64
MLA_ROPE = 32
MLA_V = 64
ROPE_BASE = 10000.0

NA_HEADS = 4
NA_DIM = 64
NA_KR = 8
NA_KC = 16
NA_QROWS = 4
NA_KROWS = 12

ML_HEADS = 4
ML_QK = 64
ML_V = 128
ML_CHUNK = 128

N_EXPERTS = 16
D_EXPERT = 1024
EC_FACTOR = 2

LANES = 128
ROW_TILE = 256
PAIR = 2
TOK_TILE = 256
SLOT_WIN = 64
BF16_ROWS = 16
HEAD_PAD = 128
NEG_BIG = -1e30

_IN_GROUPS = (("cq", 256), ("ckv", 128), ("kr1", 128), ("kr2", 128), ("nq", 256), ("nk", 256), ("nv", 256),
              ("mq", 256), ("mk", 256), ("mv", 512), ("mo", 512), ("mg", 128), ("bg", 3 * D_MODEL))
_OFF = {}
_o = 0
for _n, _w in _IN_GROUPS:
    _OFF[_n] = (_o, _o + _w)
    _o += _w
D_IN_PAD = _o


def _cparams(sem, vmem_mb):
    return pltpu.CompilerParams(dimension_semantics=sem, vmem_limit_bytes=vmem_mb * 1024 * 1024)


def _dot(a, b):
    return jnp.dot(a, b, preferred_element_type=F32)


def _dot_nt(a, b):
    return lax.dot_general(a, b, (((1,), (1,)), ((), ())), preferred_element_type=F32)


def _rms(x, g):
    return x * lax.rsqrt(jnp.mean(x * x, axis=-1, keepdims=True) + EPS) * g


def _split3(x):
    hi = x.astype(BF16)
    r1 = x - hi.astype(F32)
    mid = r1.astype(BF16)
    lo = (r1 - mid.astype(F32)).astype(BF16)
    return hi, mid, lo


def _log_sigmoid(x):
    return jnp.minimum(x, 0.0) - jnp.log(1.0 + jnp.exp(-jnp.abs(x)))


def _mod_kernel(c_ref, w_ref, b_ref, o_ref):
    cs = c_ref[...]
    cs = cs * jax.nn.sigmoid(cs)
    o_ref[0] = _dot(cs.astype(BF16), w_ref[0].astype(BF16)) + b_ref[0]


def _modulation(c_all, w_mod, b_mod):
    depth, d, n6 = w_mod.shape
    rows = c_all.shape[0]
    tn = 512
    return pl.pallas_call(
        _mod_kernel,
        grid=(depth, n6 // tn),
        in_specs=[pl.BlockSpec((rows, d), lambda l, j: (0, 0)),
                  pl.BlockSpec((1, d, tn), lambda l, j: (l, 0, j)),
                  pl.BlockSpec((1, 1, tn), lambda l, j: (l, 0, j))],
        out_specs=pl.BlockSpec((1, rows, tn), lambda l, j: (l, 0, j)),
        out_shape=jax.ShapeDtypeStruct((depth, rows, n6), F32),
        compiler_params=_cparams(("parallel", "parallel"), 32),
        name="modulation",
    )(c_all, w_mod, b_mod.reshape(depth, 1, n6))


def _inproj_kernel(x_ref, sh_ref, sc_ref, g1_ref, w_ref, gq_ref, gkv_ref, wqa_ref, wqb_ref, wk_ref, wv_ref,
                   ta_ref, tb_ref, tc_ref, ts_ref,
                   q_ref, k_ref, v_ref, nq_ref, nk_ref, nv_ref, mq_ref, mk_ref, mv_ref, mo_ref, mg_ref, bg_ref):
    nb, tm, d = x_ref.shape
    rows = nb * tm
    split = lambda a: a.reshape(nb, tm, a.shape[-1])
    h = split(_rms(x_ref[...].reshape(rows, d), g1_ref[...])) * (1.0 + sc_ref[...]) + sh_ref[...]
    hb = h.reshape(rows, d).astype(BF16)

    def proj(name, lo=None, hi=None):
        a, b = _OFF[name]
        if lo is not None:
            a, b = a + lo, a + hi
        return _dot(hb, w_ref[:, a:b])

    def group(first, last):
        names = [n for n, _ in _IN_GROUPS]
        names = names[names.index(first):names.index(last) + 1]
        lo = _OFF[first][0]
        z = _dot(hb, w_ref[:, lo:_OFF[last][1]])
        return {n: z[:, _OFF[n][0] - lo:_OFF[n][1] - lo] for n in names}

    g0 = group("cq", "kr2")
    cq, ckv = g0["cq"], g0["ckv"]
    krr = split(g0["kr1"]) * tc_ref[...][None] + split(g0["kr2"]) * ts_ref[...][None]
    g1 = group("nq", "mk")
    nq_ref[...] = split(g1["nq"]).astype(BF16)
    nk_ref[...] = split(g1["nk"]).astype(BF16)
    nv_ref[...] = split(g1["nv"]).astype(BF16)
    mq_ref[...] = split(g1["mq"]).astype(BF16)
    mk_ref[...] = split(g1["mk"]).astype(BF16)
    cqn = _rms(cq, gq_ref[...]).astype(BF16)
    qa = _dot(cqn, wqa_ref[...])
    qb = _dot(cqn, wqb_ref[...])
    ckvn = _rms(ckv, gkv_ref[...]).astype(BF16)
    kn = _dot(ckvn, wk_ref[...])
    vv = _dot(ckvn, wv_ref[...])
    g2 = group("mv", "mg")
    mv_ref[...] = split(g2["mv"]).astype(BF16)
    mo_ref[...] = split(g2["mo"]).astype(BF16)
    mg_ref[...] = split(g2["mg"])
    bg_ref[...] = split(proj("bg")).astype(BF16)
    ta = ta_ref[...][None]
    tb = tb_ref[...][None]
    one_col = (lax.broadcasted_iota(jnp.int32, (1, 1, HEAD_PAD), 2) == MLA_V).astype(F32)
    for hh in range(MLA_HEADS):
        sl = slice(hh * HEAD_PAD, (hh + 1) * HEAD_PAD)
        q_ref[:, hh] = (split(qa[:, sl]) * ta + split(qb[:, sl]) * tb).astype(BF16)
        k_ref[:, hh] = (split(kn[:, sl]) + krr).astype(BF16)
        v_ref[:, hh] = (split(vv[:, sl]) + one_col).astype(BF16)


def _inproj(X, mod3, g_pre1, w_pad, g_q, g_kv, wqa, wqb, wk, wv, tabs, n_lat):
    B, T, d = X.shape
    tm = ROW_TILE
    nb = PAIR
    nt = T // tm
    ctx_tile = n_lat // tm

    def mrow(b, i):
        return jnp.where(i >= ctx_tile, B // nb, b)

    full = lambda shape: pl.BlockSpec(shape, lambda b, i: (0,) * len(shape))
    tab = pl.BlockSpec((tm, HEAD_PAD), lambda b, i: (i, 0))
    row_out = lambda w: pl.BlockSpec((nb, tm, w), lambda b, i: (b, i, 0))
    head_out = pl.BlockSpec((nb, MLA_HEADS, tm, HEAD_PAD), lambda b, i: (b, 0, i, 0))
    sd = jax.ShapeDtypeStruct
    outs = pl.pallas_call(
        _inproj_kernel,
        grid=(B // nb, nt),
        in_specs=[pl.BlockSpec((nb, tm, d), lambda b, i: (b, i, 0)),
                  pl.BlockSpec((nb, 1, d), lambda b, i: (mrow(b, i), 0, 0)),
                  pl.BlockSpec((nb, 1, d), lambda b, i: (mrow(b, i), 0, 1)),
                  full((1, d)), full((d, D_IN_PAD)), full((1, MLA_Q_RANK)), full((1, MLA_KV_RANK)),
                  full((MLA_Q_RANK, MLA_HEADS * HEAD_PAD)), full((MLA_Q_RANK, MLA_HEADS * HEAD_PAD)),
                  full((MLA_KV_RANK, MLA_HEADS * HEAD_PAD)), full((MLA_KV_RANK, MLA_HEADS * HEAD_PAD)),
                  tab, tab, tab, tab],
        out_specs=[head_out, head_out, head_out,
                   row_out(256), row_out(256), row_out(256), row_out(256), row_out(256),
                   row_out(512), row_out(512), row_out(128), row_out(3 * d)],
        out_shape=[sd((B, MLA_HEADS, T, HEAD_PAD), BF16)] * 3
                  + [sd((B, T, 256), BF16)] * 5 + [sd((B, T, 512), BF16)] * 2
                  + [sd((B, T, 128), F32), sd((B, T, 3 * d), BF16)],
        compiler_params=_cparams(("parallel", "parallel"), 56),
        name="inproj",
    )(X, mod3, mod3, g_pre1, w_pad, g_q, g_kv, wqa, wqb, wk, wv, *tabs)
    return outs


def _flash_kernel(q_ref, k_ref, v_ref, o_ref, s0_ref, s1_ref, p0_ref, p1_ref, *, nk, rb):
    tq = q_ref.shape[2]
    s_bufs, p_bufs = (s0_ref, s1_ref), (p0_ref, p1_ref)

    def scores(hh):
        s_bufs[hh % 2][...] = _dot_nt(q_ref[0, hh], k_ref[0, hh, :nk, :])

    scores(0)
    for hh in range(MLA_HEADS):
        if hh + 1 < MLA_HEADS:
            scores(hh + 1)
        s_ref, p_ref = s_bufs[hh % 2], p_bufs[hh % 2]
        for r in range(0, tq, rb):
            s = s_ref[r:r + rb, :]
            m = jnp.max(s, axis=1, keepdims=True)
            p_ref[r:r + rb, :] = jnp.exp2(s - m).astype(BF16)
        acc = _dot(p_ref[...], v_ref[0, hh, :nk, :])
        o = acc[:, :MLA_V] / acc[:, MLA_V:MLA_V + 1]
        o_ref[0, :, hh * MLA_V:(hh + 1) * MLA_V] = o.astype(BF16)


def _mla_attention(Q, K, V, n_lat, context):
    B, H, T, E = Q.shape
    kc = 256
    if not context:
        tq, nk, nq = 512, T, n_lat
        q_spec = pl.BlockSpec((1, H, tq, E), lambda b, i: (b, 0, i, 0))
        kv_spec = pl.BlockSpec((1, H, T, E), lambda b, i: (b, 0, 0, 0))
    else:
        tq, nk, nq = CTX_LEN, CTX_LEN, CTX_LEN
        cblk = n_lat // CTX_LEN
        q_spec = pl.BlockSpec((1, H, tq, E), lambda b, i: (b, 0, cblk, 0))
        kv_spec = pl.BlockSpec((1, H, nk, E), lambda b, i: (b, 0, cblk, 0))
    return pl.pallas_call(
        functools.partial(_flash_kernel, nk=nk, rb=64),
        grid=(B, nq // tq),
        in_specs=[q_spec, kv_spec, kv_spec],
        out_specs=pl.BlockSpec((1, tq, H * MLA_V), lambda b, i: (b, i, 0)),
        out_shape=jax.ShapeDtypeStruct((B, nq, H * MLA_V), BF16),
        scratch_shapes=[pltpu.VMEM((tq, nk), F32), pltpu.VMEM((tq, nk), F32),
                        pltpu.VMEM((tq, nk), BF16), pltpu.VMEM((tq, nk), BF16)],
        compiler_params=_cparams(("parallel", "parallel"), 56),
        name="mla_attention_ctx" if context else "mla_attention",
    )(Q, K, V)


def _na_kernel(q_ref, k_ref, v_ref, bias_ref, o_ref, *, n_lat, rows, with_ctx):
    j = pl.program_id(1)
    nblk = rows // NA_QROWS
    scale = NA_DIM ** -0.5
    assert np.log2(scale) == round(np.log2(scale))
    q = (q_ref[0].astype(F32) * scale).astype(BF16)
    kctx = k_ref[0, n_lat:n_lat + CTX_LEN, :]
    vctx = v_ref[0, n_lat:n_lat + CTX_LEN, :]

    def local_step():
        kb = jnp.clip(j * NA_QROWS - NA_KR // 2, 0, rows - NA_KROWS)
        start = pl.multiple_of(kb * GRID_W, GRID_W)
        kloc = k_ref[0, pl.ds(start, NA_KROWS * GRID_W), :]
        vloc = v_ref[0, pl.ds(start, NA_KROWS * GRID_W), :]
        for hh in range(NA_HEADS):
            sl = slice(hh * NA_DIM, (hh + 1) * NA_DIM)
            qh = q[:, sl]
            s_loc = _dot_nt(qh, kloc[:, sl]) + bias_ref[0, hh]
            s_ctx = _dot_nt(qh, kctx[:, sl])
            m = jnp.maximum(jnp.max(s_loc, axis=1, keepdims=True), jnp.max(s_ctx, axis=1, keepdims=True))
            p_loc = jnp.exp(s_loc - m)
            p_ctx = jnp.exp(s_ctx - m)
            den = jnp.sum(p_loc, axis=1, keepdims=True) + jnp.sum(p_ctx, axis=1, keepdims=True)
            o = _dot(p_loc.astype(BF16), vloc[:, sl]) + _dot(p_ctx.astype(BF16), vctx[:, sl])
            o_ref[0, :, sl] = (o / den).astype(BF16)

    def ctx_step():
        for hh in range(NA_HEADS):
            sl = slice(hh * NA_DIM, (hh + 1) * NA_DIM)
            s = _dot_nt(q[:, sl], kctx[:, sl])
            p = jnp.exp(s - jnp.max(s, axis=1, keepdims=True))
            den = jnp.sum(p, axis=1, keepdims=True)
            o_ref[0, :, sl] = (_dot(p.astype(BF16), vctx[:, sl]) / den).astype(BF16)

    if with_ctx:
        pl.when(j < nblk)(local_step)
        pl.when(j == nblk)(ctx_step)
    else:
        local_step()


def _na_bias_table(rpb, rows):
    W = GRID_W
    nr, ncol = 2 * NA_KR - 1, 2 * NA_KC - 1
    w = np.arange(W)[:, None]
    c = np.arange(W)[None, :]
    cs = np.clip(w - NA_KC // 2, 0, W - NA_KC)
    col_ok = (c >= cs) & (c < cs + NA_KC)
    sel_c = (col_ok[:, :, None] & ((c - w + NA_KC - 1)[:, :, None] == np.arange(ncol))).astype(np.float32)
    sel_r = np.zeros((3, NA_QROWS, NA_KROWS, nr), np.float32)
    for p, r0 in enumerate((0, NA_QROWS, rows - NA_QROWS)):
        kb = int(np.clip(r0 - NA_KR // 2, 0, rows - NA_KROWS))
        for q in range(NA_QROWS):
            rs = int(np.clip(r0 + q - NA_KR // 2, 0, rows - NA_KR))
            for k in range(NA_KROWS):
                if rs <= kb + k < rs + NA_KR:
                    sel_r[p, q, k, kb + k - (r0 + q) + NA_KR - 1] = 1.0
    hi = lax.Precision.HIGHEST
    by_col = jnp.einsum("hrd,wcd->hrwc", rpb.astype(F32), jnp.asarray(sel_c), precision=hi)
    bias = jnp.einsum("pqkr,hrwc->phqwkc", jnp.asarray(sel_r), by_col, precision=hi)
    valid = np.einsum("pqk,wc->pqwkc", sel_r.sum(-1), col_ok.astype(np.float32)) > 0
    bias = jnp.where(jnp.asarray(valid)[:, None], bias, NEG_BIG)
    return bias.reshape(3, NA_HEADS, NA_QROWS * W, NA_KROWS * W)


def _neighbourhood_attention(nq, nk, nv, bias_tab, n_lat, with_ctx):
    B, T, w = nq.shape
    rows = n_lat // GRID_W
    nblk = rows // NA_QROWS
    tq = NA_QROWS * GRID_W
    steps = nblk + (1 if with_ctx else 0)

    def pat(j):
        return jnp.where(j == 0, 0, jnp.where(j >= nblk - 1, 2, 1))

    return pl.pallas_call(
        functools.partial(_na_kernel, n_lat=n_lat, rows=rows, with_ctx=with_ctx),
        grid=(B, steps),
        in_specs=[pl.BlockSpec((1, tq, w), lambda b, j: (b, j, 0)),
                  pl.BlockSpec((1, T, w), lambda b, j: (b, 0, 0)),
                  pl.BlockSpec((1, T, w), lambda b, j: (b, 0, 0)),
                  pl.BlockSpec((1, NA_HEADS, tq, NA_KROWS * GRID_W), lambda b, j: (pat(j), 0, 0, 0))],
        out_specs=pl.BlockSpec((1, tq, w), lambda b, j: (b, j, 0)),
        out_shape=jax.ShapeDtypeStruct((B, T, w), BF16),
        compiler_params=_cparams(("parallel", "arbitrary"), 48),
        name="neighbourhood_attention",
    )(nq, nk, nv, bias_tab)


def _mlstm_kernel(qf_ref, kf_ref, vf_ref, gf_ref, gtf_ref, qb_ref, kb_ref, vb_ref, gb_ref, gtb_ref,
                  bcol_ref, brow_ref, hf_ref, hb_ref, st_ref, m_ref):
    L = ML_CHUNK

    @pl.when(pl.program_id(1) == 0)
    def _():
        st_ref[...] = jnp.zeros_like(st_ref)
        m_ref[...] = jnp.zeros_like(m_ref)

    r_i = lax.broadcasted_iota(jnp.int32, (L, L), 0)
    c_i = lax.broadcasted_iota(jnp.int32, (L, L), 1)
    ones_t = jnp.ones((ML_V, L), BF16)
    dirs = ((qf_ref, kf_ref, vf_ref, gf_ref, gtf_ref, hf_ref), (qb_ref, kb_ref, vb_ref, gb_ref, gtb_ref, hb_ref))
    nb = qf_ref.shape[0]
    ng = 4 * ML_HEADS
    for bi, d in [(bi, d) for d in range(2) for bi in range(nb)]:
        q_ref, k_ref, v_ref, g_ref, gt_ref, h_ref = dirs[d]
        if bi == 0:
            causal_t = (r_i <= c_i) if d == 0 else (r_i >= c_i)
            m_row = jnp.where(causal_t, 1.0, 0.0).astype(BF16)
            m_col = jnp.where((c_i <= r_i) if d == 0 else (c_i >= r_i), 1.0, 0.0).astype(BF16)
            gt_all = jnp.concatenate([gt_ref[i] + bcol_ref[...] for i in range(nb)], axis=0)
            lf_r = _log_sigmoid(gt_all)
            br_all = sum(_dot(p, m_row) for p in _split3(lf_r))
            tot_all = jnp.sum(lf_r, axis=1, keepdims=True)
            gc_all = jnp.concatenate([g_ref[i] + brow_ref[...] for i in range(nb)], axis=1)
            bc_all = sum(_dot(m_col, p) for p in _split3(_log_sigmoid(gc_all)))
        gt = gt_all[bi * ng:(bi + 1) * ng, :]
        br = br_all[bi * ng:(bi + 1) * ng, :]
        tot = tot_all[bi * ng:(bi + 1) * ng, :]
        gc = gc_all[:, bi * LANES:(bi + 1) * LANES]
        bc = bc_all[:, bi * LANES:(bi + 1) * LANES]
        k16 = (k_ref[bi].astype(F32) * (ML_QK ** -0.5)).astype(BF16)
        q16 = q_ref[bi]
        q_t = q16.astype(F32).T.astype(BF16)
        v_t = v_ref[bi].astype(F32).T.astype(BF16)
        for hh in range(ML_HEADS):
            c = (bi * 2 + d) * ML_HEADS + hh
            ji = (2 * d) * ML_HEADS + hh
            jf = (2 * d + 1) * ML_HEADS + hh
            i_row = gt[ji:ji + 1, :]
            b_row = br[jf:jf + 1, :]
            b_last = tot[jf:jf + 1, :]
            src_col = gc[:, ji:ji + 1] - bc[:, jf:jf + 1]
            m_old = m_ref[c:c + 1, 0:1]
            st_old = st_ref[c]
            qh = q16[:, hh * ML_QK:(hh + 1) * ML_QK]
            kh = k16[:, hh * ML_QK:(hh + 1) * ML_QK]
            qh_t = q_t[hh * ML_QK:(hh + 1) * ML_QK, :]
            va_t = jnp.concatenate([v_t[hh * ML_V:(hh + 1) * ML_V, :], ones_t], axis=0)
            d_log = jnp.where(causal_t, src_col + b_row, -jnp.inf)
            inter = b_row + m_old
            m_t = jnp.maximum(inter, jnp.max(d_log, axis=0, keepdims=True))
            a = jnp.exp(inter - m_t)
            smat_t = (_dot_nt(kh, qh) * jnp.exp(d_log - m_t)).astype(BF16)
            out_t = a * _dot(st_old.astype(BF16), qh_t) + _dot(va_t, smat_t)
            den = jnp.maximum(jnp.abs(out_t[ML_V:, :]), jnp.exp(-m_t))
            h_ref[bi, :, hh * ML_V:(hh + 1) * ML_V] = (out_t[:ML_V, :] / den).T
            w_log = b_last - b_row + i_row
            m_new = jnp.maximum(b_last + m_old, jnp.max(w_log, axis=1, keepdims=True))
            decay = jnp.exp(b_last + m_old - m_new)
            vw_t = (va_t.astype(F32) * jnp.exp(w_log - m_new)).astype(BF16)
            st_ref[c] = decay * st_old + _dot(vw_t, kh)
            m_ref[c:c + 1, :] = jnp.broadcast_to(m_new, (1, LANES))


def _mlstm(mq, mk, mv, mg, mgt, b_ml, n_lat):
    B, T, _ = mq.shape
    L = ML_CHUNK
    nl = n_lat // L
    nc = CTX_LEN // L
    steps = nl + nc

    def cf(s):
        return jnp.where(s < nc, nl + s, s - nc)

    def cb(s):
        return steps - 1 - s

    nb = PAIR

    def specs(cm):
        return [pl.BlockSpec((nb, L, ML_HEADS * ML_QK), lambda b, s: (b, cm(s), 0)),
                pl.BlockSpec((nb, L, ML_HEADS * ML_QK), lambda b, s: (b, cm(s), 0)),
                pl.BlockSpec((nb, L, ML_HEADS * ML_V), lambda b, s: (b, cm(s), 0)),
                pl.BlockSpec((nb, L, LANES), lambda b, s: (b, cm(s), 0)),
                pl.BlockSpec((nb, 4 * ML_HEADS, L), lambda b, s: (b, 0, cm(s)))]

    b_row = jnp.concatenate([b_ml.astype(F32), jnp.zeros((LANES - 4 * ML_HEADS,), F32)])[None, :]
    b_col = b_ml.astype(F32)[:, None]
    sd = jax.ShapeDtypeStruct
    n_chain = nb * 2 * ML_HEADS
    return pl.pallas_call(
        _mlstm_kernel,
        grid=(B // nb, steps),
        in_specs=specs(cf) + specs(cb) + [pl.BlockSpec((4 * ML_HEADS, 1), lambda b, s: (0, 0)),
                                          pl.BlockSpec((1, LANES), lambda b, s: (0, 0))],
        out_specs=[pl.BlockSpec((nb, L, ML_HEADS * ML_V), lambda b, s: (b, cf(s), 0)),
                   pl.BlockSpec((nb, L, ML_HEADS * ML_V), lambda b, s: (b, cb(s), 0))],
        out_shape=[sd((B, T, ML_HEADS * ML_V), F32)] * 2,
        scratch_shapes=[pltpu.VMEM((n_chain, 2 * ML_V, ML_QK), F32), pltpu.VMEM((n_chain, LANES), F32)],
        compiler_params=_cparams(("parallel", "arbitrary"), 32),
        name="mlstm",
    )(mq, mk, mv, mg, mgt, mq, mk, mv, mg, mgt, b_col, b_row)


def _merge_kernel(a_ref, bn_ref, hf_ref, hb_ref, mo_ref, bg_ref, x_ref, gt1_ref, sh2_ref, sc2_ref,
                  gml_ref, gpost_ref, gpre2_ref, wa_ref, wb_ref, wm_ref, wo_ref, wrh_ref, wrl_ref,
                  xo_ref, h2_ref, aff_ref):
    nb, tm, d = x_ref.shape
    gml = gml_ref[...]
    for bi in range(nb):
        hm = hf_ref[bi] + hb_ref[bi]
        parts = []
        for hh in range(ML_HEADS):
            sl = slice(hh * ML_V, (hh + 1) * ML_V)
            parts.append(_rms(hm[:, sl], gml[:, sl]))
        mlo = (jnp.concatenate(parts, axis=1) * jax.nn.sigmoid(mo_ref[bi].astype(F32))).astype(BF16)
        ya = _dot(a_ref[bi], wa_ref[...])
        yb = _dot(bn_ref[bi], wb_ref[...])
        ym = _dot(mlo, wm_ref[...])
        y = (jax.nn.sigmoid(bg_ref[bi, :, 0:d].astype(F32)) * ya
             + jax.nn.sigmoid(bg_ref[bi, :, d:2 * d].astype(F32)) * yb
             + jax.nn.sigmoid(bg_ref[bi, :, 2 * d:3 * d].astype(F32)) * ym)
        out = _dot(y.astype(BF16), wo_ref[...])
        xn = x_ref[bi] + gt1_ref[bi] * _rms(out, gpost_ref[...])
        xo_ref[bi] = xn
        h2 = _rms(xn, gpre2_ref[...]) * (1.0 + sc2_ref[bi]) + sh2_ref[bi]
        h2_ref[bi] = h2.astype(BF16)
        hi = h2.astype(BF16)
        lo = (h2 - hi.astype(F32)).astype(BF16)
        logits = _dot(hi, wrh_ref[...]) + _dot(lo, wrh_ref[...]) + _dot(hi, wrl_ref[...])
        lane = lax.broadcasted_iota(jnp.int32, logits.shape, 1)
        logits = jnp.where(lane < N_EXPERTS, logits, -jnp.inf)
        e = jnp.exp(logits - jnp.max(logits, axis=1, keepdims=True))
        aff_ref[bi] = e / jnp.sum(e, axis=1, keepdims=True)


def _merge(a, bn, hf, hb, mo, bg, X, mod3, g_ml, g_post1, g_pre2, wa, wb, wm, wo, wrh, wrl, n_lat, with_ctx):
    B, T, d = X.shape
    tm = ROW_TILE
    ctx_tile = n_lat // tm
    nt = ctx_tile + (1 if with_ctx else 0)

    nb = PAIR

    def mrow(b, i):
        return jnp.where(i >= ctx_tile, B // nb, b)

    full = lambda shape: pl.BlockSpec(shape, lambda b, i: (0,) * len(shape))
    row = lambda w: pl.BlockSpec((nb, tm, w), lambda b, i: (b, i, 0))
    modc = lambda ch: pl.BlockSpec((nb, 1, d), lambda b, i: (mrow(b, i), 0, ch))
    sd = jax.ShapeDtypeStruct
    return pl.pallas_call(
        _merge_kernel,
        grid=(B // nb, nt),
        in_specs=[row(256), row(256), row(512), row(512), row(512), row(3 * d), row(d),
                  modc(2), modc(3), modc(4),
                  full((1, ML_HEADS * ML_V)), full((1, d)), full((1, d)),
                  full((256, d)), full((256, d)), full((512, d)), full((d, d)), full((d, LANES)), full((d, LANES))],
        out_specs=[row(d), row(d), row(LANES)],
        out_shape=[sd((B, T, d), F32), sd((B, T, d), BF16), sd((B, T, LANES), F32)],
        compiler_params=_cparams(("parallel", "parallel"), 48),
        name="merge",
    )(a, bn, hf, hb, mo, bg, X, mod3, mod3, mod3, g_ml, g_post1, g_pre2, wa, wb, wm, wo, wrh, wrl)


def _select_kernel(aff_ref, slot_ref, start_ref, *, cap):
    aff = aff_ref[0]
    n = aff.shape[1]
    bits = pltpu.bitcast(aff, jnp.int32)
    capf = jnp.float32(cap)

    def count(mask):
        return jnp.sum(jnp.where(mask, 1.0, 0.0), axis=1, keepdims=True)

    def refine(it, thr):
        cand = thr | jnp.left_shift(jnp.int32(1), 30 - it)
        return jnp.where(count(bits >= cand) >= capf, cand, thr)

    thr = lax.fori_loop(0, 31, refine, jnp.zeros((N_EXPERTS, 1), jnp.int32))
    gt = bits > thr
    eq = bits == thr
    need = capf - count(gt)
    r_i = lax.broadcasted_iota(jnp.int32, (LANES, LANES), 0)
    c_i = lax.broadcasted_iota(jnp.int32, (LANES, LANES), 1)
    before = jnp.where(r_i < c_i, 1.0, 0.0).astype(BF16)

    def excl_prefix(mask):
        carry = jnp.zeros((N_EXPERTS, 1), F32)
        outs, carries = [], []
        for c in range(n // LANES):
            carries.append(carry)
            mc = jnp.where(mask[:, c * LANES:(c + 1) * LANES], 1.0, 0.0)
            outs.append(_dot(mc.astype(BF16), before) + carry)
            carry = carry + jnp.sum(mc, axis=1, keepdims=True)
        return jnp.concatenate(outs, axis=1), carries + [carry]

    sel = gt | (eq & (excl_prefix(eq)[0] < need))
    rank, carries = excl_prefix(sel)
    slot_ref[0] = jnp.where(sel, rank, -1.0)
    lane = lax.broadcasted_iota(jnp.int32, (N_EXPERTS, LANES), 1)
    per_tile = TOK_TILE // LANES
    starts = jnp.zeros((N_EXPERTS, LANES), F32)
    for k in range(n // TOK_TILE + 1):
        starts = jnp.where(lane == k, carries[k * per_tile], starts)
    start_ref[0] = starts


def _select(aff_t, cap):
    B, E, n = aff_t.shape
    slot, starts = pl.pallas_call(
        functools.partial(_select_kernel, cap=cap),
        grid=(B,),
        in_specs=[pl.BlockSpec((1, E, n), lambda b: (b, 0, 0))],
        out_specs=[pl.BlockSpec((1, E, n), lambda b: (b, 0, 0)), pl.BlockSpec((1, E, LANES), lambda b: (b, 0, 0))],
        out_shape=[jax.ShapeDtypeStruct((B, E, n), F32), jax.ShapeDtypeStruct((B, E, LANES), F32)],
        compiler_params=_cparams(("parallel",), 32),
        name="route_select",
    )(aff_t)
    return slot, starts[:, :, :n // TOK_TILE + 1].astype(jnp.int32)


def _log2(v):
    assert v & (v - 1) == 0
    return v.bit_length() - 1


def _window_plan(start_ref, b, kt, cap, win):
    firsts, rounds = [], jnp.int32(0)
    for e in range(N_EXPERTS):
        lo = start_ref[b, e, kt]
        hi = start_ref[b, e, kt + 1]
        first = jnp.minimum(lax.shift_right_logical(lo, _log2(BF16_ROWS)) * BF16_ROWS, cap - win)
        firsts.append(first)
        rounds = jnp.maximum(rounds, lax.shift_right_logical(hi - first + (win - 1), _log2(win)))
    return firsts, rounds


def _dispatch_kernel(start_ref, slot_ref, h_ref, xs_ref, *, cap, win):
    b, kt = pl.program_id(0), pl.program_id(1)

    @pl.when(kt == 0)
    def _():
        xs_ref[...] = jnp.zeros_like(xs_ref)

    slot = slot_ref[0]
    tile = slot.shape[1]
    h = h_ref[0]
    firsts, rounds = _window_plan(start_ref, b, kt, cap, win)
    sub = lax.broadcasted_iota(jnp.int32, (win, tile), 0)

    def one_round(r, carry):
        rows, pieces = [], []
        for e in range(N_EXPERTS):
            base = firsts[e] + r * win
            row0 = pl.multiple_of(jnp.minimum(base, cap - win), BF16_ROWS)
            abs_slot = row0 + sub
            hit = (abs_slot.astype(F32) == slot[e:e + 1, :]) & (abs_slot >= base)
            pieces.append(jnp.where(hit, 1.0, 0.0).astype(BF16))
            rows.append(row0)
        res = _dot(jnp.concatenate(pieces, axis=0), h)
        for e in range(N_EXPERTS):
            cur = xs_ref[0, e, pl.ds(rows[e], win), :]
            xs_ref[0, e, pl.ds(rows[e], win), :] = cur + res[e * win:(e + 1) * win].astype(BF16)
        return carry

    lax.fori_loop(0, rounds, one_round, 0)


def _dispatch(starts, slot, h2, cap, n, tok_off):
    B, E, _ = slot.shape
    d = h2.shape[2]
    blk0 = tok_off // TOK_TILE
    grid_spec = pltpu.PrefetchScalarGridSpec(
        num_scalar_prefetch=1,
        grid=(B, n // TOK_TILE),
        in_specs=[pl.BlockSpec((1, E, TOK_TILE), lambda b, k, st: (b, 0, k)),
                  pl.BlockSpec((1, TOK_TILE, d), lambda b, k, st: (b, blk0 + k, 0))],
        out_specs=pl.BlockSpec((1, E, cap, d), lambda b, k, st: (b, 0, 0, 0)))
    return pl.pallas_call(
        functools.partial(_dispatch_kernel, cap=cap, win=min(SLOT_WIN, cap)),
        grid_spec=grid_spec,
        out_shape=jax.ShapeDtypeStruct((B, E, cap, d), BF16),
        compiler_params=_cparams(("parallel", "arbitrary"), 56),
        name="moe_dispatch",
    )(starts, slot, h2)


def _ffn_kernel(*refs, with_ctx):
    if with_ctx:
        xs_ref, xc_ref, w1_ref, w3_ref, w2_ref, y_ref, yc_ref, w1b, w3b, w2b = refs
    else:
        xs_ref, w1_ref, w3_ref, w2_ref, y_ref, w1b, w3b, w2b = refs

    @pl.when(pl.program_id(1) == 0)
    def _():
        step = 256
        for src, dst in ((w1_ref, w1b), (w3_ref, w3b), (w2_ref, w2b)):
            for r in range(0, src.shape[2], step):
                dst[r:r + step, :] = src[0, 0, r:r + step, :].astype(BF16)

    x = xs_ref[0, 0]
    cap = x.shape[0]
    if with_ctx:
        x = jnp.concatenate([x, xc_ref[0, 0]], axis=0)
    h1 = _dot(x, w1b[...])
    h3 = _dot(x, w3b[...])
    hid = (h1 * jax.nn.sigmoid(h1) * h3).astype(BF16)
    y = _dot(hid, w2b[...]).astype(BF16)
    y_ref[0, 0] = y[:cap]
    if with_ctx:
        yc_ref[0, 0] = y[cap:]


def _expert_ffn(xs, xs_ctx, w1, w3, w2, layer):
    B, E, cap, d = xs.shape
    f = w1.shape[3]
    slots = lambda c: pl.BlockSpec((1, 1, c, d), lambda e, b: (b, e, 0, 0))
    x_in, x_specs = [xs], [slots(cap)]
    out_specs, out_shape = [slots(cap)], [jax.ShapeDtypeStruct((B, E, cap, d), BF16)]
    if xs_ctx is not None:
        cap_c = xs_ctx.shape[2]
        x_in.append(xs_ctx)
        x_specs.append(slots(cap_c))
        out_specs.append(slots(cap_c))
        out_shape.append(jax.ShapeDtypeStruct((B, E, cap_c, d), BF16))
    outs = pl.pallas_call(
        functools.partial(_ffn_kernel, with_ctx=xs_ctx is not None),
        grid=(E, B),
        in_specs=x_specs + [pl.BlockSpec((1, 1, d, f), lambda e, b: (layer, e, 0, 0)),
                            pl.BlockSpec((1, 1, d, f), lambda e, b: (layer, e, 0, 0)),
                            pl.BlockSpec((1, 1, f, d), lambda e, b: (layer, e, 0, 0))],
        out_specs=out_specs,
        out_shape=out_shape,
        scratch_shapes=[pltpu.VMEM((d, f), BF16), pltpu.VMEM((d, f), BF16), pltpu.VMEM((f, d), BF16)],
        compiler_params=_cparams(("arbitrary", "arbitrary"), 56),
        name="moe_expert_ffn",
    )(*x_in, w1, w3, w2)
    return outs if xs_ctx is not None else (outs[0], None)


def _combine_kernel(start_ref, slotc_ref, aff_ref, y_ref, x_ref, gt2_ref, gpost_ref, xo_ref, acc_ref, *, cap, win):
    b, kt = pl.program_id(0), pl.program_id(1)
    slotc = slotc_ref[0]
    aff = aff_ref[0]
    tile = slotc.shape[0]
    firsts, rounds = _window_plan(start_ref, b, kt, cap, win)
    group = LANES // win
    lane = lax.broadcasted_iota(jnp.int32, (tile, LANES), 1)
    seg = lax.shift_right_logical(lane, _log2(win))
    off = lane - seg * win
    acc_ref[...] = jnp.zeros_like(acc_ref)

    def one_round(r, carry):
        hi_p, lo_p, rhs_p = [], [], []
        for j in range(N_EXPERTS // group):
            row0v = jnp.zeros((tile, LANES), jnp.int32)
            basev = jnp.zeros((tile, LANES), jnp.int32)
            scol = jnp.zeros((tile, LANES), F32)
            gate = jnp.zeros((tile, LANES), F32)
            for i in range(group):
                e = j * group + i
                base = firsts[e] + r * win
                row0 = pl.multiple_of(jnp.minimum(base, cap - win), BF16_ROWS)
                mine = seg == i
                row0v = jnp.where(mine, row0, row0v)
                basev = jnp.where(mine, base, basev)
                scol = jnp.where(mine, slotc[:, e:e + 1], scol)
                gate = jnp.where(mine, aff[:, e:e + 1], gate)
                rhs_p.append(y_ref[0, e, pl.ds(row0, win), :])
            abs_slot = row0v + off
            hit = (abs_slot.astype(F32) == scol) & (abs_slot >= basev)
            g = jnp.where(hit, gate, 0.0)
            g_hi = g.astype(BF16)
            hi_p.append(g_hi)
            lo_p.append((g - g_hi.astype(F32)).astype(BF16))
        lhs = jnp.concatenate([jnp.concatenate(hi_p, axis=1), jnp.concatenate(lo_p, axis=1)], axis=0)
        res = _dot(lhs, jnp.concatenate(rhs_p, axis=0))
        acc_ref[...] += res[:tile] + res[tile:]
        return carry

    lax.fori_loop(0, rounds, one_round, 0)
    xo_ref[0] = x_ref[0] + gt2_ref[0] * _rms(acc_ref[...], gpost_ref[...])


def _combine(starts, slotc, aff, y, X, mod3, g_post2, n, tok_off, is_ctx, in_place):
    B, T, d = X.shape
    E, cap = y.shape[1], y.shape[2]
    tt = TOK_TILE
    blk0 = tok_off // tt
    mrow = (lambda b: B) if is_ctx else (lambda b: b)
    if in_place:
        out_spec = pl.BlockSpec((1, tt, d), lambda b, t, st: (b, blk0 + t, 0))
        out_shape, aliases = jax.ShapeDtypeStruct((B, T, d), F32), {4: 0}
    else:
        out_spec = pl.BlockSpec((1, tt, d), lambda b, t, st: (b, t, 0))
        out_shape, aliases = jax.ShapeDtypeStruct((B, n, d), F32), {}
    grid_spec = pltpu.PrefetchScalarGridSpec(
        num_scalar_prefetch=1,
        grid=(B, n // tt),
        in_specs=[pl.BlockSpec((1, tt, E), lambda b, t, st: (b, t, 0)),
                  pl.BlockSpec((1, tt, LANES), lambda b, t, st: (b, blk0 + t, 0)),
                  pl.BlockSpec((1, E, cap, d), lambda b, t, st: (b, 0, 0, 0)),
                  pl.BlockSpec((1, tt, d), lambda b, t, st: (b, blk0 + t, 0)),
                  pl.BlockSpec((1, 1, d), lambda b, t, st: (mrow(b), 0, 5)),
                  pl.BlockSpec((1, d), lambda b, t, st: (0, 0))],
        out_specs=out_spec,
        scratch_shapes=[pltpu.VMEM((tt, d), F32)])
    return pl.pallas_call(
        functools.partial(_combine_kernel, cap=cap, win=min(SLOT_WIN, cap)),
        grid_spec=grid_spec,
        out_shape=out_shape,
        input_output_aliases=aliases,
        compiler_params=_cparams(("parallel", "arbitrary"), 56),
        name="moe_combine",
    )(starts, slotc, aff, y, X, mod3, g_post2)


def _moe(X, h2, aff, mod3, g_post2, w1, w3, w2, layer, n_lat, with_ctx, in_place):
    def route(n, off):
        cap = EC_FACTOR * n // N_EXPERTS
        aff_t = jnp.transpose(aff[:, off:off + n, :N_EXPERTS], (0, 2, 1))
        slot, starts = _select(aff_t, cap)
        return jnp.transpose(slot, (0, 2, 1)), starts, _dispatch(starts, slot, h2, cap, n, off)

    slotc, starts, xs = route(n_lat, 0)
    slotc_c, starts_c, xs_c = route(CTX_LEN, n_lat) if with_ctx else (None, None, None)
    y, y_c = _expert_ffn(xs, xs_c, w1, w3, w2, layer)
    X = _combine(starts, slotc, aff, y, X, mod3, g_post2, n_lat, 0, False, in_place)
    if with_ctx:
        X = _combine(starts_c, slotc_c, aff, y_c, X, mod3, g_post2, CTX_LEN, n_lat, True, True)
    return X


def _rope_partner(w):
    lead = w.shape[:-1]
    ws = w.reshape(lead + (2, 2, MLA_ROPE // 4))
    return jnp.stack([-ws[..., 1, :], ws[..., 0, :]], axis=-2).reshape(lead + (MLA_ROPE,))


def _pad_cols(w, left, total):
    return jnp.pad(w, [(0, 0)] * (w.ndim - 1) + [(left, total - left - w.shape[-1])])


def _prep_in_weights(w_in):
    sizes = (MLA_Q_RANK, MLA_KV_RANK, MLA_ROPE, 256, 256, 256, 256, 256, 512, 512, 4 * ML_HEADS, 3 * D_MODEL)
    offs = np.cumsum(sizes)[:-1].tolist()
    cq, ckv, kr, nq, nk, nv, mq, mk, mv, mo, mg, bg = jnp.split(w_in, offs, axis=-1)
    kr1 = _pad_cols(kr, MLA_NOPE, HEAD_PAD)
    kr2 = _pad_cols(_rope_partner(kr), MLA_NOPE, HEAD_PAD)
    mgp = _pad_cols(mg, 0, LANES)
    return jnp.concatenate([cq, ckv, kr1, kr2, nq, nk, nv, mq, mk, mv, mo, mgp, bg], axis=-1).astype(BF16)


def _prep_mla_weights(w_uq, w_ukv):
    nope, rope = w_uq[..., :MLA_NOPE], w_uq[..., MLA_NOPE:]
    wqa = _pad_cols(jnp.concatenate([nope, rope], axis=-1), 0, HEAD_PAD)
    wqb = _pad_cols(_rope_partner(rope), MLA_NOPE, HEAD_PAD)
    wk = _pad_cols(w_ukv[..., :MLA_NOPE], 0, HEAD_PAD)
    wv = _pad_cols(w_ukv[..., MLA_NOPE:], 0, HEAD_PAD)
    flat = lambda w: w.reshape(w.shape[0], MLA_HEADS * HEAD_PAD).astype(BF16)
    return flat(wqa), flat(wqb), flat(wk), flat(wv)


def _rope_tables(n_lat):
    t = np.arange(n_lat)
    pos = np.stack([t // GRID_W, t % GRID_W], axis=-1).astype(np.float64)
    nf = MLA_ROPE // 4
    inv = np.float32(ROPE_BASE) ** (-np.arange(nf, dtype=np.float32) / nf)
    ang = (pos[..., None].astype(np.float32) * inv).astype(np.float64)
    cos = np.repeat(np.cos(ang)[:, :, None, :], 2, axis=2).reshape(n_lat, MLA_ROPE)
    sin = np.repeat(np.sin(ang)[:, :, None, :], 2, axis=2).reshape(n_lat, MLA_ROPE)
    cos = np.concatenate([cos, np.ones((CTX_LEN, MLA_ROPE))], axis=0)
    sin = np.concatenate([sin, np.zeros((CTX_LEN, MLA_ROPE))], axis=0)
    T = n_lat + CTX_LEN
    z = lambda w: np.zeros((T, w))
    scale = (MLA_NOPE + MLA_ROPE) ** -0.5 * np.log2(np.e)
    ta = scale * np.concatenate([np.ones((T, MLA_NOPE)), cos, z(HEAD_PAD - MLA_NOPE - MLA_ROPE)], axis=1)
    tb = scale * np.concatenate([z(MLA_NOPE), sin, z(HEAD_PAD - MLA_NOPE - MLA_ROPE)], axis=1)
    tc = np.concatenate([z(MLA_NOPE), cos, z(HEAD_PAD - MLA_NOPE - MLA_ROPE)], axis=1)
    ts = np.concatenate([z(MLA_NOPE), sin, z(HEAD_PAD - MLA_NOPE - MLA_ROPE)], axis=1)
    return tuple(jnp.asarray(a, F32) for a in (ta, tb, tc, ts))


def kernel(x, c, ctx, c_ctx, w_mod, b_mod, g_pre1, g_post1, g_pre2, g_post2, w_in, g_q, g_kv, w_uq, w_ukv, rpb,
           b_ml_gates, g_ml, w_br_mla, w_br_na, w_br_ml, w_out, w_router, w_e1, w_e3, w_e2):
    B, n_lat, d = x.shape
    depth = w_mod.shape[0]
    rows = n_lat // GRID_W
    assert d == D_MODEL and ctx.shape[1] == CTX_LEN and n_lat % 512 == 0 and rows >= NA_KROWS + NA_QROWS and B % PAIR == 0

    X = jnp.concatenate([x, ctx], axis=1)
    mod_rows = -(-(B + PAIR) // 8) * 8
    c_all = jnp.concatenate([c, jnp.broadcast_to(c_ctx[None, :], (PAIR, d)),
                             jnp.zeros((mod_rows - B - PAIR, d), F32)], axis=0)
    mod = _modulation(c_all, w_mod, b_mod)
    tabs = _rope_tables(n_lat)
    row2 = lambda v: v.astype(F32)[None, :]

    for l in range(depth):
        ctx_out = l < depth - 1
        mod3 = mod[l][:, None, :]
        w_pad = _prep_in_weights(w_in[l])
        wqa, wqb, wk, wv = _prep_mla_weights(w_uq[l], w_ukv[l])
        (Q, K, V, nq, nk, nv, mq, mk, mv, mo, mg, bg) = _inproj(
            X, mod3, row2(g_pre1[l]), w_pad, row2(g_q[l]), row2(g_kv[l]), wqa, wqb, wk, wv, tabs, n_lat)
        a = jnp.concatenate([_mla_attention(Q, K, V, n_lat, False), _mla_attention(Q, K, V, n_lat, True)], axis=1)
        bn = _neighbourhood_attention(nq, nk, nv, _na_bias_table(rpb[l], rows), n_lat, True)
        mgt = jnp.transpose(mg[:, :, :4 * ML_HEADS], (0, 2, 1))
        hf, hb = _mlstm(mq, mk, mv, mg, mgt, b_ml_gates[l], n_lat)
        wr = _pad_cols(w_router[l].astype(F32), 0, LANES)
        wrh = wr.astype(BF16)
        wrl = (wr - wrh.astype(F32)).astype(BF16)
        X, h2, aff = _merge(a, bn, hf, hb, mo, bg, X, mod3, row2(g_ml[l]), row2(g_post1[l]), row2(g_pre2[l]),
                            w_br_mla[l].astype(BF16), w_br_na[l].astype(BF16), w_br_ml[l].astype(BF16),
                            w_out[l].astype(BF16), wrh, wrl, n_lat, True)
        X = _moe(X, h2, aff, mod3, row2(g_post2[l]), w_e1, w_e3, w_e2, l, n_lat, ctx_out, in_place=ctx_out)
    return X
```

```python
import functools

import numpy as np
import jax
import jax.numpy as jnp
from jax import lax
from jax.experimental import pallas as pl
from jax.experimental.pallas import tpu as pltpu

F32 = jnp.float32
BF16 = jnp.bfloat16

D_MODEL = 1024
CTX_LEN = 256
GRID_W = 64
EPS = 1e-6

MLA_HEADS = 4
MLA_Q_RANK = 256
MLA_KV_RANK = 128
MLA_NOPE = 64
MLA_ROPE = 32
MLA_V = 64
ROPE_BASE = 10000.0

NA_HEADS = 4
NA_DIM = 64
NA_KR = 8
NA_KC = 16
NA_QROWS = 4
NA_KROWS = 12

ML_HEADS = 4
ML_QK = 64
ML_V = 128
ML_CHUNK = 128

N_EXPERTS = 16
D_EXPERT = 1024
EC_FACTOR = 2

LANES = 128
ROW_TILE = 256
PAIR = 2
TOK_TILE = 256
SLOT_WIN = 64
BF16_ROWS = 16
HEAD_PAD = 128
NEG_BIG = -1e30

_IN_GROUPS = (("cq", 256), ("ckv", 128), ("kr1", 128), ("kr2", 128), ("nq", 256), ("nk", 256), ("nv", 256),
              ("mq", 256), ("mk", 256), ("mv", 512), ("mo", 512), ("mg", 128), ("bg", 3 * D_MODEL))
_OFF = {}
_o = 0
for _n, _w in _IN_GROUPS:
    _OFF[_n] = (_o, _o + _w)
    _o += _w
D_IN_PAD = _o


def _cparams(sem, vmem_mb):
    return pltpu.CompilerParams(dimension_semantics=sem, vmem_limit_bytes=vmem_mb * 1024 * 1024)


def _dot(a, b):
    return jnp.dot(a, b, preferred_element_type=F32)


def _dot_nt(a, b):
    return lax.dot_general(a, b, (((1,), (1,)), ((), ())), preferred_element_type=F32)


def _rms(x, g):
    return x * lax.rsqrt(jnp.mean(x * x, axis=-1, keepdims=True) + EPS) * g


def _split3(x):
    hi = x.astype(BF16)
    r1 = x - hi.astype(F32)
    mid = r1.astype(BF16)
    lo = (r1 - mid.astype(F32)).astype(BF16)
    return hi, mid, lo


def _log_sigmoid(x):
    return jnp.minimum(x, 0.0) - jnp.log(1.0 + jnp.exp(-jnp.abs(x)))


def _mod_kernel(c_ref, w_ref, b_ref, o_ref):
    cs = c_ref[...]
    cs = cs * jax.nn.sigmoid(cs)
    o_ref[0] = _dot(cs.astype(BF16), w_ref[0].astype(BF16)) + b_ref[0]


def _modulation(c_all, w_mod, b_mod):
    depth, d, n6 = w_mod.shape
    rows = c_all.shape[0]
    tn = 512
    return pl.pallas_call(
        _mod_kernel,
        grid=(depth, n6 // tn),
        in_specs=[pl.BlockSpec((rows, d), lambda l, j: (0, 0)),
                  pl.BlockSpec((1, d, tn), lambda l, j: (l, 0, j)),
                  pl.BlockSpec((1, 1, tn), lambda l, j: (l, 0, j))],
        out_specs=pl.BlockSpec((1, rows, tn), lambda l, j: (l, 0, j)),
        out_shape=jax.ShapeDtypeStruct((depth, rows, n6), F32),
        compiler_params=_cparams(("parallel", "parallel"), 32),
        name="modulation",
    )(c_all, w_mod, b_mod.reshape(depth, 1, n6))


def _inproj_kernel(x_ref, sh_ref, sc_ref, g1_ref, w_ref, gq_ref, gkv_ref, wqa_ref, wqb_ref, wk_ref, wv_ref,
                   ta_ref, tb_ref, tc_ref, ts_ref,
                   q_ref, k_ref, v_ref, nq_ref, nk_ref, nv_ref, mq_ref, mk_ref, mv_ref, mo_ref, mg_ref, bg_ref):
    nb, tm, d = x_ref.shape
    rows = nb * tm
    split = lambda a: a.reshape(nb, tm, a.shape[-1])
    h = split(_rms(x_ref[...].reshape(rows, d), g1_ref[...])) * (1.0 + sc_ref[...]) + sh_ref[...]
    hb = h.reshape(rows, d).astype(BF16)

    def proj(name, lo=None, hi=None):
        a, b = _OFF[name]
        if lo is not None:
            a, b = a + lo, a + hi
        return _dot(hb, w_ref[:, a:b])

    def group(first, last):
        names = [n for n, _ in _IN_GROUPS]
        names = names[names.index(first):names.index(last) + 1]
        lo = _OFF[first][0]
        z = _dot(hb, w_ref[:, lo:_OFF[last][1]])
        return {n: z[:, _OFF[n][0] - lo:_OFF[n][1] - lo] for n in names}

    g0 = group("cq", "kr2")
    cq, ckv = g0["cq"], g0["ckv"]
    krr = split(g0["kr1"]) * tc_ref[...][None] + split(g0["kr2"]) * ts_ref[...][None]
    g1 = group("nq", "mk")
    nq_ref[...] = split(g1["nq"]).astype(BF16)
    nk_ref[...] = split(g1["nk"]).astype(BF16)
    nv_ref[...] = split(g1["nv"]).astype(BF16)
    mq_ref[...] = split(g1["mq"]).astype(BF16)
    mk_ref[...] = split(g1["mk"]).astype(BF16)
    cqn = _rms(cq, gq_ref[...]).astype(BF16)
    qa = _dot(cqn, wqa_ref[...])
    qb = _dot(cqn, wqb_ref[...])
    ckvn = _rms(ckv, gkv_ref[...]).astype(BF16)
    kn = _dot(ckvn, wk_ref[...])
    vv = _dot(ckvn, wv_ref[...])
    g2 = group("mv", "mg")
    mv_ref[...] = split(g2["mv"]).astype(BF16)
    mo_ref[...] = split(g2["mo"]).astype(BF16)
    mg_ref[...] = split(g2["mg"])
    bg_ref[...] = split(proj("bg")).astype(BF16)
    ta = ta_ref[...][None]
    tb = tb_ref[...][None]
    one_col = (lax.broadcasted_iota(jnp.int32, (1, 1, HEAD_PAD), 2) == MLA_V).astype(F32)
    for hh in range(MLA_HEADS):
        sl = slice(hh * HEAD_PAD, (hh + 1) * HEAD_PAD)
        q_ref[:, hh] = (split(qa[:, sl]) * ta + split(qb[:, sl]) * tb).astype(BF16)
        k_ref[:, hh] = (split(kn[:, sl]) + krr).astype(BF16)
        v_ref[:, hh] = (split(vv[:, sl]) + one_col).astype(BF16)


def _inproj(X, mod3, g_pre1, w_pad, g_q, g_kv, wqa, wqb, wk, wv, tabs, n_lat):
    B, T, d = X.shape
    tm = ROW_TILE
    nb = PAIR
    nt = T // tm
    ctx_tile = n_lat // tm

    def mrow(b, i):
        return jnp.where(i >= ctx_tile, B // nb, b)

    full = lambda shape: pl.BlockSpec(shape, lambda b, i: (0,) * len(shape))
    tab = pl.BlockSpec((tm, HEAD_PAD), lambda b, i: (i, 0))
    row_out = lambda w: pl.BlockSpec((nb, tm, w), lambda b, i: (b, i, 0))
    head_out = pl.BlockSpec((nb, MLA_HEADS, tm, HEAD_PAD), lambda b, i: (b, 0, i, 0))
    sd = jax.ShapeDtypeStruct
    outs = pl.pallas_call(
        _inproj_kernel,
        grid=(B // nb, nt),
        in_specs=[pl.BlockSpec((nb, tm, d), lambda b, i: (b, i, 0)),
                  pl.BlockSpec((nb, 1, d), lambda b, i: (mrow(b, i), 0, 0)),
                  pl.BlockSpec((nb, 1, d), lambda b, i: (mrow(b, i), 0, 1)),
                  full((1, d)), full((d, D_IN_PAD)), full((1, MLA_Q_RANK)), full((1, MLA_KV_RANK)),
                  full((MLA_Q_RANK, MLA_HEADS * HEAD_PAD)), full((MLA_Q_RANK, MLA_HEADS * HEAD_PAD)),
                  full((MLA_KV_RANK, MLA_HEADS * HEAD_PAD)), full((MLA_KV_RANK, MLA_HEADS * HEAD_PAD)),
                  tab, tab, tab, tab],
        out_specs=[head_out, head_out, head_out,
                   row_out(256), row_out(256), row_out(256), row_out(256), row_out(256),
                   row_out(512), row_out(512), row_out(128), row_out(3 * d)],
        out_shape=[sd((B, MLA_HEADS, T, HEAD_PAD), BF16)] * 3
                  + [sd((B, T, 256), BF16)] * 5 + [sd((B, T, 512), BF16)] * 2
                  + [sd((B, T, 128), F32), sd((B, T, 3 * d), BF16)],
        compiler_params=_cparams(("parallel", "parallel"), 56),
        name="inproj",
    )(X, mod3, mod3, g_pre1, w_pad, g_q, g_kv, wqa, wqb, wk, wv, *tabs)
    return outs


def _flash_kernel(q_ref, k_ref, v_ref, o_ref, s0_ref, s1_ref, p0_ref, p1_ref, *, nk, rb):
    tq = q_ref.shape[2]
    s_bufs, p_bufs = (s0_ref, s1_ref), (p0_ref, p1_ref)

    def scores(hh):
        s_bufs[hh % 2][...] = _dot_nt(q_ref[0, hh], k_ref[0, hh, :nk, :])

    scores(0)
    for hh in range(MLA_HEADS):
        if hh + 1 < MLA_HEADS:
            scores(hh + 1)
        s_ref, p_ref = s_bufs[hh % 2], p_bufs[hh % 2]
        for r in range(0, tq, rb):
            s = s_ref[r:r + rb, :]
            m = jnp.max(s, axis=1, keepdims=True)
            p_ref[r:r + rb, :] = jnp.exp2(s - m).astype(BF16)
        acc = _dot(p_ref[...], v_ref[0, hh, :nk, :])
        o = acc[:, :MLA_V] / acc[:, MLA_V:MLA_V + 1]
        o_ref[0, :, hh * MLA_V:(hh + 1) * MLA_V] = o.astype(BF16)


def _mla_attention(Q, K, V, n_lat, context):
    B, H, T, E = Q.shape
    kc = 256
    if not context:
        tq, nk, nq = 512, T, n_lat
        q_spec = pl.BlockSpec((1, H, tq, E), lambda b, i: (b, 0, i, 0))
        kv_spec = pl.BlockSpec((1, H, T, E), lambda b, i: (b, 0, 0, 0))
    else:
        tq, nk, nq = CTX_LEN, CTX_LEN, CTX_LEN
        cblk = n_lat // CTX_LEN
        q_spec = pl.BlockSpec((1, H, tq, E), lambda b, i: (b, 0, cblk, 0))
        kv_spec = pl.BlockSpec((1, H, nk, E), lambda b, i: (b, 0, cblk, 0))
    return pl.pallas_call(
        functools.partial(_flash_kernel, nk=nk, rb=64),
        grid=(B, nq // tq),
        in_specs=[q_spec, kv_spec, kv_spec],
        out_specs=pl.BlockSpec((1, tq, H * MLA_V), lambda b, i: (b, i, 0)),
        out_shape=jax.ShapeDtypeStruct((B, nq, H * MLA_V), BF16),
        scratch_shapes=[pltpu.VMEM((tq, nk), F32), pltpu.VMEM((tq, nk), F32),
                        pltpu.VMEM((tq, nk), BF16), pltpu.VMEM((tq, nk), BF16)],
        compiler_params=_cparams(("parallel", "parallel"), 56),
        name="mla_attention_ctx" if context else "mla_attention",
    )(Q, K, V)


def _na_kernel(q_ref, k_ref, v_ref, bias_ref, o_ref, *, n_lat, rows, with_ctx):
    j = pl.program_id(1)
    nblk = rows // NA_QROWS
    scale = NA_DIM ** -0.5
    assert np.log2(scale) == round(np.log2(scale))
    q = (q_ref[0].astype(F32) * scale).astype(BF16)
    kctx = k_ref[0, n_lat:n_lat + CTX_LEN, :]
    vctx = v_ref[0, n_lat:n_lat + CTX_LEN, :]

    def local_step():
        kb = jnp.clip(j * NA_QROWS - NA_KR // 2, 0, rows - NA_KROWS)
        start = pl.multiple_of(kb * GRID_W, GRID_W)
        kloc = k_ref[0, pl.ds(start, NA_KROWS * GRID_W), :]
        vloc = v_ref[0, pl.ds(start, NA_KROWS * GRID_W), :]
        for hh in range(NA_HEADS):
            sl = slice(hh * NA_DIM, (hh + 1) * NA_DIM)
            qh = q[:, sl]
            s_loc = _dot_nt(qh, kloc[:, sl]) + bias_ref[0, hh]
            s_ctx = _dot_nt(qh, kctx[:, sl])
            m = jnp.maximum(jnp.max(s_loc, axis=1, keepdims=True), jnp.max(s_ctx, axis=1, keepdims=True))
            p_loc = jnp.exp(s_loc - m)
            p_ctx = jnp.exp(s_ctx - m)
            den = jnp.sum(p_loc, axis=1, keepdims=True) + jnp.sum(p_ctx, axis=1, keepdims=True)
            o = _dot(p_loc.astype(BF16), vloc[:, sl]) + _dot(p_ctx.astype(BF16), vctx[:, sl])
            o_ref[0, :, sl] = (o / den).astype(BF16)

    def ctx_step():
        for hh in range(NA_HEADS):
            sl = slice(hh * NA_DIM, (hh + 1) * NA_DIM)
            s = _dot_nt(q[:, sl], kctx[:, sl])
            p = jnp.exp(s - jnp.max(s, axis=1, keepdims=True))
            den = jnp.sum(p, axis=1, keepdims=True)
            o_ref[0, :, sl] = (_dot(p.astype(BF16), vctx[:, sl]) / den).astype(BF16)

    if with_ctx:
        pl.when(j < nblk)(local_step)
        pl.when(j == nblk)(ctx_step)
    else:
        local_step()


def _na_bias_table(rpb, rows):
    W = GRID_W
    nr, ncol = 2 * NA_KR - 1, 2 * NA_KC - 1
    w = np.arange(W)[:, None]
    c = np.arange(W)[None, :]
    cs = np.clip(w - NA_KC // 2, 0, W - NA_KC)
    col_ok = (c >= cs) & (c < cs + NA_KC)
    sel_c = (col_ok[:, :, None] & ((c - w + NA_KC - 1)[:, :, None] == np.arange(ncol))).astype(np.float32)
    sel_r = np.zeros((3, NA_QROWS, NA_KROWS, nr), np.float32)
    for p, r0 in enumerate((0, NA_QROWS, rows - NA_QROWS)):
        kb = int(np.clip(r0 - NA_KR // 2, 0, rows - NA_KROWS))
        for q in range(NA_QROWS):
            rs = int(np.clip(r0 + q - NA_KR // 2, 0, rows - NA_KR))
            for k in range(NA_KROWS):
                if rs <= kb + k < rs + NA_KR:
                    sel_r[p, q, k, kb + k - (r0 + q) + NA_KR - 1] = 1.0
    hi = lax.Precision.HIGHEST
    by_col = jnp.einsum("hrd,wcd->hrwc", rpb.astype(F32), jnp.asarray(sel_c), precision=hi)
    bias = jnp.einsum("pqkr,hrwc->phqwkc", jnp.asarray(sel_r), by_col, precision=hi)
    valid = np.einsum("pqk,wc->pqwkc", sel_r.sum(-1), col_ok.astype(np.float32)) > 0
    bias = jnp.where(jnp.asarray(valid)[:, None], bias, NEG_BIG)
    return bias.reshape(3, NA_HEADS, NA_QROWS * W, NA_KROWS * W)


def _neighbourhood_attention(nq, nk, nv, bias_tab, n_lat, with_ctx):
    B, T, w = nq.shape
    rows = n_lat // GRID_W
    nblk = rows // NA_QROWS
    tq = NA_QROWS * GRID_W
    steps = nblk + (1 if with_ctx else 0)

    def pat(j):
        return jnp.where(j == 0, 0, jnp.where(j >= nblk - 1, 2, 1))

    return pl.pallas_call(
        functools.partial(_na_kernel, n_lat=n_lat, rows=rows, with_ctx=with_ctx),
        grid=(B, steps),
        in_specs=[pl.BlockSpec((1, tq, w), lambda b, j: (b, j, 0)),
                  pl.BlockSpec((1, T, w), lambda b, j: (b, 0, 0)),
                  pl.BlockSpec((1, T, w), lambda b, j: (b, 0, 0)),
                  pl.BlockSpec((1, NA_HEADS, tq, NA_KROWS * GRID_W), lambda b, j: (pat(j), 0, 0, 0))],
        out_specs=pl.BlockSpec((1, tq, w), lambda b, j: (b, j, 0)),
        out_shape=jax.ShapeDtypeStruct((B, T, w), BF16),
        compiler_params=_cparams(("parallel", "arbitrary"), 48),
        name="neighbourhood_attention",
    )(nq, nk, nv, bias_tab)


def _mlstm_kernel(qf_ref, kf_ref, vf_ref, gf_ref, gtf_ref, qb_ref, kb_ref, vb_ref, gb_ref, gtb_ref,
                  bcol_ref, brow_ref, hf_ref, hb_ref, st_ref, m_ref):
    L = ML_CHUNK

    @pl.when(pl.program_id(1) == 0)
    def _():
        st_ref[...] = jnp.zeros_like(st_ref)
        m_ref[...] = jnp.zeros_like(m_ref)

    r_i = lax.broadcasted_iota(jnp.int32, (L, L), 0)
    c_i = lax.broadcasted_iota(jnp.int32, (L, L), 1)
    ones_t = jnp.ones((ML_V, L), BF16)
    dirs = ((qf_ref, kf_ref, vf_ref, gf_ref, gtf_ref, hf_ref), (qb_ref, kb_ref, vb_ref, gb_ref, gtb_ref, hb_ref))
    nb = qf_ref.shape[0]
    ng = 4 * ML_HEADS
    for bi, d in [(bi, d) for d in range(2) for bi in range(nb)]:
        q_ref, k_ref, v_ref, g_ref, gt_ref, h_ref = dirs[d]
        if bi == 0:
            causal_t = (r_i <= c_i) if d == 0 else (r_i >= c_i)
            m_row = jnp.where(causal_t, 1.0, 0.0).astype(BF16)
            m_col = jnp.where((c_i <= r_i) if d == 0 else (c_i >= r_i), 1.0, 0.0).astype(BF16)
            gt_all = jnp.concatenate([gt_ref[i] + bcol_ref[...] for i in range(nb)], axis=0)
            lf_r = _log_sigmoid(gt_all)
            br_all = sum(_dot(p, m_row) for p in _split3(lf_r))
            tot_all = jnp.sum(lf_r, axis=1, keepdims=True)
            gc_all = jnp.concatenate([g_ref[i] + brow_ref[...] for i in range(nb)], axis=1)
            bc_all = sum(_dot(m_col, p) for p in _split3(_log_sigmoid(gc_all)))
        gt = gt_all[bi * ng:(bi + 1) * ng, :]
        br = br_all[bi * ng:(bi + 1) * ng, :]
        tot = tot_all[bi * ng:(bi + 1) * ng, :]
        gc = gc_all[:, bi * LANES:(bi + 1) * LANES]
        bc = bc_all[:, bi * LANES:(bi + 1) * LANES]
        k16 = (k_ref[bi].astype(F32) * (ML_QK ** -0.5)).astype(BF16)
        q16 = q_ref[bi]
        q_t = q16.astype(F32).T
        v_t = v_ref[bi].astype(F32).T.astype(BF16)
        for hh in range(ML_HEADS):
            c = (bi * 2 + d) * ML_HEADS + hh
            ji = (2 * d) * ML_HEADS + hh
            jf = (2 * d + 1) * ML_HEADS + hh
            i_row = gt[ji:ji + 1, :]
            b_row = br[jf:jf + 1, :]
            b_last = tot[jf:jf + 1, :]
            src_col = gc[:, ji:ji + 1] - bc[:, jf:jf + 1]
            m_old = m_ref[c:c + 1, 0:1]
            st_old = st_ref[c]
            qh = q16[:, hh * ML_QK:(hh + 1) * ML_QK]
            kh = k16[:, hh * ML_QK:(hh + 1) * ML_QK]
            qh_t = q_t[hh * ML_QK:(hh + 1) * ML_QK, :]
            va_t = jnp.concatenate([v_t[hh * ML_V:(hh + 1) * ML_V, :], ones_t], axis=0)
            d_log = jnp.where(causal_t, src_col + b_row, -jnp.inf)
            inter = b_row + m_old
            m_t = jnp.maximum(inter, jnp.max(d_log, axis=0, keepdims=True))
            a = jnp.exp(inter - m_t)
            smat_t = (_dot_nt(kh, qh) * jnp.exp(d_log - m_t)).astype(BF16)
            lhs = jnp.concatenate([va_t, st_old.astype(BF16)], axis=1)
            rhs = jnp.concatenate([smat_t, (qh_t * a).astype(BF16)], axis=0)
            out_t = _dot(lhs, rhs)
            den = jnp.maximum(jnp.abs(out_t[ML_V:, :]), jnp.exp(-m_t))
            h_ref[bi, :, hh * ML_V:(hh + 1) * ML_V] = (out_t[:ML_V, :] / den).T
            w_log = b_last - b_row + i_row
            m_new = jnp.maximum(b_last + m_old, jnp.max(w_log, axis=1, keepdims=True))
            decay = jnp.exp(b_last + m_old - m_new)
            vw_t = (va_t.astype(F32) * jnp.exp(w_log - m_new)).astype(BF16)
            st_ref[c] = decay * st_old + _dot(vw_t, kh)
            m_ref[c:c + 1, :] = jnp.broadcast_to(m_new, (1, LANES))


def _mlstm(mq, mk, mv, mg, mgt, b_ml, n_lat):
    B, T, _ = mq.shape
    L = ML_CHUNK
    nl = n_lat // L
    nc = CTX_LEN // L
    steps = nl + nc

    def cf(s):
        return jnp.where(s < nc, nl + s, s - nc)

    def cb(s):
        return steps - 1 - s

    nb = PAIR

    def specs(cm):
        return [pl.BlockSpec((nb, L, ML_HEADS * ML_QK), lambda b, s: (b, cm(s), 0)),
                pl.BlockSpec((nb, L, ML_HEADS * ML_QK), lambda b, s: (b, cm(s), 0)),
                pl.BlockSpec((nb, L, ML_HEADS * ML_V), lambda b, s: (b, cm(s), 0)),
                pl.BlockSpec((nb, L, LANES), lambda b, s: (b, cm(s), 0)),
                pl.BlockSpec((nb, 4 * ML_HEADS, L), lambda b, s: (b, 0, cm(s)))]

    b_row = jnp.concatenate([b_ml.astype(F32), jnp.zeros((LANES - 4 * ML_HEADS,), F32)])[None, :]
    b_col = b_ml.astype(F32)[:, None]
    sd = jax.ShapeDtypeStruct
    n_chain = nb * 2 * ML_HEADS
    return pl.pallas_call(
        _mlstm_kernel,
        grid=(B // nb, steps),
        in_specs=specs(cf) + specs(cb) + [pl.BlockSpec((4 * ML_HEADS, 1), lambda b, s: (0, 0)),
                                          pl.BlockSpec((1, LANES), lambda b, s: (0, 0))],
        out_specs=[pl.BlockSpec((nb, L, ML_HEADS * ML_V), lambda b, s: (b, cf(s), 0)),
                   pl.BlockSpec((nb, L, ML_HEADS * ML_V), lambda b, s: (b, cb(s), 0))],
        out_shape=[sd((B, T, ML_HEADS * ML_V), F32)] * 2,
        scratch_shapes=[pltpu.VMEM((n_chain, 2 * ML_V, ML_QK), F32), pltpu.VMEM((n_chain, LANES), F32)],
        compiler_params=_cparams(("parallel", "arbitrary"), 32),
        name="mlstm",
    )(mq, mk, mv, mg, mgt, mq, mk, mv, mg, mgt, b_col, b_row)


def _merge_kernel(a_ref, bn_ref, hf_ref, hb_ref, mo_ref, bg_ref, x_ref, gt1_ref, sh2_ref, sc2_ref,
                  gml_ref, gpost_ref, gpre2_ref, wa_ref, wb_ref, wm_ref, wo_ref, wrh_ref, wrl_ref,
                  xo_ref, h2_ref, aff_ref):
    nb, tm, d = x_ref.shape
    gml = gml_ref[...]
    for bi in range(nb):
        hm = hf_ref[bi] + hb_ref[bi]
        parts = []
        for hh in range(ML_HEADS):
            sl = slice(hh * ML_V, (hh + 1) * ML_V)
            parts.append(_rms(hm[:, sl], gml[:, sl]))
        mlo = (jnp.concatenate(parts, axis=1) * jax.nn.sigmoid(mo_ref[bi].astype(F32))).astype(BF16)
        ya = _dot(a_ref[bi], wa_ref[...])
        yb = _dot(bn_ref[bi], wb_ref[...])
        ym = _dot(mlo, wm_ref[...])
        y = (jax.nn.sigmoid(bg_ref[bi, :, 0:d].astype(F32)) * ya
             + jax.nn.sigmoid(bg_ref[bi, :, d:2 * d].astype(F32)) * yb
             + jax.nn.sigmoid(bg_ref[bi, :, 2 * d:3 * d].astype(F32)) * ym)
        out = _dot(y.astype(BF16), wo_ref[...])
        xn = x_ref[bi] + gt1_ref[bi] * _rms(out, gpost_ref[...])
        xo_ref[bi] = xn
        h2 = _rms(xn, gpre2_ref[...]) * (1.0 + sc2_ref[bi]) + sh2_ref[bi]
        h2_ref[bi] = h2.astype(BF16)
        hi = h2.astype(BF16)
        lo = (h2 - hi.astype(F32)).astype(BF16)
        logits = _dot(hi, wrh_ref[...]) + _dot(lo, wrh_ref[...]) + _dot(hi, wrl_ref[...])
        lane = lax.broadcasted_iota(jnp.int32, logits.shape, 1)
        logits = jnp.where(lane < N_EXPERTS, logits, -jnp.inf)
        e = jnp.exp(logits - jnp.max(logits, axis=1, keepdims=True))
        aff_ref[bi] = e / jnp.sum(e, axis=1, keepdims=True)


def _merge(a, bn, hf, hb, mo, bg, X, mod3, g_ml, g_post1, g_pre2, wa, wb, wm, wo, wrh, wrl, n_lat, with_ctx):
    B, T, d = X.shape
    tm = ROW_TILE
    ctx_tile = n_lat // tm
    nt = ctx_tile + (1 if with_ctx else 0)

    nb = PAIR

    def mrow(b, i):
        return jnp.where(i >= ctx_tile, B // nb, b)

    full = lambda shape: pl.BlockSpec(shape, lambda b, i: (0,) * len(shape))
    row = lambda w: pl.BlockSpec((nb, tm, w), lambda b, i: (b, i, 0))
    modc = lambda ch: pl.BlockSpec((nb, 1, d), lambda b, i: (mrow(b, i), 0, ch))
    sd = jax.ShapeDtypeStruct
    return pl.pallas_call(
        _merge_kernel,
        grid=(B // nb, nt),
        in_specs=[row(256), row(256), row(512), row(512), row(512), row(3 * d), row(d),
                  modc(2), modc(3), modc(4),
                  full((1, ML_HEADS * ML_V)), full((1, d)), full((1, d)),
                  full((256, d)), full((256, d)), full((512, d)), full((d, d)), full((d, LANES)), full((d, LANES))],
        out_specs=[row(d), row(d), row(LANES)],
        out_shape=[sd((B, T, d), F32), sd((B, T, d), BF16), sd((B, T, LANES), F32)],
        compiler_params=_cparams(("parallel", "parallel"), 48),
        name="merge",
    )(a, bn, hf, hb, mo, bg, X, mod3, mod3, mod3, g_ml, g_post1, g_pre2, wa, wb, wm, wo, wrh, wrl)


def _select_kernel(aff_ref, slot_ref, start_ref, *, cap):
    aff = aff_ref[0]
    n = aff.shape[1]
    bits = pltpu.bitcast(aff, jnp.int32)
    capf = jnp.float32(cap)

    def count(mask):
        return jnp.sum(jnp.where(mask, 1.0, 0.0), axis=1, keepdims=True)

    def refine(it, thr):
        cand = thr | jnp.left_shift(jnp.int32(1), 30 - it)
        return jnp.where(count(bits >= cand) >= capf, cand, thr)

    thr = lax.fori_loop(0, 31, refine, jnp.zeros((N_EXPERTS, 1), jnp.int32))
    gt = bits > thr
    eq = bits == thr
    need = capf - count(gt)
    r_i = lax.broadcasted_iota(jnp.int32, (LANES, LANES), 0)
    c_i = lax.broadcasted_iota(jnp.int32, (LANES, LANES), 1)
    before = jnp.where(r_i < c_i, 1.0, 0.0).astype(BF16)

    def excl_prefix(mask):
        carry = jnp.zeros((N_EXPERTS, 1), F32)
        outs, carries = [], []
        for c in range(n // LANES):
            carries.append(carry)
            mc = jnp.where(mask[:, c * LANES:(c + 1) * LANES], 1.0, 0.0)
            outs.append(_dot(mc.astype(BF16), before) + carry)
            carry = carry + jnp.sum(mc, axis=1, keepdims=True)
        return jnp.concatenate(outs, axis=1), carries + [carry]

    sel = gt | (eq & (excl_prefix(eq)[0] < need))
    rank, carries = excl_prefix(sel)
    slot_ref[0] = jnp.where(sel, rank, -1.0)
    lane = lax.broadcasted_iota(jnp.int32, (N_EXPERTS, LANES), 1)
    per_tile = TOK_TILE // LANES
    starts = jnp.zeros((N_EXPERTS, LANES), F32)
    for k in range(n // TOK_TILE + 1):
        starts = jnp.where(lane == k, carries[k * per_tile], starts)
    start_ref[0] = starts


def _select(aff_t, cap):
    B, E, n = aff_t.shape
    slot, starts = pl.pallas_call(
        functools.partial(_select_kernel, cap=cap),
        grid=(B,),
        in_specs=[pl.BlockSpec((1, E, n), lambda b: (b, 0, 0))],
        out_specs=[pl.BlockSpec((1, E, n), lambda b: (b, 0, 0)), pl.BlockSpec((1, E, LANES), lambda b: (b, 0, 0))],
        out_shape=[jax.ShapeDtypeStruct((B, E, n), F32), jax.ShapeDtypeStruct((B, E, LANES), F32)],
        compiler_params=_cparams(("parallel",), 32),
        name="route_select",
    )(aff_t)
    return slot, starts[:, :, :n // TOK_TILE + 1].astype(jnp.int32)


def _log2(v):
    assert v & (v - 1) == 0
    return v.bit_length() - 1


def _window_plan(start_ref, b, kt, cap, win):
    firsts, rounds = [], jnp.int32(0)
    for e in range(N_EXPERTS):
        lo = start_ref[b, e, kt]
        hi = start_ref[b, e, kt + 1]
        first = jnp.minimum(lax.shift_right_logical(lo, _log2(BF16_ROWS)) * BF16_ROWS, cap - win)
        firsts.append(first)
        rounds = jnp.maximum(rounds, lax.shift_right_logical(hi - first + (win - 1), _log2(win)))
    return firsts, rounds


def _dispatch_kernel(start_ref, slot_ref, h_ref, xs_ref, *, cap, win):
    b, kt = pl.program_id(0), pl.program_id(1)

    @pl.when(kt == 0)
    def _():
        xs_ref[...] = jnp.zeros_like(xs_ref)

    slot = slot_ref[0]
    tile = slot.shape[1]
    h = h_ref[0]
    firsts, rounds = _window_plan(start_ref, b, kt, cap, win)
    sub = lax.broadcasted_iota(jnp.int32, (win, tile), 0)

    def one_round(r, carry):
        rows, pieces = [], []
        for e in range(N_EXPERTS):
            base = firsts[e] + r * win
            row0 = pl.multiple_of(jnp.minimum(base, cap - win), BF16_ROWS)
            abs_slot = row0 + sub
            hit = (abs_slot.astype(F32) == slot[e:e + 1, :]) & (abs_slot >= base)
            pieces.append(jnp.where(hit, 1.0, 0.0).astype(BF16))
            rows.append(row0)
        res = _dot(jnp.concatenate(pieces, axis=0), h)
        for e in range(N_EXPERTS):
            cur = xs_ref[0, e, pl.ds(rows[e], win), :]
            xs_ref[0, e, pl.ds(rows[e], win), :] = cur + res[e * win:(e + 1) * win].astype(BF16)
        return carry

    lax.fori_loop(0, rounds, one_round, 0)


def _dispatch(starts, slot, h2, cap, n, tok_off):
    B, E, _ = slot.shape
    d = h2.shape[2]
    blk0 = tok_off // TOK_TILE
    grid_spec = pltpu.PrefetchScalarGridSpec(
        num_scalar_prefetch=1,
        grid=(B, n // TOK_TILE),
        in_specs=[pl.BlockSpec((1, E, TOK_TILE), lambda b, k, st: (b, 0, k)),
                  pl.BlockSpec((1, TOK_TILE, d), lambda b, k, st: (b, blk0 + k, 0))],
        out_specs=pl.BlockSpec((1, E, cap, d), lambda b, k, st: (b, 0, 0, 0)))
    return pl.pallas_call(
        functools.partial(_dispatch_kernel, cap=cap, win=min(SLOT_WIN, cap)),
        grid_spec=grid_spec,
        out_shape=jax.ShapeDtypeStruct((B, E, cap, d), BF16),
        compiler_params=_cparams(("parallel", "arbitrary"), 56),
        name="moe_dispatch",
    )(starts, slot, h2)


def _ffn_kernel(*refs, with_ctx):
    if with_ctx:
        xs_ref, xc_ref, w1_ref, w3_ref, w2_ref, y_ref, yc_ref, w1b, w3b, w2b = refs
    else:
        xs_ref, w1_ref, w3_ref, w2_ref, y_ref, w1b, w3b, w2b = refs

    @pl.when(pl.program_id(1) == 0)
    def _():
        step = 256
        for src, dst in ((w1_ref, w1b), (w3_ref, w3b), (w2_ref, w2b)):
            for r in range(0, src.shape[2], step):
                dst[r:r + step, :] = src[0, 0, r:r + step, :].astype(BF16)

    x = xs_ref[0, 0]
    cap = x.shape[0]
    if with_ctx:
        x = jnp.concatenate([x, xc_ref[0, 0]], axis=0)
    h1 = _dot(x, w1b[...])
    h3 = _dot(x, w3b[...])
    hid = (h1 * jax.nn.sigmoid(h1) * h3).astype(BF16)
    y = _dot(hid, w2b[...]).astype(BF16)
    y_ref[0, 0] = y[:cap]
    if with_ctx:
        yc_ref[0, 0] = y[cap:]


def _expert_ffn(xs, xs_ctx, w1, w3, w2, layer):
    B, E, cap, d = xs.shape
    f = w1.shape[3]
    slots = lambda c: pl.BlockSpec((1, 1, c, d), lambda e, b: (b, e, 0, 0))
    x_in, x_specs = [xs], [slots(cap)]
    out_specs, out_shape = [slots(cap)], [jax.ShapeDtypeStruct((B, E, cap, d), BF16)]
    if xs_ctx is not None:
        cap_c = xs_ctx.shape[2]
        x_in.append(xs_ctx)
        x_specs.append(slots(cap_c))
        out_specs.append(slots(cap_c))
        out_shape.append(jax.ShapeDtypeStruct((B, E, cap_c, d), BF16))
    outs = pl.pallas_call(
        functools.partial(_ffn_kernel, with_ctx=xs_ctx is not None),
        grid=(E, B),
        in_specs=x_specs + [pl.BlockSpec((1, 1, d, f), lambda e, b: (layer, e, 0, 0)),
                            pl.BlockSpec((1, 1, d, f), lambda e, b: (layer, e, 0, 0)),
                            pl.BlockSpec((1, 1, f, d), lambda e, b: (layer, e, 0, 0))],
        out_specs=out_specs,
        out_shape=out_shape,
        scratch_shapes=[pltpu.VMEM((d, f), BF16), pltpu.VMEM((d, f), BF16), pltpu.VMEM((f, d), BF16)],
        compiler_params=_cparams(("arbitrary", "arbitrary"), 56),
        name="moe_expert_ffn",
    )(*x_in, w1, w3, w2)
    return outs if xs_ctx is not None else (outs[0], None)


def _combine_kernel(start_ref, slotc_ref, aff_ref, y_ref, x_ref, gt2_ref, gpost_ref, xo_ref, acc_ref, *, cap, win):
    b, kt = pl.program_id(0), pl.program_id(1)
    slotc = slotc_ref[0]
    aff = aff_ref[0]
    tile = slotc.shape[0]
    firsts, rounds = _window_plan(start_ref, b, kt, cap, win)
    width = N_EXPERTS * win
    col = lax.broadcasted_iota(jnp.int32, (1, width), 1)
    owner = lax.shift_right_logical(col, _log2(win))
    off = col - owner * win
    spread = jnp.where(lax.broadcasted_iota(jnp.int32, (LANES, width), 0) == owner, 1.0, 0.0).astype(BF16)
    half = jnp.floor(slotc * 0.5)
    g_hi = aff.astype(BF16)
    g_lo = (aff - g_hi.astype(F32)).astype(BF16)
    stacked = jnp.concatenate([half.astype(BF16), (slotc - 2.0 * half).astype(BF16), g_hi, g_lo], axis=0)
    rep = _dot(stacked, spread)
    scol = 2.0 * rep[:tile] + rep[tile:2 * tile]
    ghi_rep = rep[2 * tile:3 * tile]
    glo_rep = rep[3 * tile:]
    acc_ref[...] = jnp.zeros_like(acc_ref)

    def one_round(r, carry):
        row0v = jnp.zeros((1, width), jnp.int32)
        basev = jnp.zeros((1, width), jnp.int32)
        rhs_p = []
        for e in range(N_EXPERTS):
            base = firsts[e] + r * win
            row0 = pl.multiple_of(jnp.minimum(base, cap - win), BF16_ROWS)
            row0v = jnp.where(owner == e, row0, row0v)
            basev = jnp.where(owner == e, base, basev)
            rhs_p.append(y_ref[0, e, pl.ds(row0, win), :])
        abs_slot = row0v + off
        hit = (abs_slot.astype(F32) == scol) & (abs_slot >= basev)
        lhs = jnp.concatenate([jnp.where(hit, ghi_rep, 0.0).astype(BF16),
                               jnp.where(hit, glo_rep, 0.0).astype(BF16)], axis=0)
        res = _dot(lhs, jnp.concatenate(rhs_p, axis=0))
        acc_ref[...] += res[:tile] + res[tile:]
        return carry

    lax.fori_loop(0, rounds, one_round, 0)
    xo_ref[0] = x_ref[0] + gt2_ref[0] * _rms(acc_ref[...], gpost_ref[...])


def _combine(starts, slotc, aff, y, X, mod3, g_post2, n, tok_off, is_ctx, in_place):
    B, T, d = X.shape
    E, cap = y.shape[1], y.shape[2]
    tt = TOK_TILE
    blk0 = tok_off // tt
    mrow = (lambda b: B) if is_ctx else (lambda b: b)
    if in_place:
        out_spec = pl.BlockSpec((1, tt, d), lambda b, t, st: (b, blk0 + t, 0))
        out_shape, aliases = jax.ShapeDtypeStruct((B, T, d), F32), {4: 0}
    else:
        out_spec = pl.BlockSpec((1, tt, d), lambda b, t, st: (b, t, 0))
        out_shape, aliases = jax.ShapeDtypeStruct((B, n, d), F32), {}
    grid_spec = pltpu.PrefetchScalarGridSpec(
        num_scalar_prefetch=1,
        grid=(B, n // tt),
        in_specs=[pl.BlockSpec((1, tt, LANES), lambda b, t, st: (b, t, 0)),
                  pl.BlockSpec((1, tt, LANES), lambda b, t, st: (b, blk0 + t, 0)),
                  pl.BlockSpec((1, E, cap, d), lambda b, t, st: (b, 0, 0, 0)),
                  pl.BlockSpec((1, tt, d), lambda b, t, st: (b, blk0 + t, 0)),
                  pl.BlockSpec((1, 1, d), lambda b, t, st: (mrow(b), 0, 5)),
                  pl.BlockSpec((1, d), lambda b, t, st: (0, 0))],
        out_specs=out_spec,
        scratch_shapes=[pltpu.VMEM((tt, d), F32)])
    return pl.pallas_call(
        functools.partial(_combine_kernel, cap=cap, win=min(SLOT_WIN, cap)),
        grid_spec=grid_spec,
        out_shape=out_shape,
        input_output_aliases=aliases,
        compiler_params=_cparams(("parallel", "arbitrary"), 56),
        name="moe_combine",
    )(starts, slotc, aff, y, X, mod3, g_post2)


def _moe(X, h2, aff, mod3, g_post2, w1, w3, w2, layer, n_lat, with_ctx, in_place):
    def route(n, off):
        cap = EC_FACTOR * n // N_EXPERTS
        aff_t = jnp.transpose(aff[:, off:off + n, :N_EXPERTS], (0, 2, 1))
        slot, starts = _select(aff_t, cap)
        slotc = _pad_cols(jnp.transpose(slot, (0, 2, 1)), 0, LANES)
        return slotc, starts, _dispatch(starts, slot, h2, cap, n, off)

    slotc, starts, xs = route(n_lat, 0)
    slotc_c, starts_c, xs_c = route(CTX_LEN, n_lat) if with_ctx else (None, None, None)
    y, y_c = _expert_ffn(xs, xs_c, w1, w3, w2, layer)
    X = _combine(starts, slotc, aff, y, X, mod3, g_post2, n_lat, 0, False, in_place)
    if with_ctx:
        X = _combine(starts_c, slotc_c, aff, y_c, X, mod3, g_post2, CTX_LEN, n_lat, True, True)
    return X


def _rope_partner(w):
    lead = w.shape[:-1]
    ws = w.reshape(lead + (2, 2, MLA_ROPE // 4))
    return jnp.stack([-ws[..., 1, :], ws[..., 0, :]], axis=-2).reshape(lead + (MLA_ROPE,))


def _pad_cols(w, left, total):
    return jnp.pad(w, [(0, 0)] * (w.ndim - 1) + [(left, total - left - w.shape[-1])])


def _prep_in_weights(w_in):
    sizes = (MLA_Q_RANK, MLA_KV_RANK, MLA_ROPE, 256, 256, 256, 256, 256, 512, 512, 4 * ML_HEADS, 3 * D_MODEL)
    offs = np.cumsum(sizes)[:-1].tolist()
    cq, ckv, kr, nq, nk, nv, mq, mk, mv, mo, mg, bg = jnp.split(w_in, offs, axis=-1)
    kr1 = _pad_cols(kr, MLA_NOPE, HEAD_PAD)
    kr2 = _pad_cols(_rope_partner(kr), MLA_NOPE, HEAD_PAD)
    mgp = _pad_cols(mg, 0, LANES)
    return jnp.concatenate([cq, ckv, kr1, kr2, nq, nk, nv, mq, mk, mv, mo, mgp, bg], axis=-1).astype(BF16)


def _prep_mla_weights(w_uq, w_ukv):
    nope, rope = w_uq[..., :MLA_NOPE], w_uq[..., MLA_NOPE:]
    wqa = _pad_cols(jnp.concatenate([nope, rope], axis=-1), 0, HEAD_PAD)
    wqb = _pad_cols(_rope_partner(rope), MLA_NOPE, HEAD_PAD)
    wk = _pad_cols(w_ukv[..., :MLA_NOPE], 0, HEAD_PAD)
    wv = _pad_cols(w_ukv[..., MLA_NOPE:], 0, HEAD_PAD)
    flat = lambda w: w.reshape(w.shape[0], MLA_HEADS * HEAD_PAD).astype(BF16)
    return flat(wqa), flat(wqb), flat(wk), flat(wv)


def _rope_tables(n_lat):
    t = np.arange(n_lat)
    pos = np.stack([t // GRID_W, t % GRID_W], axis=-1).astype(np.float64)
    nf = MLA_ROPE // 4
    inv = np.float32(ROPE_BASE) ** (-np.arange(nf, dtype=np.float32) / nf)
    ang = (pos[..., None].astype(np.float32) * inv).astype(np.float64)
    cos = np.repeat(np.cos(ang)[:, :, None, :], 2, axis=2).reshape(n_lat, MLA_ROPE)
    sin = np.repeat(np.sin(ang)[:, :, None, :], 2, axis=2).reshape(n_lat, MLA_ROPE)
    cos = np.concatenate([cos, np.ones((CTX_LEN, MLA_ROPE))], axis=0)
    sin = np.concatenate([sin, np.zeros((CTX_LEN, MLA_ROPE))], axis=0)
    T = n_lat + CTX_LEN
    z = lambda w: np.zeros((T, w))
    scale = (MLA_NOPE + MLA_ROPE) ** -0.5 * np.log2(np.e)
    ta = scale * np.concatenate([np.ones((T, MLA_NOPE)), cos, z(HEAD_PAD - MLA_NOPE - MLA_ROPE)], axis=1)
    tb = scale * np.concatenate([z(MLA_NOPE), sin, z(HEAD_PAD - MLA_NOPE - MLA_ROPE)], axis=1)
    tc = np.concatenate([z(MLA_NOPE), cos, z(HEAD_PAD - MLA_NOPE - MLA_ROPE)], axis=1)
    ts = np.concatenate([z(MLA_NOPE), sin, z(HEAD_PAD - MLA_NOPE - MLA_ROPE)], axis=1)
    return tuple(jnp.asarray(a, F32) for a in (ta, tb, tc, ts))


def kernel(x, c, ctx, c_ctx, w_mod, b_mod, g_pre1, g_post1, g_pre2, g_post2, w_in, g_q, g_kv, w_uq, w_ukv, rpb,
           b_ml_gates, g_ml, w_br_mla, w_br_na, w_br_ml, w_out, w_router, w_e1, w_e3, w_e2):
    B, n_lat, d = x.shape
    depth = w_mod.shape[0]
    rows = n_lat // GRID_W
    assert d == D_MODEL and ctx.shape[1] == CTX_LEN and n_lat % 512 == 0 and rows >= NA_KROWS + NA_QROWS and B % PAIR == 0

    X = jnp.concatenate([x, ctx], axis=1)
    mod_rows = -(-(B + PAIR) // 8) * 8
    c_all = jnp.concatenate([c, jnp.broadcast_to(c_ctx[None, :], (PAIR, d)),
                             jnp.zeros((mod_rows - B - PAIR, d), F32)], axis=0)
    mod = _modulation(c_all, w_mod, b_mod)
    tabs = _rope_tables(n_lat)
    row2 = lambda v: v.astype(F32)[None, :]

    for l in range(depth):
        ctx_out = l < depth - 1
        mod3 = mod[l][:, None, :]
        w_pad = _prep_in_weights(w_in[l])
        wqa, wqb, wk, wv = _prep_mla_weights(w_uq[l], w_ukv[l])
        (Q, K, V, nq, nk, nv, mq, mk, mv, mo, mg, bg) = _inproj(
            X, mod3, row2(g_pre1[l]), w_pad, row2(g_q[l]), row2(g_kv[l]), wqa, wqb, wk, wv, tabs, n_lat)
        a = jnp.concatenate([_mla_attention(Q, K, V, n_lat, False), _mla_attention(Q, K, V, n_lat, True)], axis=1)
        bn = _neighbourhood_attention(nq, nk, nv, _na_bias_table(rpb[l], rows), n_lat, True)
        mgt = jnp.transpose(mg[:, :, :4 * ML_HEADS], (0, 2, 1))
        hf, hb = _mlstm(mq, mk, mv, mg, mgt, b_ml_gates[l], n_lat)
        wr = _pad_cols(w_router[l].astype(F32), 0, LANES)
        wrh = wr.astype(BF16)
        wrl = (wr - wrh.astype(F32)).astype(BF16)
        X, h2, aff = _merge(a, bn, hf, hb, mo, bg, X, mod3, row2(g_ml[l]), row2(g_post1[l]), row2(g_pre2[l]),
                            w_br_mla[l].astype(BF16), w_br_na[l].astype(BF16), w_br_ml[l].astype(BF16),
                            w_out[l].astype(BF16), wrh, wrl, n_lat, True)
        X = _moe(X, h2, aff, mod3, row2(g_post2[l]), w_e1, w_e3, w_e2, l, n_lat, ctx_out, in_place=ctx_out)
    return X
```

```python
import functools

import numpy as np
import jax
import jax.numpy as jnp
from jax import lax
from jax.experimental import pallas as pl
from jax.experimental.pallas import tpu as pltpu

F32 = jnp.float32
BF16 = jnp.bfloat16

D_MODEL = 1024
CTX_LEN = 256
GRID_W = 64
EPS = 1e-6

MLA_HEADS = 4
MLA_Q_RANK = 256
MLA_KV_RANK = 128
MLA_NOPE = 64
MLA_ROPE = 32
MLA_V = 64
ROPE_BASE = 10000.0

NA_HEADS = 4
NA_DIM = 64
NA_KR = 8
NA_KC = 16
NA_QROWS = 4
NA_KROWS = 12

ML_HEADS = 4
ML_QK = 64
ML_V = 128
ML_CHUNK = 128

N_EXPERTS = 16
D_EXPERT = 1024
EC_FACTOR = 2

LANES = 128
ROW_TILE = 256
PAIR = 2
TOK_TILE = 256
SLOT_WIN = 64
BF16_ROWS = 16
HEAD_PAD = 128
NEG_BIG = -1e30

_IN_GROUPS = (("cq", 256), ("ckv", 128), ("kr1", 128), ("kr2", 128), ("nq", 256), ("nk", 256), ("nv", 256),
              ("mq", 256), ("mk", 256), ("mv", 512), ("mo", 512), ("mg", 128), ("bg", 3 * D_MODEL))
_OFF = {}
_o = 0
for _n, _w in _IN_GROUPS:
    _OFF[_n] = (_o, _o + _w)
    _o += _w
D_IN_PAD = _o


def _cparams(sem, vmem_mb):
    return pltpu.CompilerParams(dimension_semantics=sem, vmem_limit_bytes=vmem_mb * 1024 * 1024)


def _dot(a, b):
    return jnp.dot(a, b, preferred_element_type=F32)


def _dot_nt(a, b):
    return lax.dot_general(a, b, (((1,), (1,)), ((), ())), preferred_element_type=F32)


def _rms(x, g):
    return x * lax.rsqrt(jnp.mean(x * x, axis=-1, keepdims=True) + EPS) * g


def _split3(x):
    hi = x.astype(BF16)
    r1 = x - hi.astype(F32)
    mid = r1.astype(BF16)
    lo = (r1 - mid.astype(F32)).astype(BF16)
    return hi, mid, lo


def _log_sigmoid(x):
    return jnp.minimum(x, 0.0) - jnp.log(1.0 + jnp.exp(-jnp.abs(x)))


def _mod_kernel(c_ref, w_ref, b_ref, o_ref):
    cs = c_ref[...]
    cs = cs * jax.nn.sigmoid(cs)
    o_ref[0] = _dot(cs.astype(BF16), w_ref[0].astype(BF16)) + b_ref[0]


def _modulation(c_all, w_mod, b_mod):
    depth, d, n6 = w_mod.shape
    rows = c_all.shape[0]
    tn = 512
    return pl.pallas_call(
        _mod_kernel,
        grid=(depth, n6 // tn),
        in_specs=[pl.BlockSpec((rows, d), lambda l, j: (0, 0)),
                  pl.BlockSpec((1, d, tn), lambda l, j: (l, 0, j)),
                  pl.BlockSpec((1, 1, tn), lambda l, j: (l, 0, j))],
        out_specs=pl.BlockSpec((1, rows, tn), lambda l, j: (l, 0, j)),
        out_shape=jax.ShapeDtypeStruct((depth, rows, n6), F32),
        compiler_params=_cparams(("parallel", "parallel"), 32),
        name="modulation",
    )(c_all, w_mod, b_mod.reshape(depth, 1, n6))


def _inproj_kernel(x_ref, xc_ref, sh_ref, sc_ref, g1_ref, w_ref, gq_ref, gkv_ref, wqa_ref, wqb_ref, wk_ref, wv_ref,
                   ta_ref, tb_ref, tc_ref, ts_ref,
                   q_ref, k_ref, v_ref, nq_ref, nk_ref, nv_ref, mq_ref, mk_ref, mv_ref, mo_ref, mg_ref, bg_ref,
                   *, ctx_tile):
    nb, tm, d = x_ref.shape
    rows = nb * tm
    split = lambda a: a.reshape(nb, tm, a.shape[-1])
    x = jnp.where(pl.program_id(1) >= ctx_tile, xc_ref[...], x_ref[...]).reshape(rows, d)
    xn = x * lax.rsqrt(jnp.mean(x * x, axis=-1, keepdims=True) + EPS)
    h = split(xn) * (g1_ref[...] * (1.0 + sc_ref[...])) + sh_ref[...]
    hb = h.reshape(rows, d).astype(BF16)

    def proj(name, lo=None, hi=None):
        a, b = _OFF[name]
        if lo is not None:
            a, b = a + lo, a + hi
        return _dot(hb, w_ref[:, a:b])

    def group(first, last):
        names = [n for n, _ in _IN_GROUPS]
        names = names[names.index(first):names.index(last) + 1]
        lo = _OFF[first][0]
        z = _dot(hb, w_ref[:, lo:_OFF[last][1]])
        return {n: z[:, _OFF[n][0] - lo:_OFF[n][1] - lo] for n in names}

    g0 = group("cq", "kr2")
    cq, ckv = g0["cq"], g0["ckv"]
    krr = split(g0["kr1"]) * tc_ref[...][None] + split(g0["kr2"]) * ts_ref[...][None]
    g1 = group("nq", "mk")
    nq_ref[...] = split(g1["nq"]).astype(BF16)
    nk_ref[...] = split(g1["nk"]).astype(BF16)
    nv_ref[...] = split(g1["nv"]).astype(BF16)
    mq_ref[...] = split(g1["mq"]).astype(BF16)
    mk_ref[...] = split(g1["mk"]).astype(BF16)
    cqn = _rms(cq, gq_ref[...]).astype(BF16)
    qa = _dot(cqn, wqa_ref[...])
    qb = _dot(cqn, wqb_ref[...])
    ckvn = _rms(ckv, gkv_ref[...]).astype(BF16)
    kn = _dot(ckvn, wk_ref[...])
    vv = _dot(ckvn, wv_ref[...])
    g2 = group("mv", "mg")
    mv_ref[...] = split(g2["mv"]).astype(BF16)
    mo_ref[...] = split(g2["mo"]).astype(BF16)
    mg_ref[...] = split(g2["mg"])
    bg_ref[...] = split(proj("bg")).astype(BF16)
    ta = ta_ref[...][None]
    tb = tb_ref[...][None]
    one_col = (lax.broadcasted_iota(jnp.int32, (1, 1, HEAD_PAD), 2) == MLA_V).astype(F32)
    for hh in range(MLA_HEADS):
        sl = slice(hh * HEAD_PAD, (hh + 1) * HEAD_PAD)
        q_ref[:, hh] = (split(qa[:, sl]) * ta + split(qb[:, sl]) * tb).astype(BF16)
        k_ref[:, hh] = (split(kn[:, sl]) + krr).astype(BF16)
        v_ref[:, hh] = (split(vv[:, sl]) + one_col).astype(BF16)


def _residual_specs(X, n_lat, nb, tm):
    ctx_tile = n_lat // tm
    x_lat, x_ctx, cblk = (X[0], X[1], 0) if isinstance(X, tuple) else (X, X, ctx_tile)
    d = x_lat.shape[-1]
    lat_spec = pl.BlockSpec((nb, tm, d), lambda b, i: (b, jnp.minimum(i, ctx_tile - 1), 0))
    ctx_spec = pl.BlockSpec((nb, tm, d), lambda b, i: (b, cblk, 0))
    return x_lat, x_ctx, lat_spec, ctx_spec


def _inproj(X, mod3, g_pre1, w_pad, g_q, g_kv, wqa, wqb, wk, wv, tabs, n_lat):
    tm = ROW_TILE
    nb = PAIR
    x_lat, x_ctx, lat_spec, ctx_spec = _residual_specs(X, n_lat, nb, tm)
    B, d = x_lat.shape[0], x_lat.shape[2]
    T = n_lat + CTX_LEN
    nt = T // tm
    ctx_tile = n_lat // tm

    def mrow(b, i):
        return jnp.where(i >= ctx_tile, B // nb, b)

    full = lambda shape: pl.BlockSpec(shape, lambda b, i: (0,) * len(shape))
    tab = pl.BlockSpec((tm, HEAD_PAD), lambda b, i: (i, 0))
    row_out = lambda w: pl.BlockSpec((nb, tm, w), lambda b, i: (b, i, 0))
    head_out = pl.BlockSpec((nb, MLA_HEADS, tm, HEAD_PAD), lambda b, i: (b, 0, i, 0))
    sd = jax.ShapeDtypeStruct
    outs = pl.pallas_call(
        functools.partial(_inproj_kernel, ctx_tile=ctx_tile),
        grid=(B // nb, nt),
        in_specs=[lat_spec, ctx_spec,
                  pl.BlockSpec((nb, 1, d), lambda b, i: (mrow(b, i), 0, 0)),
                  pl.BlockSpec((nb, 1, d), lambda b, i: (mrow(b, i), 0, 1)),
                  full((1, d)), full((d, D_IN_PAD)), full((1, MLA_Q_RANK)), full((1, MLA_KV_RANK)),
                  full((MLA_Q_RANK, MLA_HEADS * HEAD_PAD)), full((MLA_Q_RANK, MLA_HEADS * HEAD_PAD)),
                  full((MLA_KV_RANK, MLA_HEADS * HEAD_PAD)), full((MLA_KV_RANK, MLA_HEADS * HEAD_PAD)),
                  tab, tab, tab, tab],
        out_specs=[head_out, head_out, head_out,
                   row_out(256), row_out(256), row_out(256), row_out(256), row_out(256),
                   row_out(512), row_out(512), row_out(128), row_out(3 * d)],
        out_shape=[sd((B, MLA_HEADS, T, HEAD_PAD), BF16)] * 3
                  + [sd((B, T, 256), BF16)] * 5 + [sd((B, T, 512), BF16)] * 2
                  + [sd((B, T, 128), F32), sd((B, T, 3 * d), BF16)],
        compiler_params=_cparams(("parallel", "parallel"), 56),
        name="inproj",
    )(x_lat, x_ctx, mod3, mod3, g_pre1, w_pad, g_q, g_kv, wqa, wqb, wk, wv, *tabs)
    return outs


def _flash_kernel(q_ref, k_ref, v_ref, o_ref, s0_ref, s1_ref, p0_ref, p1_ref, *, nk, rb):
    tq = q_ref.shape[2]
    s_bufs, p_bufs = (s0_ref, s1_ref), (p0_ref, p1_ref)

    def scores(hh):
        s_bufs[hh % 2][...] = _dot_nt(q_ref[0, hh], k_ref[0, hh, :nk, :])

    scores(0)
    for hh in range(MLA_HEADS):
        if hh + 1 < MLA_HEADS:
            scores(hh + 1)
        s_ref, p_ref = s_bufs[hh % 2], p_bufs[hh % 2]
        for r in range(0, tq, rb):
            s = s_ref[r:r + rb, :]
            m = jnp.max(s, axis=1, keepdims=True)
            p_ref[r:r + rb, :] = jnp.exp2(s - m).astype(BF16)
        acc = _dot(p_ref[...], v_ref[0, hh, :nk, :])
        o = acc[:, :MLA_V] / acc[:, MLA_V:MLA_V + 1]
        o_ref[0, :, hh * MLA_V:(hh + 1) * MLA_V] = o.astype(BF16)


def _mla_attention(Q, K, V, n_lat, context):
    B, H, T, E = Q.shape
    kc = 256
    if not context:
        tq, nk, nq = 512, T, n_lat
        q_spec = pl.BlockSpec((1, H, tq, E), lambda b, i: (b, 0, i, 0))
        kv_spec = pl.BlockSpec((1, H, T, E), lambda b, i: (b, 0, 0, 0))
    else:
        tq, nk, nq = CTX_LEN, CTX_LEN, CTX_LEN
        cblk = n_lat // CTX_LEN
        q_spec = pl.BlockSpec((1, H, tq, E), lambda b, i: (b, 0, cblk, 0))
        kv_spec = pl.BlockSpec((1, H, nk, E), lambda b, i: (b, 0, cblk, 0))
    return pl.pallas_call(
        functools.partial(_flash_kernel, nk=nk, rb=64),
        grid=(B, nq // tq),
        in_specs=[q_spec, kv_spec, kv_spec],
        out_specs=pl.BlockSpec((1, tq, H * MLA_V), lambda b, i: (b, i, 0)),
        out_shape=jax.ShapeDtypeStruct((B, nq, H * MLA_V), BF16),
        scratch_shapes=[pltpu.VMEM((tq, nk), F32), pltpu.VMEM((tq, nk), F32),
                        pltpu.VMEM((tq, nk), BF16), pltpu.VMEM((tq, nk), BF16)],
        compiler_params=_cparams(("parallel", "parallel"), 56),
        name="mla_attention_ctx" if context else "mla_attention",
    )(Q, K, V)


def _na_kernel(q_ref, k_ref, v_ref, bias_ref, o_ref, *, n_lat, rows, with_ctx):
    j = pl.program_id(1)
    nblk = rows // NA_QROWS
    scale = NA_DIM ** -0.5
    assert np.log2(scale) == round(np.log2(scale))
    q = (q_ref[0].astype(F32) * scale).astype(BF16)
    kctx = k_ref[0, n_lat:n_lat + CTX_LEN, :]
    vctx = v_ref[0, n_lat:n_lat + CTX_LEN, :]

    def local_step():
        kb = jnp.clip(j * NA_QROWS - NA_KR // 2, 0, rows - NA_KROWS)
        start = pl.multiple_of(kb * GRID_W, GRID_W)
        kloc = k_ref[0, pl.ds(start, NA_KROWS * GRID_W), :]
        vloc = v_ref[0, pl.ds(start, NA_KROWS * GRID_W), :]
        for hh in range(NA_HEADS):
            sl = slice(hh * NA_DIM, (hh + 1) * NA_DIM)
            qh = q[:, sl]
            s_loc = _dot_nt(qh, kloc[:, sl]) + bias_ref[0, hh]
            s_ctx = _dot_nt(qh, kctx[:, sl])
            m = jnp.maximum(jnp.max(s_loc, axis=1, keepdims=True), jnp.max(s_ctx, axis=1, keepdims=True))
            p_loc = jnp.exp(s_loc - m)
            p_ctx = jnp.exp(s_ctx - m)
            den = jnp.sum(p_loc, axis=1, keepdims=True) + jnp.sum(p_ctx, axis=1, keepdims=True)
            o = _dot(p_loc.astype(BF16), vloc[:, sl]) + _dot(p_ctx.astype(BF16), vctx[:, sl])
            o_ref[0, :, sl] = (o / den).astype(BF16)

    def ctx_step():
        for hh in range(NA_HEADS):
            sl = slice(hh * NA_DIM, (hh + 1) * NA_DIM)
            s = _dot_nt(q[:, sl], kctx[:, sl])
            p = jnp.exp(s - jnp.max(s, axis=1, keepdims=True))
            den = jnp.sum(p, axis=1, keepdims=True)
            o_ref[0, :, sl] = (_dot(p.astype(BF16), vctx[:, sl]) / den).astype(BF16)

    if with_ctx:
        pl.when(j < nblk)(local_step)
        pl.when(j == nblk)(ctx_step)
    else:
        local_step()


def _na_bias_table(rpb, rows):
    W = GRID_W
    nr, ncol = 2 * NA_KR - 1, 2 * NA_KC - 1
    w = np.arange(W)[:, None]
    c = np.arange(W)[None, :]
    cs = np.clip(w - NA_KC // 2, 0, W - NA_KC)
    col_ok = (c >= cs) & (c < cs + NA_KC)
    sel_c = (col_ok[:, :, None] & ((c - w + NA_KC - 1)[:, :, None] == np.arange(ncol))).astype(np.float32)
    sel_r = np.zeros((3, NA_QROWS, NA_KROWS, nr), np.float32)
    for p, r0 in enumerate((0, NA_QROWS, rows - NA_QROWS)):
        kb = int(np.clip(r0 - NA_KR // 2, 0, rows - NA_KROWS))
        for q in range(NA_QROWS):
            rs = int(np.clip(r0 + q - NA_KR // 2, 0, rows - NA_KR))
            for k in range(NA_KROWS):
                if rs <= kb + k < rs + NA_KR:
                    sel_r[p, q, k, kb + k - (r0 + q) + NA_KR - 1] = 1.0
    hi = lax.Precision.HIGHEST
    by_col = jnp.einsum("hrd,wcd->hrwc", rpb.astype(F32), jnp.asarray(sel_c), precision=hi)
    bias = jnp.einsum("pqkr,hrwc->phqwkc", jnp.asarray(sel_r), by_col, precision=hi)
    valid = np.einsum("pqk,wc->pqwkc", sel_r.sum(-1), col_ok.astype(np.float32)) > 0
    bias = jnp.where(jnp.asarray(valid)[:, None], bias, NEG_BIG)
    return bias.reshape(3, NA_HEADS, NA_QROWS * W, NA_KROWS * W)


def _neighbourhood_attention(nq, nk, nv, bias_tab, n_lat, with_ctx):
    B, T, w = nq.shape
    rows = n_lat // GRID_W
    nblk = rows // NA_QROWS
    tq = NA_QROWS * GRID_W
    steps = nblk + (1 if with_ctx else 0)

    def pat(j):
        return jnp.where(j == 0, 0, jnp.where(j >= nblk - 1, 2, 1))

    return pl.pallas_call(
        functools.partial(_na_kernel, n_lat=n_lat, rows=rows, with_ctx=with_ctx),
        grid=(B, steps),
        in_specs=[pl.BlockSpec((1, tq, w), lambda b, j: (b, j, 0)),
                  pl.BlockSpec((1, T, w), lambda b, j: (b, 0, 0)),
                  pl.BlockSpec((1, T, w), lambda b, j: (b, 0, 0)),
                  pl.BlockSpec((1, NA_HEADS, tq, NA_KROWS * GRID_W), lambda b, j: (pat(j), 0, 0, 0))],
        out_specs=pl.BlockSpec((1, tq, w), lambda b, j: (b, j, 0)),
        out_shape=jax.ShapeDtypeStruct((B, T, w), BF16),
        compiler_params=_cparams(("parallel", "arbitrary"), 48),
        name="neighbourhood_attention",
    )(nq, nk, nv, bias_tab)


def _mlstm_kernel(qf_ref, kf_ref, vf_ref, gf_ref, gtf_ref, qb_ref, kb_ref, vb_ref, gb_ref, gtb_ref,
                  bcol_ref, brow_ref, hf_ref, hb_ref, st_ref, m_ref):
    L = ML_CHUNK

    @pl.when(pl.program_id(1) == 0)
    def _():
        st_ref[...] = jnp.zeros_like(st_ref)
        m_ref[...] = jnp.zeros_like(m_ref)

    r_i = lax.broadcasted_iota(jnp.int32, (L, L), 0)
    c_i = lax.broadcasted_iota(jnp.int32, (L, L), 1)
    ones_t = jnp.ones((ML_V, L), BF16)
    dirs = ((qf_ref, kf_ref, vf_ref, gf_ref, gtf_ref, hf_ref), (qb_ref, kb_ref, vb_ref, gb_ref, gtb_ref, hb_ref))
    nb = qf_ref.shape[0]
    ng = 4 * ML_HEADS
    for bi, d in [(bi, d) for d in range(2) for bi in range(nb)]:
        q_ref, k_ref, v_ref, g_ref, gt_ref, h_ref = dirs[d]
        if bi == 0:
            causal_t = (r_i <= c_i) if d == 0 else (r_i >= c_i)
            m_row = jnp.where(causal_t, 1.0, 0.0).astype(BF16)
            m_col = jnp.where((c_i <= r_i) if d == 0 else (c_i >= r_i), 1.0, 0.0).astype(BF16)
            gt_all = jnp.concatenate([gt_ref[i] + bcol_ref[...] for i in range(nb)], axis=0)
            lf_r = _log_sigmoid(gt_all)
            br_all = sum(_dot(p, m_row) for p in _split3(lf_r))
            tot_all = jnp.sum(lf_r, axis=1, keepdims=True)
            gc_all = jnp.concatenate([g_ref[i] + brow_ref[...] for i in range(nb)], axis=1)
            bc_all = sum(_dot(m_col, p) for p in _split3(_log_sigmoid(gc_all)))
        gt = gt_all[bi * ng:(bi + 1) * ng, :]
        br = br_all[bi * ng:(bi + 1) * ng, :]
        tot = tot_all[bi * ng:(bi + 1) * ng, :]
        gc = gc_all[:, bi * LANES:(bi + 1) * LANES]
        bc = bc_all[:, bi * LANES:(bi + 1) * LANES]
        k16 = (k_ref[bi].astype(F32) * (ML_QK ** -0.5)).astype(BF16)
        q16 = q_ref[bi]
        q_t = q16.astype(F32).T
        v_t = v_ref[bi].astype(F32).T.astype(BF16)
        for hh in range(ML_HEADS):
            c = (bi * 2 + d) * ML_HEADS + hh
            ji = (2 * d) * ML_HEADS + hh
            jf = (2 * d + 1) * ML_HEADS + hh
            i_row = gt[ji:ji + 1, :]
            b_row = br[jf:jf + 1, :]
            b_last = tot[jf:jf + 1, :]
            src_col = gc[:, ji:ji + 1] - bc[:, jf:jf + 1]
            m_old = m_ref[c:c + 1, 0:1]
            st_old = st_ref[c]
            qh = q16[:, hh * ML_QK:(hh + 1) * ML_QK]
            kh = k16[:, hh * ML_QK:(hh + 1) * ML_QK]
            qh_t = q_t[hh * ML_QK:(hh + 1) * ML_QK, :]
            va_t = jnp.concatenate([v_t[hh * ML_V:(hh + 1) * ML_V, :], ones_t], axis=0)
            d_log = jnp.where(causal_t, src_col + b_row, -jnp.inf)
            inter = b_row + m_old
            m_t = jnp.maximum(inter, jnp.max(d_log, axis=0, keepdims=True))
            a = jnp.exp(inter - m_t)
            smat_t = (_dot_nt(kh, qh) * jnp.exp(d_log - m_t)).astype(BF16)
            lhs = jnp.concatenate([va_t, st_old.astype(BF16)], axis=1)
            rhs = jnp.concatenate([smat_t, (qh_t * a).astype(BF16)], axis=0)
            out_t = _dot(lhs, rhs)
            den = jnp.maximum(jnp.abs(out_t[ML_V:, :]), jnp.exp(-m_t))
            h_ref[bi, :, hh * ML_V:(hh + 1) * ML_V] = (out_t[:ML_V, :] / den).T
            w_log = b_last - b_row + i_row
            m_new = jnp.maximum(b_last + m_old, jnp.max(w_log, axis=1, keepdims=True))
            decay = jnp.exp(b_last + m_old - m_new)
            vw_t = (va_t.astype(F32) * jnp.exp(w_log - m_new)).astype(BF16)
            st_ref[c] = decay * st_old + _dot(vw_t, kh)
            m_ref[c:c + 1, :] = jnp.broadcast_to(m_new, (1, LANES))


def _mlstm(mq, mk, mv, mg, mgt, b_ml, n_lat):
    B, T, _ = mq.shape
    L = ML_CHUNK
    nl = n_lat // L
    nc = CTX_LEN // L
    steps = nl + nc

    def cf(s):
        return jnp.where(s < nc, nl + s, s - nc)

    def cb(s):
        return steps - 1 - s

    nb = PAIR

    def specs(cm):
        return [pl.BlockSpec((nb, L, ML_HEADS * ML_QK), lambda b, s: (b, cm(s), 0)),
                pl.BlockSpec((nb, L, ML_HEADS * ML_QK), lambda b, s: (b, cm(s), 0)),
                pl.BlockSpec((nb, L, ML_HEADS * ML_V), lambda b, s: (b, cm(s), 0)),
                pl.BlockSpec((nb, L, LANES), lambda b, s: (b, cm(s), 0)),
                pl.BlockSpec((nb, 4 * ML_HEADS, L), lambda b, s: (b, 0, cm(s)))]

    b_row = jnp.concatenate([b_ml.astype(F32), jnp.zeros((LANES - 4 * ML_HEADS,), F32)])[None, :]
    b_col = b_ml.astype(F32)[:, None]
    sd = jax.ShapeDtypeStruct
    n_chain = nb * 2 * ML_HEADS
    return pl.pallas_call(
        _mlstm_kernel,
        grid=(B // nb, steps),
        in_specs=specs(cf) + specs(cb) + [pl.BlockSpec((4 * ML_HEADS, 1), lambda b, s: (0, 0)),
                                          pl.BlockSpec((1, LANES), lambda b, s: (0, 0))],
        out_specs=[pl.BlockSpec((nb, L, ML_HEADS * ML_V), lambda b, s: (b, cf(s), 0)),
                   pl.BlockSpec((nb, L, ML_HEADS * ML_V), lambda b, s: (b, cb(s), 0))],
        out_shape=[sd((B, T, ML_HEADS * ML_V), F32)] * 2,
        scratch_shapes=[pltpu.VMEM((n_chain, 2 * ML_V, ML_QK), F32), pltpu.VMEM((n_chain, LANES), F32)],
        compiler_params=_cparams(("parallel", "arbitrary"), 32),
        name="mlstm",
    )(mq, mk, mv, mg, mgt, mq, mk, mv, mg, mgt, b_col, b_row)


def _merge_kernel(a_ref, bn_ref, hf_ref, hb_ref, mo_ref, bg_ref, x_ref, xc_ref, gt1_ref, sh2_ref, sc2_ref,
                  gml_ref, gpost_ref, gpre2_ref, wa_ref, wb_ref, wm_ref, wo_ref, wrh_ref, wrl_ref,
                  xo_ref, h2_ref, aff_ref, *, ctx_tile):
    nb, tm, d = x_ref.shape
    gml = gml_ref[...]
    is_ctx = pl.program_id(1) >= ctx_tile
    for bi in range(nb):
        hm = hf_ref[bi] + hb_ref[bi]
        parts = []
        for hh in range(ML_HEADS):
            sl = slice(hh * ML_V, (hh + 1) * ML_V)
            parts.append(_rms(hm[:, sl], gml[:, sl]))
        mlo = (jnp.concatenate(parts, axis=1) * jax.nn.sigmoid(mo_ref[bi].astype(F32))).astype(BF16)
        ya = _dot(a_ref[bi], wa_ref[...])
        yb = _dot(bn_ref[bi], wb_ref[...])
        ym = _dot(mlo, wm_ref[...])
        y = (jax.nn.sigmoid(bg_ref[bi, :, 0:d].astype(F32)) * ya
             + jax.nn.sigmoid(bg_ref[bi, :, d:2 * d].astype(F32)) * yb
             + jax.nn.sigmoid(bg_ref[bi, :, 2 * d:3 * d].astype(F32)) * ym)
        out = _dot(y.astype(BF16), wo_ref[...])
        x_in = jnp.where(is_ctx, xc_ref[bi], x_ref[bi])
        xn = x_in + _rms(out, gpost_ref[...] * gt1_ref[bi])
        xo_ref[bi] = xn
        h2 = _rms(xn, gpre2_ref[...] * (1.0 + sc2_ref[bi])) + sh2_ref[bi]
        h2_ref[bi] = h2.astype(BF16)
        hi = h2.astype(BF16)
        lo = (h2 - hi.astype(F32)).astype(BF16)
        logits = _dot(hi, wrh_ref[...]) + _dot(lo, wrh_ref[...]) + _dot(hi, wrl_ref[...])
        lane = lax.broadcasted_iota(jnp.int32, logits.shape, 1)
        logits = jnp.where(lane < N_EXPERTS, logits, -jnp.inf)
        e = jnp.exp(logits - jnp.max(logits, axis=1, keepdims=True))
        aff_ref[bi] = e / jnp.sum(e, axis=1, keepdims=True)


def _merge(a, bn, hf, hb, mo, bg, X, mod3, g_ml, g_post1, g_pre2, wa, wb, wm, wo, wrh, wrl, n_lat, with_ctx):
    tm = ROW_TILE
    nb = PAIR
    x_lat, x_ctx, lat_spec, ctx_spec = _residual_specs(X, n_lat, nb, tm)
    B, d = x_lat.shape[0], x_lat.shape[2]
    T = n_lat + CTX_LEN
    ctx_tile = n_lat // tm
    nt = ctx_tile + (1 if with_ctx else 0)

    def mrow(b, i):
        return jnp.where(i >= ctx_tile, B // nb, b)

    full = lambda shape: pl.BlockSpec(shape, lambda b, i: (0,) * len(shape))
    row = lambda w: pl.BlockSpec((nb, tm, w), lambda b, i: (b, i, 0))
    modc = lambda ch: pl.BlockSpec((nb, 1, d), lambda b, i: (mrow(b, i), 0, ch))
    sd = jax.ShapeDtypeStruct
    return pl.pallas_call(
        functools.partial(_merge_kernel, ctx_tile=ctx_tile),
        grid=(B // nb, nt),
        in_specs=[row(256), row(256), row(512), row(512), row(512), row(3 * d), lat_spec, ctx_spec,
                  modc(2), modc(3), modc(4),
                  full((1, ML_HEADS * ML_V)), full((1, d)), full((1, d)),
                  full((256, d)), full((256, d)), full((512, d)), full((d, d)), full((d, LANES)), full((d, LANES))],
        out_specs=[row(d), row(d), row(LANES)],
        out_shape=[sd((B, T, d), F32), sd((B, T, d), BF16), sd((B, T, LANES), F32)],
        compiler_params=_cparams(("parallel", "parallel"), 48),
        name="merge",
    )(a, bn, hf, hb, mo, bg, x_lat, x_ctx, mod3, mod3, mod3, g_ml, g_post1, g_pre2, wa, wb, wm, wo, wrh, wrl)


def _select_kernel(aff_ref, slot_ref, start_ref, *, cap):
    aff = aff_ref[0]
    n = aff.shape[1]
    bits = pltpu.bitcast(aff, jnp.int32)
    capf = jnp.float32(cap)

    def count(mask):
        return jnp.sum(jnp.where(mask, 1.0, 0.0), axis=1, keepdims=True)

    def refine(it, thr):
        cand = thr | jnp.left_shift(jnp.int32(1), 30 - it)
        return jnp.where(count(bits >= cand) >= capf, cand, thr)

    thr = lax.fori_loop(0, 31, refine, jnp.zeros((N_EXPERTS, 1), jnp.int32))
    gt = bits > thr
    eq = bits == thr
    need = capf - count(gt)
    r_i = lax.broadcasted_iota(jnp.int32, (LANES, LANES), 0)
    c_i = lax.broadcasted_iota(jnp.int32, (LANES, LANES), 1)
    before = jnp.where(r_i < c_i, 1.0, 0.0).astype(BF16)

    def excl_prefix(mask):
        carry = jnp.zeros((N_EXPERTS, 1), F32)
        outs, carries = [], []
        for c in range(n // LANES):
            carries.append(carry)
            mc = jnp.where(mask[:, c * LANES:(c + 1) * LANES], 1.0, 0.0)
            outs.append(_dot(mc.astype(BF16), before) + carry)
            carry = carry + jnp.sum(mc, axis=1, keepdims=True)
        return jnp.concatenate(outs, axis=1), carries + [carry]

    sel = gt | (eq & (excl_prefix(eq)[0] < need))
    rank, carries = excl_prefix(sel)
    slot_ref[0] = jnp.where(sel, rank, -1.0)
    lane = lax.broadcasted_iota(jnp.int32, (N_EXPERTS, LANES), 1)
    per_tile = TOK_TILE // LANES
    starts = jnp.zeros((N_EXPERTS, LANES), F32)
    for k in range(n // TOK_TILE + 1):
        starts = jnp.where(lane == k, carries[k * per_tile], starts)
    start_ref[0] = starts


def _select(aff_t, cap):
    B, E, n = aff_t.shape
    slot, starts = pl.pallas_call(
        functools.partial(_select_kernel, cap=cap),
        grid=(B,),
        in_specs=[pl.BlockSpec((1, E, n), lambda b: (b, 0, 0))],
        out_specs=[pl.BlockSpec((1, E, n), lambda b: (b, 0, 0)), pl.BlockSpec((1, E, LANES), lambda b: (b, 0, 0))],
        out_shape=[jax.ShapeDtypeStruct((B, E, n), F32), jax.ShapeDtypeStruct((B, E, LANES), F32)],
        compiler_params=_cparams(("parallel",), 32),
        name="route_select",
    )(aff_t)
    return slot, starts[:, :, :n // TOK_TILE + 1].astype(jnp.int32)


def _log2(v):
    assert v & (v - 1) == 0
    return v.bit_length() - 1


def _window_plan(start_ref, b, kt, cap, win):
    firsts, rounds = [], jnp.int32(0)
    for e in range(N_EXPERTS):
        lo = start_ref[b, e, kt]
        hi = start_ref[b, e, kt + 1]
        first = jnp.minimum(lax.shift_right_logical(lo, _log2(BF16_ROWS)) * BF16_ROWS, cap - win)
        firsts.append(first)
        rounds = jnp.maximum(rounds, lax.shift_right_logical(hi - first + (win - 1), _log2(win)))
    return firsts, rounds


def _dispatch_kernel(start_ref, slot_ref, h_ref, xs_ref, *, cap, win):
    b, kt = pl.program_id(0), pl.program_id(1)

    @pl.when(kt == 0)
    def _():
        xs_ref[...] = jnp.zeros_like(xs_ref)

    slot = slot_ref[0]
    tile = slot.shape[1]
    h = h_ref[0]
    firsts, rounds = _window_plan(start_ref, b, kt, cap, win)
    sub = lax.broadcasted_iota(jnp.int32, (win, tile), 0)

    def one_round(r, carry):
        rows, pieces = [], []
        for e in range(N_EXPERTS):
            base = firsts[e] + r * win
            row0 = pl.multiple_of(jnp.minimum(base, cap - win), BF16_ROWS)
            abs_slot = row0 + sub
            hit = (abs_slot.astype(F32) == slot[e:e + 1, :]) & (abs_slot >= base)
            pieces.append(jnp.where(hit, 1.0, 0.0).astype(BF16))
            rows.append(row0)
        res = _dot(jnp.concatenate(pieces, axis=0), h)
        for e in range(N_EXPERTS):
            cur = xs_ref[0, e, pl.ds(rows[e], win), :]
            xs_ref[0, e, pl.ds(rows[e], win), :] = cur + res[e * win:(e + 1) * win].astype(BF16)
        return carry

    lax.fori_loop(0, rounds, one_round, 0)


def _dispatch(starts, slot, h2, cap, n, tok_off):
    B, E, _ = slot.shape
    d = h2.shape[2]
    blk0 = tok_off // TOK_TILE
    grid_spec = pltpu.PrefetchScalarGridSpec(
        num_scalar_prefetch=1,
        grid=(B, n // TOK_TILE),
        in_specs=[pl.BlockSpec((1, E, TOK_TILE), lambda b, k, st: (b, 0, k)),
                  pl.BlockSpec((1, TOK_TILE, d), lambda b, k, st: (b, blk0 + k, 0))],
        out_specs=pl.BlockSpec((1, E, cap, d), lambda b, k, st: (b, 0, 0, 0)))
    return pl.pallas_call(
        functools.partial(_dispatch_kernel, cap=cap, win=min(SLOT_WIN, cap)),
        grid_spec=grid_spec,
        out_shape=jax.ShapeDtypeStruct((B, E, cap, d), BF16),
        compiler_params=_cparams(("parallel", "arbitrary"), 56),
        name="moe_dispatch",
    )(starts, slot, h2)


def _ffn_kernel(*refs, with_ctx):
    if with_ctx:
        xs_ref, xc_ref, w1_ref, w3_ref, w2_ref, y_ref, yc_ref, w1b, w3b, w2b = refs
    else:
        xs_ref, w1_ref, w3_ref, w2_ref, y_ref, w1b, w3b, w2b = refs

    @pl.when(pl.program_id(1) == 0)
    def _():
        step = 256
        for src, dst in ((w1_ref, w1b), (w3_ref, w3b), (w2_ref, w2b)):
            for r in range(0, src.shape[2], step):
                dst[r:r + step, :] = src[0, 0, r:r + step, :].astype(BF16)

    nb, _, cap, _ = xs_ref.shape
    parts = [xs_ref[i, 0] for i in range(nb)]
    if with_ctx:
        parts += [xc_ref[i, 0] for i in range(nb)]
    x = jnp.concatenate(parts, axis=0)
    h1 = _dot(x, w1b[...])
    h3 = _dot(x, w3b[...])
    hid = (h1 * jax.nn.sigmoid(h1) * h3).astype(BF16)
    y = _dot(hid, w2b[...]).astype(BF16)
    for i in range(nb):
        y_ref[i, 0] = y[i * cap:(i + 1) * cap]
    if with_ctx:
        cap_c = xc_ref.shape[2]
        for i in range(nb):
            yc_ref[i, 0] = y[nb * cap + i * cap_c:nb * cap + (i + 1) * cap_c]


def _expert_ffn(xs, xs_ctx, w1, w3, w2, layer):
    B, E, cap, d = xs.shape
    f = w1.shape[3]
    slots = lambda c: pl.BlockSpec((PAIR, 1, c, d), lambda e, b: (b, e, 0, 0))
    x_in, x_specs = [xs], [slots(cap)]
    out_specs, out_shape = [slots(cap)], [jax.ShapeDtypeStruct((B, E, cap, d), BF16)]
    if xs_ctx is not None:
        cap_c = xs_ctx.shape[2]
        x_in.append(xs_ctx)
        x_specs.append(slots(cap_c))
        out_specs.append(slots(cap_c))
        out_shape.append(jax.ShapeDtypeStruct((B, E, cap_c, d), BF16))
    outs = pl.pallas_call(
        functools.partial(_ffn_kernel, with_ctx=xs_ctx is not None),
        grid=(E, B // PAIR),
        in_specs=x_specs + [pl.BlockSpec((1, 1, d, f), lambda e, b: (layer, e, 0, 0)),
                            pl.BlockSpec((1, 1, d, f), lambda e, b: (layer, e, 0, 0)),
                            pl.BlockSpec((1, 1, f, d), lambda e, b: (layer, e, 0, 0))],
        out_specs=out_specs,
        out_shape=out_shape,
        scratch_shapes=[pltpu.VMEM((d, f), BF16), pltpu.VMEM((d, f), BF16), pltpu.VMEM((f, d), BF16)],
        compiler_params=_cparams(("arbitrary", "arbitrary"), 56),
        name="moe_expert_ffn",
    )(*x_in, w1, w3, w2)
    return outs if xs_ctx is not None else (outs[0], None)


def _combine_kernel(start_ref, slotc_ref, aff_ref, y_ref, x_ref, gt2_ref, gpost_ref, xo_ref, acc_ref, *, cap, win):
    b, kt = pl.program_id(0), pl.program_id(1)
    slotc = slotc_ref[0]
    aff = aff_ref[0]
    tile = slotc.shape[0]
    firsts, rounds = _window_plan(start_ref, b, kt, cap, win)
    width = N_EXPERTS * win
    col = lax.broadcasted_iota(jnp.int32, (1, width), 1)
    owner = lax.shift_right_logical(col, _log2(win))
    off = col - owner * win
    spread = jnp.where(lax.broadcasted_iota(jnp.int32, (LANES, width), 0) == owner, 1.0, 0.0).astype(BF16)
    half = jnp.floor(slotc * 0.5)
    g_hi = aff.astype(BF16)
    g_lo = (aff - g_hi.astype(F32)).astype(BF16)
    stacked = jnp.concatenate([half.astype(BF16), (slotc - 2.0 * half).astype(BF16), g_hi, g_lo], axis=0)
    rep = _dot(stacked, spread)
    scol = 2.0 * rep[:tile] + rep[tile:2 * tile]
    ghi_rep = rep[2 * tile:3 * tile]
    glo_rep = rep[3 * tile:]
    acc_ref[...] = jnp.zeros_like(acc_ref)

    def one_round(r, carry):
        row0v = jnp.zeros((1, width), jnp.int32)
        basev = jnp.zeros((1, width), jnp.int32)
        rhs_p = []
        for e in range(N_EXPERTS):
            base = firsts[e] + r * win
            row0 = pl.multiple_of(jnp.minimum(base, cap - win), BF16_ROWS)
            row0v = jnp.where(owner == e, row0, row0v)
            basev = jnp.where(owner == e, base, basev)
            rhs_p.append(y_ref[0, e, pl.ds(row0, win), :])
        abs_slot = row0v + off
        hit = (abs_slot.astype(F32) == scol) & (abs_slot >= basev)
        lhs = jnp.concatenate([jnp.where(hit, ghi_rep, 0.0).astype(BF16),
                               jnp.where(hit, glo_rep, 0.0).astype(BF16)], axis=0)
        res = _dot(lhs, jnp.concatenate(rhs_p, axis=0))
        acc_ref[...] += res[:tile] + res[tile:]
        return carry

    lax.fori_loop(0, rounds, one_round, 0)
    xo_ref[0] = x_ref[0] + _rms(acc_ref[...], gpost_ref[...] * gt2_ref[0])


def _combine(starts, slotc, aff, y, X, mod3, g_post2, n, tok_off, is_ctx, in_place):
    B, T, d = X.shape
    E, cap = y.shape[1], y.shape[2]
    tt = TOK_TILE
    blk0 = tok_off // tt
    mrow = (lambda b: B) if is_ctx else (lambda b: b)
    if in_place:
        out_spec = pl.BlockSpec((1, tt, d), lambda b, t, st: (b, blk0 + t, 0))
        out_shape, aliases = jax.ShapeDtypeStruct((B, T, d), F32), {4: 0}
    else:
        out_spec = pl.BlockSpec((1, tt, d), lambda b, t, st: (b, t, 0))
        out_shape, aliases = jax.ShapeDtypeStruct((B, n, d), F32), {}
    grid_spec = pltpu.PrefetchScalarGridSpec(
        num_scalar_prefetch=1,
        grid=(B, n // tt),
        in_specs=[pl.BlockSpec((1, tt, LANES), lambda b, t, st: (b, t, 0)),
                  pl.BlockSpec((1, tt, LANES), lambda b, t, st: (b, blk0 + t, 0)),
                  pl.BlockSpec((1, E, cap, d), lambda b, t, st: (b, 0, 0, 0)),
                  pl.BlockSpec((1, tt, d), lambda b, t, st: (b, blk0 + t, 0)),
                  pl.BlockSpec((1, 1, d), lambda b, t, st: (mrow(b), 0, 5)),
                  pl.BlockSpec((1, d), lambda b, t, st: (0, 0))],
        out_specs=out_spec,
        scratch_shapes=[pltpu.VMEM((tt, d), F32)])
    return pl.pallas_call(
        functools.partial(_combine_kernel, cap=cap, win=min(SLOT_WIN, cap)),
        grid_spec=grid_spec,
        out_shape=out_shape,
        input_output_aliases=aliases,
        compiler_params=_cparams(("parallel", "arbitrary"), 56),
        name="moe_combine",
    )(starts, slotc, aff, y, X, mod3, g_post2)


def _moe(X, h2, aff, mod3, g_post2, w1, w3, w2, layer, n_lat, with_ctx, in_place):
    def route(n, off):
        cap = EC_FACTOR * n // N_EXPERTS
        aff_t = jnp.transpose(aff[:, off:off + n, :N_EXPERTS], (0, 2, 1))
        slot, starts = _select(aff_t, cap)
        slotc = _pad_cols(jnp.transpose(slot, (0, 2, 1)), 0, LANES)
        return slotc, starts, _dispatch(starts, slot, h2, cap, n, off)

    slotc, starts, xs = route(n_lat, 0)
    slotc_c, starts_c, xs_c = route(CTX_LEN, n_lat) if with_ctx else (None, None, None)
    y, y_c = _expert_ffn(xs, xs_c, w1, w3, w2, layer)
    X = _combine(starts, slotc, aff, y, X, mod3, g_post2, n_lat, 0, False, in_place)
    if with_ctx:
        X = _combine(starts_c, slotc_c, aff, y_c, X, mod3, g_post2, CTX_LEN, n_lat, True, True)
    return X


def _rope_partner(w):
    lead = w.shape[:-1]
    ws = w.reshape(lead + (2, 2, MLA_ROPE // 4))
    return jnp.stack([-ws[..., 1, :], ws[..., 0, :]], axis=-2).reshape(lead + (MLA_ROPE,))


def _pad_cols(w, left, total):
    return jnp.pad(w, [(0, 0)] * (w.ndim - 1) + [(left, total - left - w.shape[-1])])


def _prep_in_weights(w_in):
    sizes = (MLA_Q_RANK, MLA_KV_RANK, MLA_ROPE, 256, 256, 256, 256, 256, 512, 512, 4 * ML_HEADS, 3 * D_MODEL)
    offs = np.cumsum(sizes)[:-1].tolist()
    cq, ckv, kr, nq, nk, nv, mq, mk, mv, mo, mg, bg = jnp.split(w_in, offs, axis=-1)
    kr1 = _pad_cols(kr, MLA_NOPE, HEAD_PAD)
    kr2 = _pad_cols(_rope_partner(kr), MLA_NOPE, HEAD_PAD)
    mgp = _pad_cols(mg, 0, LANES)
    return jnp.concatenate([cq, ckv, kr1, kr2, nq, nk, nv, mq, mk, mv, mo, mgp, bg], axis=-1).astype(BF16)


def _prep_mla_weights(w_uq, w_ukv):
    nope, rope = w_uq[..., :MLA_NOPE], w_uq[..., MLA_NOPE:]
    wqa = _pad_cols(jnp.concatenate([nope, rope], axis=-1), 0, HEAD_PAD)
    wqb = _pad_cols(_rope_partner(rope), MLA_NOPE, HEAD_PAD)
    wk = _pad_cols(w_ukv[..., :MLA_NOPE], 0, HEAD_PAD)
    wv = _pad_cols(w_ukv[..., MLA_NOPE:], 0, HEAD_PAD)
    flat = lambda w: w.reshape(w.shape[0], MLA_HEADS * HEAD_PAD).astype(BF16)
    return flat(wqa), flat(wqb), flat(wk), flat(wv)


def _rope_tables(n_lat):
    t = np.arange(n_lat)
    pos = np.stack([t // GRID_W, t % GRID_W], axis=-1).astype(np.float64)
    nf = MLA_ROPE // 4
    inv = np.float32(ROPE_BASE) ** (-np.arange(nf, dtype=np.float32) / nf)
    ang = (pos[..., None].astype(np.float32) * inv).astype(np.float64)
    cos = np.repeat(np.cos(ang)[:, :, None, :], 2, axis=2).reshape(n_lat, MLA_ROPE)
    sin = np.repeat(np.sin(ang)[:, :, None, :], 2, axis=2).reshape(n_lat, MLA_ROPE)
    cos = np.concatenate([cos, np.ones((CTX_LEN, MLA_ROPE))], axis=0)
    sin = np.concatenate([sin, np.zeros((CTX_LEN, MLA_ROPE))], axis=0)
    T = n_lat + CTX_LEN
    z = lambda w: np.zeros((T, w))
    scale = (MLA_NOPE + MLA_ROPE) ** -0.5 * np.log2(np.e)
    ta = scale * np.concatenate([np.ones((T, MLA_NOPE)), cos, z(HEAD_PAD - MLA_NOPE - MLA_ROPE)], axis=1)
    tb = scale * np.concatenate([z(MLA_NOPE), sin, z(HEAD_PAD - MLA_NOPE - MLA_ROPE)], axis=1)
    tc = np.concatenate([z(MLA_NOPE), cos, z(HEAD_PAD - MLA_NOPE - MLA_ROPE)], axis=1)
    ts = np.concatenate([z(MLA_NOPE), sin, z(HEAD_PAD - MLA_NOPE - MLA_ROPE)], axis=1)
    return tuple(jnp.asarray(a, F32) for a in (ta, tb, tc, ts))


def kernel(x, c, ctx, c_ctx, w_mod, b_mod, g_pre1, g_post1, g_pre2, g_post2, w_in, g_q, g_kv, w_uq, w_ukv, rpb,
           b_ml_gates, g_ml, w_br_mla, w_br_na, w_br_ml, w_out, w_router, w_e1, w_e3, w_e2):
    B, n_lat, d = x.shape
    depth = w_mod.shape[0]
    rows = n_lat // GRID_W
    assert d == D_MODEL and ctx.shape[1] == CTX_LEN and n_lat % 512 == 0 and rows >= NA_KROWS + NA_QROWS and B % PAIR == 0

    X = (x, ctx)
    mod_rows = -(-(B + PAIR) // 8) * 8
    c_all = jnp.concatenate([c, jnp.broadcast_to(c_ctx[None, :], (PAIR, d)),
                             jnp.zeros((mod_rows - B - PAIR, d), F32)], axis=0)
    mod = _modulation(c_all, w_mod, b_mod)
    tabs = _rope_tables(n_lat)
    row2 = lambda v: v.astype(F32)[None, :]

    for l in range(depth):
        ctx_out = l < depth - 1
        mod3 = mod[l][:, None, :]
        w_pad = _prep_in_weights(w_in[l])
        wqa, wqb, wk, wv = _prep_mla_weights(w_uq[l], w_ukv[l])
        (Q, K, V, nq, nk, nv, mq, mk, mv, mo, mg, bg) = _inproj(
            X, mod3, row2(g_pre1[l]), w_pad, row2(g_q[l]), row2(g_kv[l]), wqa, wqb, wk, wv, tabs, n_lat)
        a = jnp.concatenate([_mla_attention(Q, K, V, n_lat, False), _mla_attention(Q, K, V, n_lat, True)], axis=1)
        bn = _neighbourhood_attention(nq, nk, nv, _na_bias_table(rpb[l], rows), n_lat, True)
        mgt = jnp.transpose(mg[:, :, :4 * ML_HEADS], (0, 2, 1))
        hf, hb = _mlstm(mq, mk, mv, mg, mgt, b_ml_gates[l], n_lat)
        wr = _pad_cols(w_router[l].astype(F32), 0, LANES)
        wrh = wr.astype(BF16)
        wrl = (wr - wrh.astype(F32)).astype(BF16)
        X, h2, aff = _merge(a, bn, hf, hb, mo, bg, X, mod3, row2(g_ml[l]), row2(g_post1[l]), row2(g_pre2[l]),
                            w_br_mla[l].astype(BF16), w_br_na[l].astype(BF16), w_br_ml[l].astype(BF16),
                            w_out[l].astype(BF16), wrh, wrl, n_lat, True)
        X = _moe(X, h2, aff, mod3, row2(g_post2[l]), w_e1, w_e3, w_e2, l, n_lat, ctx_out, in_place=ctx_out)
    return X
```

```python
import functools

import numpy as np
import jax
import jax.numpy as jnp
from jax import lax
from jax.experimental import pallas as pl
from jax.experimental.pallas import tpu as pltpu

F32 = jnp.float32
BF16 = jnp.bfloat16

D_MODEL = 1024
CTX_LEN = 256
GRID_W = 64
EPS = 1e-6

MLA_HEADS = 4
MLA_Q_RANK = 256
MLA_KV_RANK = 128
MLA_NOPE = 64
MLA_ROPE = 32
MLA_V = 64
ROPE_BASE = 10000.0

NA_HEADS = 4
NA_DIM = 64
NA_KR = 8
NA_KC = 16
NA_QROWS = 4
NA_KROWS = 12

ML_HEADS = 4
ML_QK = 64
ML_V = 128
ML_CHUNK = 128

N_EXPERTS = 16
D_EXPERT = 1024
EC_FACTOR = 2

LANES = 128
ROW_TILE = 256
PAIR = 2
TOK_TILE = 256
SLOT_WIN = 64
BF16_ROWS = 16
HEAD_PAD = 128
NEG_BIG = -1e30

_IN_GROUPS = (("cq", 256), ("ckv", 128), ("kr1", 128), ("kr2", 128), ("nq", 256), ("nk", 256), ("nv", 256),
              ("mq", 256), ("mk", 256), ("mv", 512), ("mo", 512), ("mg", 128), ("bg", 3 * D_MODEL))
_OFF = {}
_o = 0
for _n, _w in _IN_GROUPS:
    _OFF[_n] = (_o, _o + _w)
    _o += _w
D_IN_PAD = _o


def _cparams(sem, vmem_mb):
    return pltpu.CompilerParams(dimension_semantics=sem, vmem_limit_bytes=vmem_mb * 1024 * 1024)


def _dot(a, b):
    return jnp.dot(a, b, preferred_element_type=F32)


def _dot_nt(a, b):
    return lax.dot_general(a, b, (((1,), (1,)), ((), ())), preferred_element_type=F32)


def _rms(x, g):
    return x * lax.rsqrt(jnp.mean(x * x, axis=-1, keepdims=True) + EPS) * g


def _split3(x):
    hi = x.astype(BF16)
    r1 = x - hi.astype(F32)
    mid = r1.astype(BF16)
    lo = (r1 - mid.astype(F32)).astype(BF16)
    return hi, mid, lo


def _log_sigmoid(x):
    return jnp.minimum(x, 0.0) - jnp.log(1.0 + jnp.exp(-jnp.abs(x)))


def _mod_kernel(c_ref, w_ref, b_ref, o_ref):
    cs = c_ref[...]
    cs = cs * jax.nn.sigmoid(cs)
    o_ref[0] = _dot(cs.astype(BF16), w_ref[0].astype(BF16)) + b_ref[0]


def _modulation(c_all, w_mod, b_mod):
    depth, d, n6 = w_mod.shape
    rows = c_all.shape[0]
    tn = 512
    return pl.pallas_call(
        _mod_kernel,
        grid=(depth, n6 // tn),
        in_specs=[pl.BlockSpec((rows, d), lambda l, j: (0, 0)),
                  pl.BlockSpec((1, d, tn), lambda l, j: (l, 0, j)),
                  pl.BlockSpec((1, 1, tn), lambda l, j: (l, 0, j))],
        out_specs=pl.BlockSpec((1, rows, tn), lambda l, j: (l, 0, j)),
        out_shape=jax.ShapeDtypeStruct((depth, rows, n6), F32),
        compiler_params=_cparams(("parallel", "parallel"), 32),
        name="modulation",
    )(c_all, w_mod, b_mod.reshape(depth, 1, n6))


def _inproj_kernel(x_ref, xc_ref, sh_ref, sc_ref, g1_ref, w_ref, gq_ref, gkv_ref, wqa_ref, wqb_ref, wk_ref, wv_ref,
                   ta_ref, tb_ref, tc_ref, ts_ref,
                   q_ref, k_ref, v_ref, nq_ref, nk_ref, nv_ref, mq_ref, mk_ref, mv_ref, mo_ref, mg_ref, bg_ref,
                   *, ctx_tile):
    nb, tm, d = x_ref.shape
    rows = nb * tm
    split = lambda a: a.reshape(nb, tm, a.shape[-1])
    x = jnp.where(pl.program_id(1) >= ctx_tile, xc_ref[...], x_ref[...]).reshape(rows, d)
    xn = x * lax.rsqrt(jnp.mean(x * x, axis=-1, keepdims=True) + EPS)
    h = split(xn) * (g1_ref[...] * (1.0 + sc_ref[...])) + sh_ref[...]
    hb = h.reshape(rows, d).astype(BF16)

    def proj(name, lo=None, hi=None):
        a, b = _OFF[name]
        if lo is not None:
            a, b = a + lo, a + hi
        return _dot(hb, w_ref[:, a:b])

    def group(first, last):
        names = [n for n, _ in _IN_GROUPS]
        names = names[names.index(first):names.index(last) + 1]
        lo = _OFF[first][0]
        z = _dot(hb, w_ref[:, lo:_OFF[last][1]])
        return {n: z[:, _OFF[n][0] - lo:_OFF[n][1] - lo] for n in names}

    g0 = group("cq", "kr2")
    cq, ckv = g0["cq"], g0["ckv"]
    krr = split(g0["kr1"]) * tc_ref[...][None] + split(g0["kr2"]) * ts_ref[...][None]
    g1 = group("nq", "mk")
    nq_ref[...] = split(g1["nq"]).astype(BF16)
    nk_ref[...] = split(g1["nk"]).astype(BF16)
    nv_ref[...] = split(g1["nv"]).astype(BF16)
    mq_ref[...] = split(g1["mq"]).astype(BF16)
    mk_ref[...] = split(g1["mk"]).astype(BF16)
    cqn = _rms(cq, gq_ref[...]).astype(BF16)
    qa = _dot(cqn, wqa_ref[...])
    qb = _dot(cqn, wqb_ref[...])
    ckvn = _rms(ckv, gkv_ref[...]).astype(BF16)
    kn = _dot(ckvn, wk_ref[...])
    vv = _dot(ckvn, wv_ref[...])
    g2 = group("mv", "mg")
    mv_ref[...] = split(g2["mv"]).astype(BF16)
    mo_ref[...] = split(g2["mo"]).astype(BF16)
    mg_ref[...] = split(g2["mg"])
    bg_ref[...] = split(proj("bg")).astype(BF16)
    ta = ta_ref[...][None]
    tb = tb_ref[...][None]
    one_col = (lax.broadcasted_iota(jnp.int32, (1, 1, HEAD_PAD), 2) == MLA_V).astype(F32)
    for hh in range(MLA_HEADS):
        sl = slice(hh * HEAD_PAD, (hh + 1) * HEAD_PAD)
        q_ref[:, hh] = (split(qa[:, sl]) * ta + split(qb[:, sl]) * tb).astype(BF16)
        k_ref[:, hh] = (split(kn[:, sl]) + krr).astype(BF16)
        v_ref[:, hh] = (split(vv[:, sl]) + one_col).astype(BF16)


def _residual_specs(X, n_lat, nb, tm):
    ctx_tile = n_lat // tm
    x_lat, x_ctx, cblk = (X[0], X[1], 0) if isinstance(X, tuple) else (X, X, ctx_tile)
    d = x_lat.shape[-1]
    lat_spec = pl.BlockSpec((nb, tm, d), lambda b, i: (b, jnp.minimum(i, ctx_tile - 1), 0))
    ctx_spec = pl.BlockSpec((nb, tm, d), lambda b, i: (b, cblk, 0))
    return x_lat, x_ctx, lat_spec, ctx_spec


def _inproj(X, mod3, g_pre1, w_pad, g_q, g_kv, wqa, wqb, wk, wv, tabs, n_lat):
    tm = ROW_TILE
    nb = PAIR
    x_lat, x_ctx, lat_spec, ctx_spec = _residual_specs(X, n_lat, nb, tm)
    B, d = x_lat.shape[0], x_lat.shape[2]
    T = n_lat + CTX_LEN
    nt = T // tm
    ctx_tile = n_lat // tm

    def mrow(b, i):
        return jnp.where(i >= ctx_tile, B // nb, b)

    full = lambda shape: pl.BlockSpec(shape, lambda b, i: (0,) * len(shape))
    tab = pl.BlockSpec((tm, HEAD_PAD), lambda b, i: (i, 0))
    row_out = lambda w: pl.BlockSpec((nb, tm, w), lambda b, i: (b, i, 0))
    head_out = pl.BlockSpec((nb, MLA_HEADS, tm, HEAD_PAD), lambda b, i: (b, 0, i, 0))
    sd = jax.ShapeDtypeStruct
    outs = pl.pallas_call(
        functools.partial(_inproj_kernel, ctx_tile=ctx_tile),
        grid=(B // nb, nt),
        in_specs=[lat_spec, ctx_spec,
                  pl.BlockSpec((nb, 1, d), lambda b, i: (mrow(b, i), 0, 0)),
                  pl.BlockSpec((nb, 1, d), lambda b, i: (mrow(b, i), 0, 1)),
                  full((1, d)), full((d, D_IN_PAD)), full((1, MLA_Q_RANK)), full((1, MLA_KV_RANK)),
                  full((MLA_Q_RANK, MLA_HEADS * HEAD_PAD)), full((MLA_Q_RANK, MLA_HEADS * HEAD_PAD)),
                  full((MLA_KV_RANK, MLA_HEADS * HEAD_PAD)), full((MLA_KV_RANK, MLA_HEADS * HEAD_PAD)),
                  tab, tab, tab, tab],
        out_specs=[head_out, head_out, head_out,
                   row_out(256), row_out(256), row_out(256), row_out(256), row_out(256),
                   row_out(512), row_out(512), row_out(128), row_out(3 * d)],
        out_shape=[sd((B, MLA_HEADS, T, HEAD_PAD), BF16)] * 3
                  + [sd((B, T, 256), BF16)] * 5 + [sd((B, T, 512), BF16)] * 2
                  + [sd((B, T, 128), F32), sd((B, T, 3 * d), BF16)],
        compiler_params=_cparams(("parallel", "parallel"), 56),
        name="inproj",
    )(x_lat, x_ctx, mod3, mod3, g_pre1, w_pad, g_q, g_kv, wqa, wqb, wk, wv, *tabs)
    return outs


def _flash_kernel(q_ref, k_ref, v_ref, o_ref, s0_ref, s1_ref, p0_ref, p1_ref, *, nk, rb):
    tq = q_ref.shape[2]
    s_bufs, p_bufs = (s0_ref, s1_ref), (p0_ref, p1_ref)

    def scores(hh):
        s_bufs[hh % 2][...] = _dot_nt(q_ref[0, hh], k_ref[0, hh, :nk, :])

    scores(0)
    for hh in range(MLA_HEADS):
        if hh + 1 < MLA_HEADS:
            scores(hh + 1)
        s_ref, p_ref = s_bufs[hh % 2], p_bufs[hh % 2]
        for r in range(0, tq, rb):
            s = s_ref[r:r + rb, :]
            m = jnp.max(s, axis=1, keepdims=True)
            p_ref[r:r + rb, :] = jnp.exp2(s - m).astype(BF16)
        acc = _dot(p_ref[...], v_ref[0, hh, :nk, :])
        o = acc[:, :MLA_V] / acc[:, MLA_V:MLA_V + 1]
        o_ref[0, :, hh * MLA_V:(hh + 1) * MLA_V] = o.astype(BF16)


def _mla_attention(Q, K, V, n_lat, context):
    B, H, T, E = Q.shape
    kc = 256
    if not context:
        tq, nk, nq = 512, T, n_lat
        q_spec = pl.BlockSpec((1, H, tq, E), lambda b, i: (b, 0, i, 0))
        kv_spec = pl.BlockSpec((1, H, T, E), lambda b, i: (b, 0, 0, 0))
    else:
        tq, nk, nq = CTX_LEN, CTX_LEN, CTX_LEN
        cblk = n_lat // CTX_LEN
        q_spec = pl.BlockSpec((1, H, tq, E), lambda b, i: (b, 0, cblk, 0))
        kv_spec = pl.BlockSpec((1, H, nk, E), lambda b, i: (b, 0, cblk, 0))
    return pl.pallas_call(
        functools.partial(_flash_kernel, nk=nk, rb=64),
        grid=(B, nq // tq),
        in_specs=[q_spec, kv_spec, kv_spec],
        out_specs=pl.BlockSpec((1, tq, H * MLA_V), lambda b, i: (b, i, 0)),
        out_shape=jax.ShapeDtypeStruct((B, nq, H * MLA_V), BF16),
        scratch_shapes=[pltpu.VMEM((tq, nk), F32), pltpu.VMEM((tq, nk), F32),
                        pltpu.VMEM((tq, nk), BF16), pltpu.VMEM((tq, nk), BF16)],
        compiler_params=_cparams(("parallel", "parallel"), 56),
        name="mla_attention_ctx" if context else "mla_attention",
    )(Q, K, V)


def _na_kernel(q_ref, k_ref, v_ref, bias_ref, o_ref, *, n_lat, rows, with_ctx):
    j = pl.program_id(1)
    nblk = rows // NA_QROWS
    scale = NA_DIM ** -0.5
    assert np.log2(scale) == round(np.log2(scale))
    q = (q_ref[0].astype(F32) * scale).astype(BF16)
    kctx = k_ref[0, n_lat:n_lat + CTX_LEN, :]
    vctx = v_ref[0, n_lat:n_lat + CTX_LEN, :]

    def local_step():
        kb = jnp.clip(j * NA_QROWS - NA_KR // 2, 0, rows - NA_KROWS)
        start = pl.multiple_of(kb * GRID_W, GRID_W)
        kloc = k_ref[0, pl.ds(start, NA_KROWS * GRID_W), :]
        vloc = v_ref[0, pl.ds(start, NA_KROWS * GRID_W), :]
        for hh in range(NA_HEADS):
            sl = slice(hh * NA_DIM, (hh + 1) * NA_DIM)
            qh = q[:, sl]
            s_loc = _dot_nt(qh, kloc[:, sl]) + bias_ref[0, hh]
            s_ctx = _dot_nt(qh, kctx[:, sl])
            m = jnp.maximum(jnp.max(s_loc, axis=1, keepdims=True), jnp.max(s_ctx, axis=1, keepdims=True))
            p_loc = jnp.exp(s_loc - m)
            p_ctx = jnp.exp(s_ctx - m)
            den = jnp.sum(p_loc, axis=1, keepdims=True) + jnp.sum(p_ctx, axis=1, keepdims=True)
            o = _dot(p_loc.astype(BF16), vloc[:, sl]) + _dot(p_ctx.astype(BF16), vctx[:, sl])
            o_ref[0, :, sl] = (o / den).astype(BF16)

    def ctx_step():
        for hh in range(NA_HEADS):
            sl = slice(hh * NA_DIM, (hh + 1) * NA_DIM)
            s = _dot_nt(q[:, sl], kctx[:, sl])
            p = jnp.exp(s - jnp.max(s, axis=1, keepdims=True))
            den = jnp.sum(p, axis=1, keepdims=True)
            o_ref[0, :, sl] = (_dot(p.astype(BF16), vctx[:, sl]) / den).astype(BF16)

    if with_ctx:
        pl.when(j < nblk)(local_step)
        pl.when(j == nblk)(ctx_step)
    else:
        local_step()


def _na_bias_table(rpb, rows):
    W = GRID_W
    nr, ncol = 2 * NA_KR - 1, 2 * NA_KC - 1
    w = np.arange(W)[:, None]
    c = np.arange(W)[None, :]
    cs = np.clip(w - NA_KC // 2, 0, W - NA_KC)
    col_ok = (c >= cs) & (c < cs + NA_KC)
    sel_c = (col_ok[:, :, None] & ((c - w + NA_KC - 1)[:, :, None] == np.arange(ncol))).astype(np.float32)
    sel_r = np.zeros((3, NA_QROWS, NA_KROWS, nr), np.float32)
    for p, r0 in enumerate((0, NA_QROWS, rows - NA_QROWS)):
        kb = int(np.clip(r0 - NA_KR // 2, 0, rows - NA_KROWS))
        for q in range(NA_QROWS):
            rs = int(np.clip(r0 + q - NA_KR // 2, 0, rows - NA_KR))
            for k in range(NA_KROWS):
                if rs <= kb + k < rs + NA_KR:
                    sel_r[p, q, k, kb + k - (r0 + q) + NA_KR - 1] = 1.0
    hi = lax.Precision.HIGHEST
    by_col = jnp.einsum("hrd,wcd->hrwc", rpb.astype(F32), jnp.asarray(sel_c), precision=hi)
    bias = jnp.einsum("pqkr,hrwc->phqwkc", jnp.asarray(sel_r), by_col, precision=hi)
    valid = np.einsum("pqk,wc->pqwkc", sel_r.sum(-1), col_ok.astype(np.float32)) > 0
    bias = jnp.where(jnp.asarray(valid)[:, None], bias, NEG_BIG)
    return bias.reshape(3, NA_HEADS, NA_QROWS * W, NA_KROWS * W)


def _neighbourhood_attention(nq, nk, nv, bias_tab, n_lat, with_ctx):
    B, T, w = nq.shape
    rows = n_lat // GRID_W
    nblk = rows // NA_QROWS
    tq = NA_QROWS * GRID_W
    steps = nblk + (1 if with_ctx else 0)

    def pat(j):
        return jnp.where(j == 0, 0, jnp.where(j >= nblk - 1, 2, 1))

    return pl.pallas_call(
        functools.partial(_na_kernel, n_lat=n_lat, rows=rows, with_ctx=with_ctx),
        grid=(B, steps),
        in_specs=[pl.BlockSpec((1, tq, w), lambda b, j: (b, j, 0)),
                  pl.BlockSpec((1, T, w), lambda b, j: (b, 0, 0)),
                  pl.BlockSpec((1, T, w), lambda b, j: (b, 0, 0)),
                  pl.BlockSpec((1, NA_HEADS, tq, NA_KROWS * GRID_W), lambda b, j: (pat(j), 0, 0, 0))],
        out_specs=pl.BlockSpec((1, tq, w), lambda b, j: (b, j, 0)),
        out_shape=jax.ShapeDtypeStruct((B, T, w), BF16),
        compiler_params=_cparams(("parallel", "arbitrary"), 48),
        name="neighbourhood_attention",
    )(nq, nk, nv, bias_tab)


def _mlstm_kernel(qf_ref, kf_ref, vf_ref, gf_ref, gtf_ref, qb_ref, kb_ref, vb_ref, gb_ref, gtb_ref,
                  bcol_ref, brow_ref, hf_ref, hb_ref, st_ref, m_ref):
    L = ML_CHUNK

    @pl.when(pl.program_id(1) == 0)
    def _():
        st_ref[...] = jnp.zeros_like(st_ref)
        m_ref[...] = jnp.zeros_like(m_ref)

    r_i = lax.broadcasted_iota(jnp.int32, (L, L), 0)
    c_i = lax.broadcasted_iota(jnp.int32, (L, L), 1)
    ones_t = jnp.ones((ML_V, L), BF16)
    dirs = ((qf_ref, kf_ref, vf_ref, gf_ref, gtf_ref, hf_ref), (qb_ref, kb_ref, vb_ref, gb_ref, gtb_ref, hb_ref))
    nb = qf_ref.shape[0]
    ng = 4 * ML_HEADS
    for bi, d in [(bi, d) for d in range(2) for bi in range(nb)]:
        q_ref, k_ref, v_ref, g_ref, gt_ref, h_ref = dirs[d]
        if bi == 0:
            causal_t = (r_i <= c_i) if d == 0 else (r_i >= c_i)
            m_row = jnp.where(causal_t, 1.0, 0.0).astype(BF16)
            m_col = jnp.where((c_i <= r_i) if d == 0 else (c_i >= r_i), 1.0, 0.0).astype(BF16)
            gt_all = jnp.concatenate([gt_ref[i] + bcol_ref[...] for i in range(nb)], axis=0)
            lf_r = _log_sigmoid(gt_all)
            br_all = sum(_dot(p, m_row) for p in _split3(lf_r))
            tot_all = jnp.sum(lf_r, axis=1, keepdims=True)
            gc_all = jnp.concatenate([g_ref[i] + brow_ref[...] for i in range(nb)], axis=1)
            bc_all = sum(_dot(m_col, p) for p in _split3(_log_sigmoid(gc_all)))
        gt = gt_all[bi * ng:(bi + 1) * ng, :]
        br = br_all[bi * ng:(bi + 1) * ng, :]
        tot = tot_all[bi * ng:(bi + 1) * ng, :]
        gc = gc_all[:, bi * LANES:(bi + 1) * LANES]
        bc = bc_all[:, bi * LANES:(bi + 1) * LANES]
        k16 = (k_ref[bi].astype(F32) * (ML_QK ** -0.5)).astype(BF16)
        q16 = q_ref[bi]
        q_t = q16.astype(F32).T
        v_t = v_ref[bi].astype(F32).T.astype(BF16)
        for hh in range(ML_HEADS):
            c = (bi * 2 + d) * ML_HEADS + hh
            ji = (2 * d) * ML_HEADS + hh
            jf = (2 * d + 1) * ML_HEADS + hh
            i_row = gt[ji:ji + 1, :]
            b_row = br[jf:jf + 1, :]
            b_last = tot[jf:jf + 1, :]
            src_col = gc[:, ji:ji + 1] - bc[:, jf:jf + 1]
            m_old = m_ref[c:c + 1, 0:1]
            st_old = st_ref[c]
            qh = q16[:, hh * ML_QK:(hh + 1) * ML_QK]
            kh = k16[:, hh * ML_QK:(hh + 1) * ML_QK]
            qh_t = q_t[hh * ML_QK:(hh + 1) * ML_QK, :]
            va_t = jnp.concatenate([v_t[hh * ML_V:(hh + 1) * ML_V, :], ones_t], axis=0)
            d_log = jnp.where(causal_t, src_col + b_row, -jnp.inf)
            inter = b_row + m_old
            m_t = jnp.maximum(inter, jnp.max(d_log, axis=0, keepdims=True))
            a = jnp.exp(inter - m_t)
            smat_t = (_dot_nt(kh, qh) * jnp.exp(d_log - m_t)).astype(BF16)
            lhs = jnp.concatenate([va_t, st_old.astype(BF16)], axis=1)
            rhs = jnp.concatenate([smat_t, (qh_t * a).astype(BF16)], axis=0)
            out_t = _dot(lhs, rhs)
            den = jnp.maximum(jnp.abs(out_t[ML_V:, :]), jnp.exp(-m_t))
            h_ref[bi, :, hh * ML_V:(hh + 1) * ML_V] = (out_t[:ML_V, :] / den).T
            w_log = b_last - b_row + i_row
            m_new = jnp.maximum(b_last + m_old, jnp.max(w_log, axis=1, keepdims=True))
            decay = jnp.exp(b_last + m_old - m_new)
            vw_t = (va_t.astype(F32) * jnp.exp(w_log - m_new)).astype(BF16)
            st_ref[c] = decay * st_old + _dot(vw_t, kh)
            m_ref[c:c + 1, :] = jnp.broadcast_to(m_new, (1, LANES))


def _mlstm(mq, mk, mv, mg, mgt, b_ml, n_lat):
    B, T, _ = mq.shape
    L = ML_CHUNK
    nl = n_lat // L
    nc = CTX_LEN // L
    steps = nl + nc

    def cf(s):
        return jnp.where(s < nc, nl + s, s - nc)

    def cb(s):
        return steps - 1 - s

    nb = 2 * PAIR if B % (2 * PAIR) == 0 else PAIR

    def specs(cm):
        return [pl.BlockSpec((nb, L, ML_HEADS * ML_QK), lambda b, s: (b, cm(s), 0)),
                pl.BlockSpec((nb, L, ML_HEADS * ML_QK), lambda b, s: (b, cm(s), 0)),
                pl.BlockSpec((nb, L, ML_HEADS * ML_V), lambda b, s: (b, cm(s), 0)),
                pl.BlockSpec((nb, L, LANES), lambda b, s: (b, cm(s), 0)),
                pl.BlockSpec((nb, 4 * ML_HEADS, L), lambda b, s: (b, 0, cm(s)))]

    b_row = jnp.concatenate([b_ml.astype(F32), jnp.zeros((LANES - 4 * ML_HEADS,), F32)])[None, :]
    b_col = b_ml.astype(F32)[:, None]
    sd = jax.ShapeDtypeStruct
    n_chain = nb * 2 * ML_HEADS
    return pl.pallas_call(
        _mlstm_kernel,
        grid=(B // nb, steps),
        in_specs=specs(cf) + specs(cb) + [pl.BlockSpec((4 * ML_HEADS, 1), lambda b, s: (0, 0)),
                                          pl.BlockSpec((1, LANES), lambda b, s: (0, 0))],
        out_specs=[pl.BlockSpec((nb, L, ML_HEADS * ML_V), lambda b, s: (b, cf(s), 0)),
                   pl.BlockSpec((nb, L, ML_HEADS * ML_V), lambda b, s: (b, cb(s), 0))],
        out_shape=[sd((B, T, ML_HEADS * ML_V), F32)] * 2,
        scratch_shapes=[pltpu.VMEM((n_chain, 2 * ML_V, ML_QK), F32), pltpu.VMEM((n_chain, LANES), F32)],
        compiler_params=_cparams(("parallel", "arbitrary"), 32),
        name="mlstm",
    )(mq, mk, mv, mg, mgt, mq, mk, mv, mg, mgt, b_col, b_row)


def _merge_kernel(a_ref, bn_ref, hf_ref, hb_ref, mo_ref, bg_ref, x_ref, xc_ref, gt1_ref, sh2_ref, sc2_ref,
                  gml_ref, gpost_ref, gpre2_ref, wa_ref, wb_ref, wm_ref, wo_ref, wrh_ref, wrl_ref,
                  xo_ref, h2_ref, aff_ref, *, ctx_tile):
    nb, tm, d = x_ref.shape
    gml = gml_ref[...]
    is_ctx = pl.program_id(1) >= ctx_tile
    for bi in range(nb):
        hm = hf_ref[bi] + hb_ref[bi]
        parts = []
        for hh in range(ML_HEADS):
            sl = slice(hh * ML_V, (hh + 1) * ML_V)
            parts.append(_rms(hm[:, sl], gml[:, sl]))
        hn = jnp.concatenate(parts, axis=1)
        mlo = (hn + hn * jnp.tanh(mo_ref[bi].astype(F32))).astype(BF16)
        ya = _dot(a_ref[bi], wa_ref[...])
        yb = _dot(bn_ref[bi], wb_ref[...])
        ym = _dot(mlo, wm_ref[...])
        y = ((ya + yb + ym)
             + ya * jnp.tanh(bg_ref[bi, :, 0:d].astype(F32))
             + yb * jnp.tanh(bg_ref[bi, :, d:2 * d].astype(F32))
             + ym * jnp.tanh(bg_ref[bi, :, 2 * d:3 * d].astype(F32)))
        out = _dot(y.astype(BF16), wo_ref[...])
        x_in = jnp.where(is_ctx, xc_ref[bi], x_ref[bi])
        xn = x_in + _rms(out, gpost_ref[...] * gt1_ref[bi])
        xo_ref[bi] = xn
        h2 = _rms(xn, gpre2_ref[...] * (1.0 + sc2_ref[bi])) + sh2_ref[bi]
        h2_ref[bi] = h2.astype(BF16)
        hi = h2.astype(BF16)
        lo = (h2 - hi.astype(F32)).astype(BF16)
        logits = _dot(hi, wrh_ref[...]) + _dot(lo, wrh_ref[...]) + _dot(hi, wrl_ref[...])
        lane = lax.broadcasted_iota(jnp.int32, logits.shape, 1)
        logits = jnp.where(lane < N_EXPERTS, logits, -jnp.inf)
        e = jnp.exp(logits - jnp.max(logits, axis=1, keepdims=True))
        aff_ref[bi] = e / jnp.sum(e, axis=1, keepdims=True)


def _merge(a, bn, hf, hb, mo, bg, X, mod3, g_ml, g_post1, g_pre2, wa, wb, wm, wo, wrh, wrl, n_lat, with_ctx):
    tm = ROW_TILE
    nb = PAIR
    x_lat, x_ctx, lat_spec, ctx_spec = _residual_specs(X, n_lat, nb, tm)
    B, d = x_lat.shape[0], x_lat.shape[2]
    T = n_lat + CTX_LEN
    ctx_tile = n_lat // tm
    nt = ctx_tile + (1 if with_ctx else 0)

    def mrow(b, i):
        return jnp.where(i >= ctx_tile, B // nb, b)

    full = lambda shape: pl.BlockSpec(shape, lambda b, i: (0,) * len(shape))
    row = lambda w: pl.BlockSpec((nb, tm, w), lambda b, i: (b, i, 0))
    modc = lambda ch: pl.BlockSpec((nb, 1, d), lambda b, i: (mrow(b, i), 0, ch))
    sd = jax.ShapeDtypeStruct
    return pl.pallas_call(
        functools.partial(_merge_kernel, ctx_tile=ctx_tile),
        grid=(B // nb, nt),
        in_specs=[row(256), row(256), row(512), row(512), row(512), row(3 * d), lat_spec, ctx_spec,
                  modc(2), modc(3), modc(4),
                  full((1, ML_HEADS * ML_V)), full((1, d)), full((1, d)),
                  full((256, d)), full((256, d)), full((512, d)), full((d, d)), full((d, LANES)), full((d, LANES))],
        out_specs=[row(d), row(d), row(LANES)],
        out_shape=[sd((B, T, d), F32), sd((B, T, d), BF16), sd((B, T, LANES), F32)],
        compiler_params=_cparams(("parallel", "parallel"), 48),
        name="merge",
    )(a, bn, hf, hb, mo, bg, x_lat, x_ctx, mod3, mod3, mod3, g_ml, g_post1, g_pre2, wa, wb, wm, wo, wrh, wrl)


def _select_kernel(aff_ref, slot_ref, start_ref, *, cap):
    aff = aff_ref[0]
    n = aff.shape[1]
    bits = pltpu.bitcast(aff, jnp.int32)
    capf = jnp.float32(cap)

    def count(mask):
        return jnp.sum(jnp.where(mask, 1.0, 0.0), axis=1, keepdims=True)

    def refine(it, thr):
        cand = thr | jnp.left_shift(jnp.int32(1), 30 - it)
        return jnp.where(count(bits >= cand) >= capf, cand, thr)

    thr = lax.fori_loop(0, 31, refine, jnp.zeros((N_EXPERTS, 1), jnp.int32))
    gt = bits > thr
    eq = bits == thr
    need = capf - count(gt)
    r_i = lax.broadcasted_iota(jnp.int32, (LANES, LANES), 0)
    c_i = lax.broadcasted_iota(jnp.int32, (LANES, LANES), 1)
    before = jnp.where(r_i < c_i, 1.0, 0.0).astype(BF16)

    def excl_prefix(mask):
        carry = jnp.zeros((N_EXPERTS, 1), F32)
        outs, carries = [], []
        for c in range(n // LANES):
            carries.append(carry)
            mc = jnp.where(mask[:, c * LANES:(c + 1) * LANES], 1.0, 0.0)
            outs.append(_dot(mc.astype(BF16), before) + carry)
            carry = carry + jnp.sum(mc, axis=1, keepdims=True)
        return jnp.concatenate(outs, axis=1), carries + [carry]

    sel = gt | (eq & (excl_prefix(eq)[0] < need))
    rank, carries = excl_prefix(sel)
    slot_ref[0] = jnp.where(sel, rank, -1.0)
    lane = lax.broadcasted_iota(jnp.int32, (N_EXPERTS, LANES), 1)
    per_tile = TOK_TILE // LANES
    starts = jnp.zeros((N_EXPERTS, LANES), F32)
    for k in range(n // TOK_TILE + 1):
        starts = jnp.where(lane == k, carries[k * per_tile], starts)
    start_ref[0] = starts


def _select(aff_t, cap):
    B, E, n = aff_t.shape
    slot, starts = pl.pallas_call(
        functools.partial(_select_kernel, cap=cap),
        grid=(B,),
        in_specs=[pl.BlockSpec((1, E, n), lambda b: (b, 0, 0))],
        out_specs=[pl.BlockSpec((1, E, n), lambda b: (b, 0, 0)), pl.BlockSpec((1, E, LANES), lambda b: (b, 0, 0))],
        out_shape=[jax.ShapeDtypeStruct((B, E, n), F32), jax.ShapeDtypeStruct((B, E, LANES), F32)],
        compiler_params=_cparams(("parallel",), 32),
        name="route_select",
    )(aff_t)
    return slot, starts[:, :, :n // TOK_TILE + 1].astype(jnp.int32)


def _log2(v):
    assert v & (v - 1) == 0
    return v.bit_length() - 1


def _window_plan(start_ref, b, kt, cap, win):
    firsts, rounds = [], jnp.int32(0)
    for e in range(N_EXPERTS):
        lo = start_ref[b, e, kt]
        hi = start_ref[b, e, kt + 1]
        first = jnp.minimum(lax.shift_right_logical(lo, _log2(BF16_ROWS)) * BF16_ROWS, cap - win)
        firsts.append(first)
        rounds = jnp.maximum(rounds, lax.shift_right_logical(hi - first + (win - 1), _log2(win)))
    return firsts, rounds


def _dispatch_kernel(start_ref, slot_ref, h_ref, xs_ref, *, cap, win):
    b, kt = pl.program_id(0), pl.program_id(1)

    @pl.when(kt == 0)
    def _():
        xs_ref[...] = jnp.zeros_like(xs_ref)

    slot = slot_ref[0]
    tile = slot.shape[1]
    h = h_ref[0]
    firsts, rounds = _window_plan(start_ref, b, kt, cap, win)
    sub = lax.broadcasted_iota(jnp.int32, (win, tile), 0)

    def one_round(r, carry):
        rows, pieces = [], []
        for e in range(N_EXPERTS):
            base = firsts[e] + r * win
            row0 = pl.multiple_of(jnp.minimum(base, cap - win), BF16_ROWS)
            abs_slot = row0 + sub
            hit = (abs_slot.astype(F32) == slot[e:e + 1, :]) & (abs_slot >= base)
            pieces.append(jnp.where(hit, 1.0, 0.0).astype(BF16))
            rows.append(row0)
        res = _dot(jnp.concatenate(pieces, axis=0), h)
        for e in range(N_EXPERTS):
            cur = xs_ref[0, e, pl.ds(rows[e], win), :]
            xs_ref[0, e, pl.ds(rows[e], win), :] = cur + res[e * win:(e + 1) * win].astype(BF16)
        return carry

    lax.fori_loop(0, rounds, one_round, 0)


def _dispatch(starts, slot, h2, cap, n, tok_off):
    B, E, _ = slot.shape
    d = h2.shape[2]
    blk0 = tok_off // TOK_TILE
    grid_spec = pltpu.PrefetchScalarGridSpec(
        num_scalar_prefetch=1,
        grid=(B, n // TOK_TILE),
        in_specs=[pl.BlockSpec((1, E, TOK_TILE), lambda b, k, st: (b, 0, k)),
                  pl.BlockSpec((1, TOK_TILE, d), lambda b, k, st: (b, blk0 + k, 0))],
        out_specs=pl.BlockSpec((1, E, cap, d), lambda b, k, st: (b, 0, 0, 0)))
    return pl.pallas_call(
        functools.partial(_dispatch_kernel, cap=cap, win=min(SLOT_WIN, cap)),
        grid_spec=grid_spec,
        out_shape=jax.ShapeDtypeStruct((B, E, cap, d), BF16),
        compiler_params=_cparams(("parallel", "arbitrary"), 56),
        name="moe_dispatch",
    )(starts, slot, h2)


def _ffn_kernel(*refs, with_ctx):
    if with_ctx:
        xs_ref, xc_ref, w1_ref, w3_ref, w2_ref, y_ref, yc_ref, w1b, w3b, w2b = refs
    else:
        xs_ref, w1_ref, w3_ref, w2_ref, y_ref, w1b, w3b, w2b = refs

    @pl.when(pl.program_id(1) == 0)
    def _():
        step = 256
        for src, dst in ((w1_ref, w1b), (w3_ref, w3b), (w2_ref, w2b)):
            for r in range(0, src.shape[2], step):
                dst[r:r + step, :] = src[0, 0, r:r + step, :].astype(BF16)

    nb, _, cap, _ = xs_ref.shape
    parts = [xs_ref[i, 0] for i in range(nb)]
    if with_ctx:
        parts += [xc_ref[i, 0] for i in range(nb)]
    x = jnp.concatenate(parts, axis=0)
    h1 = _dot(x, w1b[...])
    h3 = _dot(x, w3b[...])
    hid = (h1 * jax.nn.sigmoid(h1) * h3).astype(BF16)
    y = _dot(hid, w2b[...]).astype(BF16)
    for i in range(nb):
        y_ref[i, 0] = y[i * cap:(i + 1) * cap]
    if with_ctx:
        cap_c = xc_ref.shape[2]
        for i in range(nb):
            yc_ref[i, 0] = y[nb * cap + i * cap_c:nb * cap + (i + 1) * cap_c]


def _expert_ffn(xs, xs_ctx, w1, w3, w2, layer):
    B, E, cap, d = xs.shape
    f = w1.shape[3]
    slots = lambda c: pl.BlockSpec((PAIR, 1, c, d), lambda e, b: (b, e, 0, 0))
    x_in, x_specs = [xs], [slots(cap)]
    out_specs, out_shape = [slots(cap)], [jax.ShapeDtypeStruct((B, E, cap, d), BF16)]
    if xs_ctx is not None:
        cap_c = xs_ctx.shape[2]
        x_in.append(xs_ctx)
        x_specs.append(slots(cap_c))
        out_specs.append(slots(cap_c))
        out_shape.append(jax.ShapeDtypeStruct((B, E, cap_c, d), BF16))
    outs = pl.pallas_call(
        functools.partial(_ffn_kernel, with_ctx=xs_ctx is not None),
        grid=(E, B // PAIR),
        in_specs=x_specs + [pl.BlockSpec((1, 1, d, f), lambda e, b: (layer, e, 0, 0)),
                            pl.BlockSpec((1, 1, d, f), lambda e, b: (layer, e, 0, 0)),
                            pl.BlockSpec((1, 1, f, d), lambda e, b: (layer, e, 0, 0))],
        out_specs=out_specs,
        out_shape=out_shape,
        scratch_shapes=[pltpu.VMEM((d, f), BF16), pltpu.VMEM((d, f), BF16), pltpu.VMEM((f, d), BF16)],
        compiler_params=_cparams(("arbitrary", "arbitrary"), 56),
        name="moe_expert_ffn",
    )(*x_in, w1, w3, w2)
    return outs if xs_ctx is not None else (outs[0], None)


def _combine_kernel(start_ref, slotc_ref, aff_ref, y_ref, x_ref, gt2_ref, gpost_ref, xo_ref, acc_ref, *, cap, win):
    b, kt = pl.program_id(0), pl.program_id(1)
    slotc = slotc_ref[0]
    aff = aff_ref[0]
    tile = slotc.shape[0]
    firsts, rounds = _window_plan(start_ref, b, kt, cap, win)
    width = N_EXPERTS * win
    col = lax.broadcasted_iota(jnp.int32, (1, width), 1)
    owner = lax.shift_right_logical(col, _log2(win))
    off = col - owner * win
    spread = jnp.where(lax.broadcasted_iota(jnp.int32, (LANES, width), 0) == owner, 1.0, 0.0).astype(BF16)
    half = jnp.floor(slotc * 0.5)
    g_hi = aff.astype(BF16)
    g_lo = (aff - g_hi.astype(F32)).astype(BF16)
    stacked = jnp.concatenate([half.astype(BF16), (slotc - 2.0 * half).astype(BF16), g_hi, g_lo], axis=0)
    rep = _dot(stacked, spread)
    scol = 2.0 * rep[:tile] + rep[tile:2 * tile]
    ghi_rep = rep[2 * tile:3 * tile]
    glo_rep = rep[3 * tile:]
    acc_ref[...] = jnp.zeros_like(acc_ref)

    def one_round(r, carry):
        row0v = jnp.zeros((1, width), jnp.int32)
        basev = jnp.zeros((1, width), jnp.int32)
        rhs_p = []
        for e in range(N_EXPERTS):
            base = firsts[e] + r * win
            row0 = pl.multiple_of(jnp.minimum(base, cap - win), BF16_ROWS)
            row0v = jnp.where(owner == e, row0, row0v)
            basev = jnp.where(owner == e, base, basev)
            rhs_p.append(y_ref[0, e, pl.ds(row0, win), :])
        abs_slot = row0v + off
        hit = (abs_slot.astype(F32) == scol) & (abs_slot >= basev)
        lhs = jnp.concatenate([jnp.where(hit, ghi_rep, 0.0).astype(BF16),
                               jnp.where(hit, glo_rep, 0.0).astype(BF16)], axis=0)
        res = _dot(lhs, jnp.concatenate(rhs_p, axis=0))
        acc_ref[...] += res[:tile] + res[tile:]
        return carry

    lax.fori_loop(0, rounds, one_round, 0)
    xo_ref[0] = x_ref[0] + _rms(acc_ref[...], gpost_ref[...] * gt2_ref[0])


def _combine(starts, slotc, aff, y, X, mod3, g_post2, n, tok_off, is_ctx, in_place):
    B, T, d = X.shape
    E, cap = y.shape[1], y.shape[2]
    tt = TOK_TILE
    blk0 = tok_off // tt
    mrow = (lambda b: B) if is_ctx else (lambda b: b)
    if in_place:
        out_spec = pl.BlockSpec((1, tt, d), lambda b, t, st: (b, blk0 + t, 0))
        out_shape, aliases = jax.ShapeDtypeStruct((B, T, d), F32), {4: 0}
    else:
        out_spec = pl.BlockSpec((1, tt, d), lambda b, t, st: (b, t, 0))
        out_shape, aliases = jax.ShapeDtypeStruct((B, n, d), F32), {}
    grid_spec = pltpu.PrefetchScalarGridSpec(
        num_scalar_prefetch=1,
        grid=(B, n // tt),
        in_specs=[pl.BlockSpec((1, tt, LANES), lambda b, t, st: (b, t, 0)),
                  pl.BlockSpec((1, tt, LANES), lambda b, t, st: (b, blk0 + t, 0)),
                  pl.BlockSpec((1, E, cap, d), lambda b, t, st: (b, 0, 0, 0)),
                  pl.BlockSpec((1, tt, d), lambda b, t, st: (b, blk0 + t, 0)),
                  pl.BlockSpec((1, 1, d), lambda b, t, st: (mrow(b), 0, 5)),
                  pl.BlockSpec((1, d), lambda b, t, st: (0, 0))],
        out_specs=out_spec,
        scratch_shapes=[pltpu.VMEM((tt, d), F32)])
    return pl.pallas_call(
        functools.partial(_combine_kernel, cap=cap, win=min(SLOT_WIN, cap)),
        grid_spec=grid_spec,
        out_shape=out_shape,
        input_output_aliases=aliases,
        compiler_params=_cparams(("parallel", "arbitrary"), 56),
        name="moe_combine",
    )(starts, slotc, aff, y, X, mod3, g_post2)


def _moe(X, h2, aff, mod3, g_post2, w1, w3, w2, layer, n_lat, with_ctx, in_place):
    def route(n, off):
        cap = EC_FACTOR * n // N_EXPERTS
        aff_t = jnp.transpose(aff[:, off:off + n, :N_EXPERTS], (0, 2, 1))
        slot, starts = _select(aff_t, cap)
        slotc = _pad_cols(jnp.transpose(slot, (0, 2, 1)), 0, LANES)
        return slotc, starts, _dispatch(starts, slot, h2, cap, n, off)

    slotc, starts, xs = route(n_lat, 0)
    slotc_c, starts_c, xs_c = route(CTX_LEN, n_lat) if with_ctx else (None, None, None)
    y, y_c = _expert_ffn(xs, xs_c, w1, w3, w2, layer)
    X = _combine(starts, slotc, aff, y, X, mod3, g_post2, n_lat, 0, False, in_place)
    if with_ctx:
        X = _combine(starts_c, slotc_c, aff, y_c, X, mod3, g_post2, CTX_LEN, n_lat, True, True)
    return X


def _rope_partner(w):
    lead = w.shape[:-1]
    ws = w.reshape(lead + (2, 2, MLA_ROPE // 4))
    return jnp.stack([-ws[..., 1, :], ws[..., 0, :]], axis=-2).reshape(lead + (MLA_ROPE,))


def _pad_cols(w, left, total):
    return jnp.pad(w, [(0, 0)] * (w.ndim - 1) + [(left, total - left - w.shape[-1])])


def _prep_in_weights(w_in):
    sizes = (MLA_Q_RANK, MLA_KV_RANK, MLA_ROPE, 256, 256, 256, 256, 256, 512, 512, 4 * ML_HEADS, 3 * D_MODEL)
    offs = np.cumsum(sizes)[:-1].tolist()
    cq, ckv, kr, nq, nk, nv, mq, mk, mv, mo, mg, bg = jnp.split(w_in, offs, axis=-1)
    kr1 = _pad_cols(kr, MLA_NOPE, HEAD_PAD)
    kr2 = _pad_cols(_rope_partner(kr), MLA_NOPE, HEAD_PAD)
    mgp = _pad_cols(mg, 0, LANES)
    return jnp.concatenate([cq, ckv, kr1, kr2, nq, nk, nv, mq, mk, mv, 0.5 * mo, mgp, 0.5 * bg], axis=-1).astype(BF16)


def _prep_mla_weights(w_uq, w_ukv):
    nope, rope = w_uq[..., :MLA_NOPE], w_uq[..., MLA_NOPE:]
    wqa = _pad_cols(jnp.concatenate([nope, rope], axis=-1), 0, HEAD_PAD)
    wqb = _pad_cols(_rope_partner(rope), MLA_NOPE, HEAD_PAD)
    wk = _pad_cols(w_ukv[..., :MLA_NOPE], 0, HEAD_PAD)
    wv = _pad_cols(w_ukv[..., MLA_NOPE:], 0, HEAD_PAD)
    flat = lambda w: w.reshape(w.shape[0], MLA_HEADS * HEAD_PAD).astype(BF16)
    return flat(wqa), flat(wqb), flat(wk), flat(wv)


def _rope_tables(n_lat):
    t = np.arange(n_lat)
    pos = np.stack([t // GRID_W, t % GRID_W], axis=-1).astype(np.float64)
    nf = MLA_ROPE // 4
    inv = np.float32(ROPE_BASE) ** (-np.arange(nf, dtype=np.float32) / nf)
    ang = (pos[..., None].astype(np.float32) * inv).astype(np.float64)
    cos = np.repeat(np.cos(ang)[:, :, None, :], 2, axis=2).reshape(n_lat, MLA_ROPE)
    sin = np.repeat(np.sin(ang)[:, :, None, :], 2, axis=2).reshape(n_lat, MLA_ROPE)
    cos = np.concatenate([cos, np.ones((CTX_LEN, MLA_ROPE))], axis=0)
    sin = np.concatenate([sin, np.zeros((CTX_LEN, MLA_ROPE))], axis=0)
    T = n_lat + CTX_LEN
    z = lambda w: np.zeros((T, w))
    scale = (MLA_NOPE + MLA_ROPE) ** -0.5 * np.log2(np.e)
    ta = scale * np.concatenate([np.ones((T, MLA_NOPE)), cos, z(HEAD_PAD - MLA_NOPE - MLA_ROPE)], axis=1)
    tb = scale * np.concatenate([z(MLA_NOPE), sin, z(HEAD_PAD - MLA_NOPE - MLA_ROPE)], axis=1)
    tc = np.concatenate([z(MLA_NOPE), cos, z(HEAD_PAD - MLA_NOPE - MLA_ROPE)], axis=1)
    ts = np.concatenate([z(MLA_NOPE), sin, z(HEAD_PAD - MLA_NOPE - MLA_ROPE)], axis=1)
    return tuple(jnp.asarray(a, F32) for a in (ta, tb, tc, ts))


def kernel(x, c, ctx, c_ctx, w_mod, b_mod, g_pre1, g_post1, g_pre2, g_post2, w_in, g_q, g_kv, w_uq, w_ukv, rpb,
           b_ml_gates, g_ml, w_br_mla, w_br_na, w_br_ml, w_out, w_router, w_e1, w_e3, w_e2):
    B, n_lat, d = x.shape
    depth = w_mod.shape[0]
    rows = n_lat // GRID_W
    assert d == D_MODEL and ctx.shape[1] == CTX_LEN and n_lat % 512 == 0 and rows >= NA_KROWS + NA_QROWS and B % PAIR == 0

    X = (x, ctx)
    mod_rows = -(-(B + PAIR) // 8) * 8
    c_all = jnp.concatenate([c, jnp.broadcast_to(c_ctx[None, :], (PAIR, d)),
                             jnp.zeros((mod_rows - B - PAIR, d), F32)], axis=0)
    mod = _modulation(c_all, w_mod, b_mod)
    tabs = _rope_tables(n_lat)
    row2 = lambda v: v.astype(F32)[None, :]

    for l in range(depth):
        ctx_out = l < depth - 1
        mod3 = mod[l][:, None, :]
        w_pad = _prep_in_weights(w_in[l])
        wqa, wqb, wk, wv = _prep_mla_weights(w_uq[l], w_ukv[l])
        (Q, K, V, nq, nk, nv, mq, mk, mv, mo, mg, bg) = _inproj(
            X, mod3, row2(g_pre1[l]), w_pad, row2(g_q[l]), row2(g_kv[l]), wqa, wqb, wk, wv, tabs, n_lat)
        a = jnp.concatenate([_mla_attention(Q, K, V, n_lat, False), _mla_attention(Q, K, V, n_lat, True)], axis=1)
        bn = _neighbourhood_attention(nq, nk, nv, _na_bias_table(rpb[l], rows), n_lat, True)
        mgt = jnp.transpose(mg[:, :, :4 * ML_HEADS], (0, 2, 1))
        hf, hb = _mlstm(mq, mk, mv, mg, mgt, b_ml_gates[l], n_lat)
        wr = _pad_cols(w_router[l].astype(F32), 0, LANES)
        wrh = wr.astype(BF16)
        wrl = (wr - wrh.astype(F32)).astype(BF16)
        half = lambda w: (0.5 * w).astype(BF16)
        X, h2, aff = _merge(a, bn, hf, hb, mo, bg, X, mod3, row2(0.5 * g_ml[l]), row2(g_post1[l]), row2(g_pre2[l]),
                            half(w_br_mla[l]), half(w_br_na[l]), half(w_br_ml[l]),
                            w_out[l].astype(BF16), wrh, wrl, n_lat, True)
        X = _moe(X, h2, aff, mod3, row2(g_post2[l]), w_e1, w_e3, w_e2, l, n_lat, ctx_out, in_place=ctx_out)
    return X
```

```python
import functools

import numpy as np
import jax
import jax.numpy as jnp
from jax import lax
from jax.experimental import pallas as pl
from jax.experimental.pallas import tpu as pltpu

F32 = jnp.float32
BF16 = jnp.bfloat16

D_MODEL = 1024
CTX_LEN = 256
GRID_W = 64
EPS = 1e-6

MLA_HEADS = 4
MLA_Q_RANK = 256
MLA_KV_RANK = 128
MLA_NOPE = 64
MLA_ROPE = 32
MLA_V = 64
ROPE_BASE = 10000.0

NA_HEADS = 4
NA_DIM = 64
NA_KR = 8
NA_KC = 16
NA_QROWS = 4
NA_KROWS = 12

ML_HEADS = 4
ML_QK = 64
ML_V = 128
ML_CHUNK = 128

N_EXPERTS = 16
D_EXPERT = 1024
EC_FACTOR = 2

LANES = 128
ROW_TILE = 256
PAIR = 2
TOK_TILE = 256
SLOT_WIN = 64
BF16_ROWS = 16
HEAD_PAD = 128
NEG_BIG = -1e30

_IN_GROUPS = (("cq", 256), ("ckv", 128), ("kr1", 128), ("kr2", 128), ("nq", 256), ("nk", 256), ("nv", 256),
              ("mq", 256), ("mk", 256), ("mv", 512), ("mo", 512), ("mg", 128), ("bg", 3 * D_MODEL))
_OFF = {}
_o = 0
for _n, _w in _IN_GROUPS:
    _OFF[_n] = (_o, _o + _w)
    _o += _w
D_IN_PAD = _o


def _cparams(sem, vmem_mb):
    return pltpu.CompilerParams(dimension_semantics=sem, vmem_limit_bytes=vmem_mb * 1024 * 1024)


def _dot(a, b):
    return jnp.dot(a, b, preferred_element_type=F32)


def _dot_nt(a, b):
    return lax.dot_general(a, b, (((1,), (1,)), ((), ())), preferred_element_type=F32)


def _rms(x, g):
    return x * lax.rsqrt(jnp.mean(x * x, axis=-1, keepdims=True) + EPS) * g


def _split3(x):
    hi = x.astype(BF16)
    r1 = x - hi.astype(F32)
    mid = r1.astype(BF16)
    lo = (r1 - mid.astype(F32)).astype(BF16)
    return hi, mid, lo


def _log_sigmoid(x):
    return jnp.minimum(x, 0.0) - jnp.log(1.0 + jnp.exp(-jnp.abs(x)))


def _mod_kernel(c_ref, w_ref, b_ref, o_ref):
    cs = c_ref[...]
    cs = cs * jax.nn.sigmoid(cs)
    o_ref[0] = _dot(cs.astype(BF16), w_ref[0].astype(BF16)) + b_ref[0]


def _modulation(c_all, w_mod, b_mod):
    depth, d, n6 = w_mod.shape
    rows = c_all.shape[0]
    tn = 512
    return pl.pallas_call(
        _mod_kernel,
        grid=(depth, n6 // tn),
        in_specs=[pl.BlockSpec((rows, d), lambda l, j: (0, 0)),
                  pl.BlockSpec((1, d, tn), lambda l, j: (l, 0, j)),
                  pl.BlockSpec((1, 1, tn), lambda l, j: (l, 0, j))],
        out_specs=pl.BlockSpec((1, rows, tn), lambda l, j: (l, 0, j)),
        out_shape=jax.ShapeDtypeStruct((depth, rows, n6), F32),
        compiler_params=_cparams(("parallel", "parallel"), 32),
        name="modulation",
    )(c_all, w_mod, b_mod.reshape(depth, 1, n6))


def _inproj_kernel(x_ref, xc_ref, sh_ref, sc_ref, g1_ref, w_ref, gq_ref, gkv_ref, wqa_ref, wqb_ref, wk_ref, wv_ref,
                   ta_ref, tb_ref, tc_ref, ts_ref,
                   q_ref, k_ref, v_ref, nq_ref, nk_ref, nv_ref, mq_ref, mk_ref, mv_ref, mo_ref, mg_ref, bg_ref,
                   *, ctx_tile):
    nb, tm, d = x_ref.shape
    rows = nb * tm
    split = lambda a: a.reshape(nb, tm, a.shape[-1])
    x = jnp.where(pl.program_id(1) >= ctx_tile, xc_ref[...], x_ref[...]).reshape(rows, d)
    xn = x * lax.rsqrt(jnp.mean(x * x, axis=-1, keepdims=True) + EPS)
    h = split(xn) * (g1_ref[...] * (1.0 + sc_ref[...])) + sh_ref[...]
    hb = h.reshape(rows, d).astype(BF16)

    def proj(name, lo=None, hi=None):
        a, b = _OFF[name]
        if lo is not None:
            a, b = a + lo, a + hi
        return _dot(hb, w_ref[:, a:b])

    def group(first, last):
        names = [n for n, _ in _IN_GROUPS]
        names = names[names.index(first):names.index(last) + 1]
        lo = _OFF[first][0]
        z = _dot(hb, w_ref[:, lo:_OFF[last][1]])
        return {n: z[:, _OFF[n][0] - lo:_OFF[n][1] - lo] for n in names}

    g0 = group("cq", "kr2")
    cq, ckv = g0["cq"], g0["ckv"]
    krr = split(g0["kr1"]) * tc_ref[...][None] + split(g0["kr2"]) * ts_ref[...][None]
    g1 = group("nq", "mk")
    nq_ref[...] = split(g1["nq"]).astype(BF16)
    nk_ref[...] = split(g1["nk"]).astype(BF16)
    nv_ref[...] = split(g1["nv"]).astype(BF16)
    mq_ref[...] = split(g1["mq"]).astype(BF16)
    mk_ref[...] = split(g1["mk"]).astype(BF16)
    cqn = _rms(cq, gq_ref[...]).astype(BF16)
    qa = _dot(cqn, wqa_ref[...])
    qb = _dot(cqn, wqb_ref[...])
    ckvn = _rms(ckv, gkv_ref[...]).astype(BF16)
    kn = _dot(ckvn, wk_ref[...])
    vv = _dot(ckvn, wv_ref[...])
    g2 = group("mv", "mg")
    mv_ref[...] = split(g2["mv"]).astype(BF16)
    mo_ref[...] = split(g2["mo"]).astype(BF16)
    mg_ref[...] = split(g2["mg"])
    bg_ref[...] = split(proj("bg")).astype(BF16)
    ta = ta_ref[...][None]
    tb = tb_ref[...][None]
    one_col = (lax.broadcasted_iota(jnp.int32, (1, 1, HEAD_PAD), 2) == MLA_V).astype(F32)
    for hh in range(MLA_HEADS):
        sl = slice(hh * HEAD_PAD, (hh + 1) * HEAD_PAD)
        q_ref[:, hh] = (split(qa[:, sl]) * ta + split(qb[:, sl]) * tb).astype(BF16)
        k_ref[:, hh] = (split(kn[:, sl]) + krr).astype(BF16)
        v_ref[:, hh] = (split(vv[:, sl]) + one_col).astype(BF16)


def _residual_specs(X, n_lat, nb, tm):
    ctx_tile = n_lat // tm
    x_lat, x_ctx, cblk = (X[0], X[1], 0) if isinstance(X, tuple) else (X, X, ctx_tile)
    d = x_lat.shape[-1]
    lat_spec = pl.BlockSpec((nb, tm, d), lambda b, i: (b, jnp.minimum(i, ctx_tile - 1), 0))
    ctx_spec = pl.BlockSpec((nb, tm, d), lambda b, i: (b, cblk, 0))
    return x_lat, x_ctx, lat_spec, ctx_spec


def _inproj(X, mod3, g_pre1, w_pad, g_q, g_kv, wqa, wqb, wk, wv, tabs, n_lat):
    tm = ROW_TILE
    nb = PAIR
    x_lat, x_ctx, lat_spec, ctx_spec = _residual_specs(X, n_lat, nb, tm)
    B, d = x_lat.shape[0], x_lat.shape[2]
    T = n_lat + CTX_LEN
    nt = T // tm
    ctx_tile = n_lat // tm

    def mrow(b, i):
        return jnp.where(i >= ctx_tile, B // nb, b)

    full = lambda shape: pl.BlockSpec(shape, lambda b, i: (0,) * len(shape))
    tab = pl.BlockSpec((tm, HEAD_PAD), lambda b, i: (i, 0))
    row_out = lambda w: pl.BlockSpec((nb, tm, w), lambda b, i: (b, i, 0))
    head_out = pl.BlockSpec((nb, MLA_HEADS, tm, HEAD_PAD), lambda b, i: (b, 0, i, 0))
    sd = jax.ShapeDtypeStruct
    outs = pl.pallas_call(
        functools.partial(_inproj_kernel, ctx_tile=ctx_tile),
        grid=(B // nb, nt),
        in_specs=[lat_spec, ctx_spec,
                  pl.BlockSpec((nb, 1, d), lambda b, i: (mrow(b, i), 0, 0)),
                  pl.BlockSpec((nb, 1, d), lambda b, i: (mrow(b, i), 0, 1)),
                  full((1, d)), full((d, D_IN_PAD)), full((1, MLA_Q_RANK)), full((1, MLA_KV_RANK)),
                  full((MLA_Q_RANK, MLA_HEADS * HEAD_PAD)), full((MLA_Q_RANK, MLA_HEADS * HEAD_PAD)),
                  full((MLA_KV_RANK, MLA_HEADS * HEAD_PAD)), full((MLA_KV_RANK, MLA_HEADS * HEAD_PAD)),
                  tab, tab, tab, tab],
        out_specs=[head_out, head_out, head_out,
                   row_out(256), row_out(256), row_out(256), row_out(256), row_out(256),
                   row_out(512), row_out(512), row_out(128), row_out(3 * d)],
        out_shape=[sd((B, MLA_HEADS, T, HEAD_PAD), BF16)] * 3
                  + [sd((B, T, 256), BF16)] * 5 + [sd((B, T, 512), BF16)] * 2
                  + [sd((B, T, 128), F32), sd((B, T, 3 * d), BF16)],
        compiler_params=_cparams(("parallel", "parallel"), 56),
        name="inproj",
    )(x_lat, x_ctx, mod3, mod3, g_pre1, w_pad, g_q, g_kv, wqa, wqb, wk, wv, *tabs)
    return outs


def _flash_kernel(q_ref, k_ref, v_ref, o_ref, s0_ref, s1_ref, p0_ref, p1_ref, *, nk, rb):
    tq = q_ref.shape[2]
    s_bufs, p_bufs = (s0_ref, s1_ref), (p0_ref, p1_ref)

    def scores(hh):
        s_bufs[hh % 2][...] = _dot_nt(q_ref[0, hh], k_ref[0, hh, :nk, :])

    scores(0)
    for hh in range(MLA_HEADS):
        if hh + 1 < MLA_HEADS:
            scores(hh + 1)
        s_ref, p_ref = s_bufs[hh % 2], p_bufs[hh % 2]
        for r in range(0, tq, rb):
            s = s_ref[r:r + rb, :]
            m = jnp.max(s, axis=1, keepdims=True)
            p_ref[r:r + rb, :] = jnp.exp2(s - m).astype(BF16)
        acc = _dot(p_ref[...], v_ref[0, hh, :nk, :])
        o = acc[:, :MLA_V] / acc[:, MLA_V:MLA_V + 1]
        o_ref[0, :, hh * MLA_V:(hh + 1) * MLA_V] = o.astype(BF16)


def _mla_attention(Q, K, V, n_lat, context):
    B, H, T, E = Q.shape
    kc = 256
    if not context:
        tq, nk, nq = 512, T, n_lat
        q_spec = pl.BlockSpec((1, H, tq, E), lambda b, i: (b, 0, i, 0))
        kv_spec = pl.BlockSpec((1, H, T, E), lambda b, i: (b, 0, 0, 0))
    else:
        tq, nk, nq = CTX_LEN, CTX_LEN, CTX_LEN
        cblk = n_lat // CTX_LEN
        q_spec = pl.BlockSpec((1, H, tq, E), lambda b, i: (b, 0, cblk, 0))
        kv_spec = pl.BlockSpec((1, H, nk, E), lambda b, i: (b, 0, cblk, 0))
    return pl.pallas_call(
        functools.partial(_flash_kernel, nk=nk, rb=64),
        grid=(B, nq // tq),
        in_specs=[q_spec, kv_spec, kv_spec],
        out_specs=pl.BlockSpec((1, tq, H * MLA_V), lambda b, i: (b, i, 0)),
        out_shape=jax.ShapeDtypeStruct((B, nq, H * MLA_V), BF16),
        scratch_shapes=[pltpu.VMEM((tq, nk), F32), pltpu.VMEM((tq, nk), F32),
                        pltpu.VMEM((tq, nk), BF16), pltpu.VMEM((tq, nk), BF16)],
        compiler_params=_cparams(("parallel", "parallel"), 56),
        name="mla_attention_ctx" if context else "mla_attention",
    )(Q, K, V)


def _na_kernel(q_ref, k_ref, v_ref, bias_ref, o_ref, *, n_lat, rows, with_ctx):
    j = pl.program_id(1)
    nblk = rows // NA_QROWS
    scale = NA_DIM ** -0.5
    assert np.log2(scale) == round(np.log2(scale))
    q = (q_ref[0].astype(F32) * scale).astype(BF16)
    kctx = k_ref[0, n_lat:n_lat + CTX_LEN, :]
    vctx = v_ref[0, n_lat:n_lat + CTX_LEN, :]

    def local_step():
        kb = jnp.clip(j * NA_QROWS - NA_KR // 2, 0, rows - NA_KROWS)
        start = pl.multiple_of(kb * GRID_W, GRID_W)
        kloc = k_ref[0, pl.ds(start, NA_KROWS * GRID_W), :]
        vloc = v_ref[0, pl.ds(start, NA_KROWS * GRID_W), :]
        for hh in range(NA_HEADS):
            sl = slice(hh * NA_DIM, (hh + 1) * NA_DIM)
            qh = q[:, sl]
            s_loc = _dot_nt(qh, kloc[:, sl]) + bias_ref[0, hh]
            s_ctx = _dot_nt(qh, kctx[:, sl])
            m = jnp.maximum(jnp.max(s_loc, axis=1, keepdims=True), jnp.max(s_ctx, axis=1, keepdims=True))
            p_loc = jnp.exp(s_loc - m)
            p_ctx = jnp.exp(s_ctx - m)
            den = jnp.sum(p_loc, axis=1, keepdims=True) + jnp.sum(p_ctx, axis=1, keepdims=True)
            o = _dot(p_loc.astype(BF16), vloc[:, sl]) + _dot(p_ctx.astype(BF16), vctx[:, sl])
            o_ref[0, :, sl] = (o / den).astype(BF16)

    def ctx_step():
        for hh in range(NA_HEADS):
            sl = slice(hh * NA_DIM, (hh + 1) * NA_DIM)
            s = _dot_nt(q[:, sl], kctx[:, sl])
            p = jnp.exp(s - jnp.max(s, axis=1, keepdims=True))
            den = jnp.sum(p, axis=1, keepdims=True)
            o_ref[0, :, sl] = (_dot(p.astype(BF16), vctx[:, sl]) / den).astype(BF16)

    if with_ctx:
        pl.when(j < nblk)(local_step)
        pl.when(j == nblk)(ctx_step)
    else:
        local_step()


def _na_bias_table(rpb, rows):
    W = GRID_W
    nr, ncol = 2 * NA_KR - 1, 2 * NA_KC - 1
    w = np.arange(W)[:, None]
    c = np.arange(W)[None, :]
    cs = np.clip(w - NA_KC // 2, 0, W - NA_KC)
    col_ok = (c >= cs) & (c < cs + NA_KC)
    sel_c = (col_ok[:, :, None] & ((c - w + NA_KC - 1)[:, :, None] == np.arange(ncol))).astype(np.float32)
    sel_r = np.zeros((3, NA_QROWS, NA_KROWS, nr), np.float32)
    for p, r0 in enumerate((0, NA_QROWS, rows - NA_QROWS)):
        kb = int(np.clip(r0 - NA_KR // 2, 0, rows - NA_KROWS))
        for q in range(NA_QROWS):
            rs = int(np.clip(r0 + q - NA_KR // 2, 0, rows - NA_KR))
            for k in range(NA_KROWS):
                if rs <= kb + k < rs + NA_KR:
                    sel_r[p, q, k, kb + k - (r0 + q) + NA_KR - 1] = 1.0
    hi = lax.Precision.HIGHEST
    by_col = jnp.einsum("hrd,wcd->hrwc", rpb.astype(F32), jnp.asarray(sel_c), precision=hi)
    bias = jnp.einsum("pqkr,hrwc->phqwkc", jnp.asarray(sel_r), by_col, precision=hi)
    valid = np.einsum("pqk,wc->pqwkc", sel_r.sum(-1), col_ok.astype(np.float32)) > 0
    bias = jnp.where(jnp.asarray(valid)[:, None], bias, NEG_BIG)
    return bias.reshape(3, NA_HEADS, NA_QROWS * W, NA_KROWS * W)


def _neighbourhood_attention(nq, nk, nv, bias_tab, n_lat, with_ctx):
    B, T, w = nq.shape
    rows = n_lat // GRID_W
    nblk = rows // NA_QROWS
    tq = NA_QROWS * GRID_W
    steps = nblk + (1 if with_ctx else 0)

    def pat(j):
        return jnp.where(j == 0, 0, jnp.where(j >= nblk - 1, 2, 1))

    return pl.pallas_call(
        functools.partial(_na_kernel, n_lat=n_lat, rows=rows, with_ctx=with_ctx),
        grid=(B, steps),
        in_specs=[pl.BlockSpec((1, tq, w), lambda b, j: (b, j, 0)),
                  pl.BlockSpec((1, T, w), lambda b, j: (b, 0, 0)),
                  pl.BlockSpec((1, T, w), lambda b, j: (b, 0, 0)),
                  pl.BlockSpec((1, NA_HEADS, tq, NA_KROWS * GRID_W), lambda b, j: (pat(j), 0, 0, 0))],
        out_specs=pl.BlockSpec((1, tq, w), lambda b, j: (b, j, 0)),
        out_shape=jax.ShapeDtypeStruct((B, T, w), BF16),
        compiler_params=_cparams(("parallel", "arbitrary"), 48),
        name="neighbourhood_attention",
    )(nq, nk, nv, bias_tab)


def _mlstm_kernel(qf_ref, kf_ref, vf_ref, gf_ref, gtf_ref, qb_ref, kb_ref, vb_ref, gb_ref, gtb_ref,
                  bcol_ref, brow_ref, hf_ref, hb_ref, st_ref, m_ref):
    L = ML_CHUNK

    @pl.when(pl.program_id(1) == 0)
    def _():
        st_ref[...] = jnp.zeros_like(st_ref)
        m_ref[...] = jnp.zeros_like(m_ref)

    r_i = lax.broadcasted_iota(jnp.int32, (L, L), 0)
    c_i = lax.broadcasted_iota(jnp.int32, (L, L), 1)
    ones_t = jnp.ones((ML_V, L), BF16)
    dirs = ((qf_ref, kf_ref, vf_ref, gf_ref, gtf_ref, hf_ref), (qb_ref, kb_ref, vb_ref, gb_ref, gtb_ref, hb_ref))
    nb = qf_ref.shape[0]
    ng = 4 * ML_HEADS
    for bi, d in [(bi, d) for d in range(2) for bi in range(nb)]:
        q_ref, k_ref, v_ref, g_ref, gt_ref, h_ref = dirs[d]
        if bi == 0:
            causal_t = (r_i <= c_i) if d == 0 else (r_i >= c_i)
            m_row = jnp.where(causal_t, 1.0, 0.0).astype(BF16)
            m_col = jnp.where((c_i <= r_i) if d == 0 else (c_i >= r_i), 1.0, 0.0).astype(BF16)
            gt_all = jnp.concatenate([gt_ref[i] + bcol_ref[...] for i in range(nb)], axis=0)
            lf_r = _log_sigmoid(gt_all)
            br_all = sum(_dot(p, m_row) for p in _split3(lf_r))
            tot_all = jnp.sum(lf_r, axis=1, keepdims=True)
            gc_all = jnp.concatenate([g_ref[i] + brow_ref[...] for i in range(nb)], axis=1)
            bc_all = sum(_dot(m_col, p) for p in _split3(_log_sigmoid(gc_all)))
        gt = gt_all[bi * ng:(bi + 1) * ng, :]
        br = br_all[bi * ng:(bi + 1) * ng, :]
        tot = tot_all[bi * ng:(bi + 1) * ng, :]
        gc = gc_all[:, bi * LANES:(bi + 1) * LANES]
        bc = bc_all[:, bi * LANES:(bi + 1) * LANES]
        k16 = (k_ref[bi].astype(F32) * (ML_QK ** -0.5)).astype(BF16)
        q16 = q_ref[bi]
        q_t = q16.astype(F32).T
        v_t = v_ref[bi].astype(F32).T.astype(BF16)
        for hh in range(ML_HEADS):
            c = (bi * 2 + d) * ML_HEADS + hh
            ji = (2 * d) * ML_HEADS + hh
            jf = (2 * d + 1) * ML_HEADS + hh
            i_row = gt[ji:ji + 1, :]
            b_row = br[jf:jf + 1, :]
            b_last = tot[jf:jf + 1, :]
            src_col = gc[:, ji:ji + 1] - bc[:, jf:jf + 1]
            m_old = m_ref[c:c + 1, 0:1]
            st_old = st_ref[c]
            qh = q16[:, hh * ML_QK:(hh + 1) * ML_QK]
            kh = k16[:, hh * ML_QK:(hh + 1) * ML_QK]
            qh_t = q_t[hh * ML_QK:(hh + 1) * ML_QK, :]
            va_t = jnp.concatenate([v_t[hh * ML_V:(hh + 1) * ML_V, :], ones_t], axis=0)
            d_log = jnp.where(causal_t, src_col + b_row, -jnp.inf)
            inter = b_row + m_old
            m_t = jnp.maximum(inter, jnp.max(d_log, axis=0, keepdims=True))
            a = jnp.exp(inter - m_t)
            smat_t = (_dot_nt(kh, qh) * jnp.exp(d_log - m_t)).astype(BF16)
            lhs = jnp.concatenate([va_t, st_old.astype(BF16)], axis=1)
            rhs = jnp.concatenate([smat_t, (qh_t * a).astype(BF16)], axis=0)
            out_t = _dot(lhs, rhs)
            den = jnp.maximum(jnp.abs(out_t[ML_V:, :]), jnp.exp(-m_t))
            h_ref[bi, :, hh * ML_V:(hh + 1) * ML_V] = (out_t[:ML_V, :] / den).T
            w_log = b_last - b_row + i_row
            m_new = jnp.maximum(b_last + m_old, jnp.max(w_log, axis=1, keepdims=True))
            decay = jnp.exp(b_last + m_old - m_new)
            vw_t = (va_t.astype(F32) * jnp.exp(w_log - m_new)).astype(BF16)
            st_ref[c] = decay * st_old + _dot(vw_t, kh)
            m_ref[c:c + 1, :] = jnp.broadcast_to(m_new, (1, LANES))


def _mlstm(mq, mk, mv, mg, mgt, b_ml, n_lat):
    B, T, _ = mq.shape
    L = ML_CHUNK
    nl = n_lat // L
    nc = CTX_LEN // L
    steps = nl + nc

    def cf(s):
        return jnp.where(s < nc, nl + s, s - nc)

    def cb(s):
        return steps - 1 - s

    nb = next(n for n in (4 * PAIR, 2 * PAIR, PAIR) if B % n == 0)

    def specs(cm):
        return [pl.BlockSpec((nb, L, ML_HEADS * ML_QK), lambda b, s: (b, cm(s), 0)),
                pl.BlockSpec((nb, L, ML_HEADS * ML_QK), lambda b, s: (b, cm(s), 0)),
                pl.BlockSpec((nb, L, ML_HEADS * ML_V), lambda b, s: (b, cm(s), 0)),
                pl.BlockSpec((nb, L, LANES), lambda b, s: (b, cm(s), 0)),
                pl.BlockSpec((nb, 4 * ML_HEADS, L), lambda b, s: (b, 0, cm(s)))]

    b_row = jnp.concatenate([b_ml.astype(F32), jnp.zeros((LANES - 4 * ML_HEADS,), F32)])[None, :]
    b_col = b_ml.astype(F32)[:, None]
    sd = jax.ShapeDtypeStruct
    n_chain = nb * 2 * ML_HEADS
    return pl.pallas_call(
        _mlstm_kernel,
        grid=(B // nb, steps),
        in_specs=specs(cf) + specs(cb) + [pl.BlockSpec((4 * ML_HEADS, 1), lambda b, s: (0, 0)),
                                          pl.BlockSpec((1, LANES), lambda b, s: (0, 0))],
        out_specs=[pl.BlockSpec((nb, L, ML_HEADS * ML_V), lambda b, s: (b, cf(s), 0)),
                   pl.BlockSpec((nb, L, ML_HEADS * ML_V), lambda b, s: (b, cb(s), 0))],
        out_shape=[sd((B, T, ML_HEADS * ML_V), F32)] * 2,
        scratch_shapes=[pltpu.VMEM((n_chain, 2 * ML_V, ML_QK), F32), pltpu.VMEM((n_chain, LANES), F32)],
        compiler_params=_cparams(("parallel", "arbitrary"), 32),
        name="mlstm",
    )(mq, mk, mv, mg, mgt, mq, mk, mv, mg, mgt, b_col, b_row)


def _merge_kernel(a_ref, bn_ref, hf_ref, hb_ref, mo_ref, bg_ref, x_ref, xc_ref, gt1_ref, sh2_ref, sc2_ref,
                  gml_ref, gpost_ref, gpre2_ref, wa_ref, wb_ref, wm_ref, wo_ref, wrh_ref, wrl_ref,
                  xo_ref, h2_ref, aff_ref, *, ctx_tile):
    nb, tm, d = x_ref.shape
    gml = gml_ref[...]
    is_ctx = pl.program_id(1) >= ctx_tile
    for bi in range(nb):
        hm = hf_ref[bi] + hb_ref[bi]
        parts = []
        for hh in range(ML_HEADS):
            sl = slice(hh * ML_V, (hh + 1) * ML_V)
            parts.append(_rms(hm[:, sl], gml[:, sl]))
        hn = jnp.concatenate(parts, axis=1)
        mlo = (hn + hn * jnp.tanh(mo_ref[bi].astype(F32))).astype(BF16)
        ya = _dot(a_ref[bi], wa_ref[...])
        yb = _dot(bn_ref[bi], wb_ref[...])
        ym = _dot(mlo, wm_ref[...])
        y = ((ya + yb + ym)
             + ya * jnp.tanh(bg_ref[bi, :, 0:d].astype(F32))
             + yb * jnp.tanh(bg_ref[bi, :, d:2 * d].astype(F32))
             + ym * jnp.tanh(bg_ref[bi, :, 2 * d:3 * d].astype(F32)))
        out = _dot(y.astype(BF16), wo_ref[...])
        x_in = jnp.where(is_ctx, xc_ref[bi], x_ref[bi])
        xn = x_in + _rms(out, gpost_ref[...] * gt1_ref[bi])
        xo_ref[bi] = xn
        h2 = _rms(xn, gpre2_ref[...] * (1.0 + sc2_ref[bi])) + sh2_ref[bi]
        h2_ref[bi] = h2.astype(BF16)
        hi = h2.astype(BF16)
        lo = (h2 - hi.astype(F32)).astype(BF16)
        logits = _dot(hi, wrh_ref[...]) + _dot(lo, wrh_ref[...]) + _dot(hi, wrl_ref[...])
        lane = lax.broadcasted_iota(jnp.int32, logits.shape, 1)
        logits = jnp.where(lane < N_EXPERTS, logits, -jnp.inf)
        e = jnp.exp(logits - jnp.max(logits, axis=1, keepdims=True))
        aff_ref[bi] = e / jnp.sum(e, axis=1, keepdims=True)


def _merge(a, bn, hf, hb, mo, bg, X, mod3, g_ml, g_post1, g_pre2, wa, wb, wm, wo, wrh, wrl, n_lat, with_ctx):
    tm = ROW_TILE
    nb = PAIR
    x_lat, x_ctx, lat_spec, ctx_spec = _residual_specs(X, n_lat, nb, tm)
    B, d = x_lat.shape[0], x_lat.shape[2]
    T = n_lat + CTX_LEN
    ctx_tile = n_lat // tm
    nt = ctx_tile + (1 if with_ctx else 0)

    def mrow(b, i):
        return jnp.where(i >= ctx_tile, B // nb, b)

    full = lambda shape: pl.BlockSpec(shape, lambda b, i: (0,) * len(shape))
    row = lambda w: pl.BlockSpec((nb, tm, w), lambda b, i: (b, i, 0))
    modc = lambda ch: pl.BlockSpec((nb, 1, d), lambda b, i: (mrow(b, i), 0, ch))
    sd = jax.ShapeDtypeStruct
    return pl.pallas_call(
        functools.partial(_merge_kernel, ctx_tile=ctx_tile),
        grid=(B // nb, nt),
        in_specs=[row(256), row(256), row(512), row(512), row(512), row(3 * d), lat_spec, ctx_spec,
                  modc(2), modc(3), modc(4),
                  full((1, ML_HEADS * ML_V)), full((1, d)), full((1, d)),
                  full((256, d)), full((256, d)), full((512, d)), full((d, d)), full((d, LANES)), full((d, LANES))],
        out_specs=[row(d), row(d), row(LANES)],
        out_shape=[sd((B, T, d), F32), sd((B, T, d), BF16), sd((B, T, LANES), F32)],
        compiler_params=_cparams(("parallel", "parallel"), 48),
        name="merge",
    )(a, bn, hf, hb, mo, bg, x_lat, x_ctx, mod3, mod3, mod3, g_ml, g_post1, g_pre2, wa, wb, wm, wo, wrh, wrl)


def _select_kernel(aff_ref, slot_ref, start_ref, *, cap):
    aff = aff_ref[0]
    n = aff.shape[1]
    bits = pltpu.bitcast(aff, jnp.int32)
    capf = jnp.float32(cap)

    def count(mask):
        return jnp.sum(jnp.where(mask, 1.0, 0.0), axis=1, keepdims=True)

    def refine(it, thr):
        cand = thr | jnp.left_shift(jnp.int32(1), 30 - it)
        return jnp.where(count(bits >= cand) >= capf, cand, thr)

    thr = lax.fori_loop(0, 31, refine, jnp.zeros((N_EXPERTS, 1), jnp.int32))
    gt = bits > thr
    eq = bits == thr
    need = capf - count(gt)
    r_i = lax.broadcasted_iota(jnp.int32, (LANES, LANES), 0)
    c_i = lax.broadcasted_iota(jnp.int32, (LANES, LANES), 1)
    before = jnp.where(r_i < c_i, 1.0, 0.0).astype(BF16)

    def excl_prefix(mask):
        carry = jnp.zeros((N_EXPERTS, 1), F32)
        outs, carries = [], []
        for c in range(n // LANES):
            carries.append(carry)
            mc = jnp.where(mask[:, c * LANES:(c + 1) * LANES], 1.0, 0.0)
            outs.append(_dot(mc.astype(BF16), before) + carry)
            carry = carry + jnp.sum(mc, axis=1, keepdims=True)
        return jnp.concatenate(outs, axis=1), carries + [carry]

    sel = gt | (eq & (excl_prefix(eq)[0] < need))
    rank, carries = excl_prefix(sel)
    slot_ref[0] = jnp.where(sel, rank, -1.0)
    lane = lax.broadcasted_iota(jnp.int32, (N_EXPERTS, LANES), 1)
    per_tile = TOK_TILE // LANES
    starts = jnp.zeros((N_EXPERTS, LANES), F32)
    for k in range(n // TOK_TILE + 1):
        starts = jnp.where(lane == k, carries[k * per_tile], starts)
    start_ref[0] = starts


def _select(aff_t, cap):
    B, E, n = aff_t.shape
    slot, starts = pl.pallas_call(
        functools.partial(_select_kernel, cap=cap),
        grid=(B,),
        in_specs=[pl.BlockSpec((1, E, n), lambda b: (b, 0, 0))],
        out_specs=[pl.BlockSpec((1, E, n), lambda b: (b, 0, 0)), pl.BlockSpec((1, E, LANES), lambda b: (b, 0, 0))],
        out_shape=[jax.ShapeDtypeStruct((B, E, n), F32), jax.ShapeDtypeStruct((B, E, LANES), F32)],
        compiler_params=_cparams(("parallel",), 32),
        name="route_select",
    )(aff_t)
    return slot, starts[:, :, :n // TOK_TILE + 1].astype(jnp.int32)


def _log2(v):
    assert v & (v - 1) == 0
    return v.bit_length() - 1


def _window_plan(start_ref, b, kt, cap, win):
    firsts, rounds = [], jnp.int32(0)
    for e in range(N_EXPERTS):
        lo = start_ref[b, e, kt]
        hi = start_ref[b, e, kt + 1]
        first = jnp.minimum(lax.shift_right_logical(lo, _log2(BF16_ROWS)) * BF16_ROWS, cap - win)
        firsts.append(first)
        rounds = jnp.maximum(rounds, lax.shift_right_logical(hi - first + (win - 1), _log2(win)))
    return firsts, rounds


def _dispatch_kernel(start_ref, slot_ref, h_ref, xs_ref, *, cap, win):
    b, kt = pl.program_id(0), pl.program_id(1)

    @pl.when(kt == 0)
    def _():
        xs_ref[...] = jnp.zeros_like(xs_ref)

    slot = slot_ref[0]
    tile = slot.shape[1]
    h = h_ref[0]
    firsts, rounds = _window_plan(start_ref, b, kt, cap, win)
    sub = lax.broadcasted_iota(jnp.int32, (win, tile), 0)

    def one_round(r, carry):
        rows, pieces = [], []
        for e in range(N_EXPERTS):
            base = firsts[e] + r * win
            row0 = pl.multiple_of(jnp.minimum(base, cap - win), BF16_ROWS)
            abs_slot = row0 + sub
            hit = (abs_slot.astype(F32) == slot[e:e + 1, :]) & (abs_slot >= base)
            pieces.append(jnp.where(hit, 1.0, 0.0).astype(BF16))
            rows.append(row0)
        res = _dot(jnp.concatenate(pieces, axis=0), h)
        for e in range(N_EXPERTS):
            cur = xs_ref[0, e, pl.ds(rows[e], win), :]
            xs_ref[0, e, pl.ds(rows[e], win), :] = cur + res[e * win:(e + 1) * win].astype(BF16)
        return carry

    lax.fori_loop(0, rounds, one_round, 0)


def _dispatch(starts, slot, h2, cap, n, tok_off):
    B, E, _ = slot.shape
    d = h2.shape[2]
    blk0 = tok_off // TOK_TILE
    grid_spec = pltpu.PrefetchScalarGridSpec(
        num_scalar_prefetch=1,
        grid=(B, n // TOK_TILE),
        in_specs=[pl.BlockSpec((1, E, TOK_TILE), lambda b, k, st: (b, 0, k)),
                  pl.BlockSpec((1, TOK_TILE, d), lambda b, k, st: (b, blk0 + k, 0))],
        out_specs=pl.BlockSpec((1, E, cap, d), lambda b, k, st: (b, 0, 0, 0)))
    return pl.pallas_call(
        functools.partial(_dispatch_kernel, cap=cap, win=min(SLOT_WIN, cap)),
        grid_spec=grid_spec,
        out_shape=jax.ShapeDtypeStruct((B, E, cap, d), BF16),
        compiler_params=_cparams(("parallel", "arbitrary"), 56),
        name="moe_dispatch",
    )(starts, slot, h2)


def _ffn_kernel(*refs, with_ctx):
    if with_ctx:
        xs_ref, xc_ref, w1_ref, w3_ref, w2_ref, y_ref, yc_ref, w1b, w3b, w2b = refs
    else:
        xs_ref, w1_ref, w3_ref, w2_ref, y_ref, w1b, w3b, w2b = refs

    @pl.when(pl.program_id(1) == 0)
    def _():
        step = 256
        for src, dst in ((w1_ref, w1b), (w3_ref, w3b), (w2_ref, w2b)):
            for r in range(0, src.shape[2], step):
                dst[r:r + step, :] = src[0, 0, r:r + step, :].astype(BF16)

    nb, _, cap, _ = xs_ref.shape
    parts = [xs_ref[i, 0] for i in range(nb)]
    if with_ctx:
        parts += [xc_ref[i, 0] for i in range(nb)]
    x = jnp.concatenate(parts, axis=0)
    h1 = _dot(x, w1b[...])
    h3 = _dot(x, w3b[...])
    hid = (h1 * jax.nn.sigmoid(h1) * h3).astype(BF16)
    y = _dot(hid, w2b[...]).astype(BF16)
    for i in range(nb):
        y_ref[i, 0] = y[i * cap:(i + 1) * cap]
    if with_ctx:
        cap_c = xc_ref.shape[2]
        for i in range(nb):
            yc_ref[i, 0] = y[nb * cap + i * cap_c:nb * cap + (i + 1) * cap_c]


def _expert_ffn(xs, xs_ctx, w1, w3, w2, layer):
    B, E, cap, d = xs.shape
    f = w1.shape[3]
    slots = lambda c: pl.BlockSpec((PAIR, 1, c, d), lambda e, b: (b, e, 0, 0))
    x_in, x_specs = [xs], [slots(cap)]
    out_specs, out_shape = [slots(cap)], [jax.ShapeDtypeStruct((B, E, cap, d), BF16)]
    if xs_ctx is not None:
        cap_c = xs_ctx.shape[2]
        x_in.append(xs_ctx)
        x_specs.append(slots(cap_c))
        out_specs.append(slots(cap_c))
        out_shape.append(jax.ShapeDtypeStruct((B, E, cap_c, d), BF16))
    outs = pl.pallas_call(
        functools.partial(_ffn_kernel, with_ctx=xs_ctx is not None),
        grid=(E, B // PAIR),
        in_specs=x_specs + [pl.BlockSpec((1, 1, d, f), lambda e, b: (layer, e, 0, 0)),
                            pl.BlockSpec((1, 1, d, f), lambda e, b: (layer, e, 0, 0)),
                            pl.BlockSpec((1, 1, f, d), lambda e, b: (layer, e, 0, 0))],
        out_specs=out_specs,
        out_shape=out_shape,
        scratch_shapes=[pltpu.VMEM((d, f), BF16), pltpu.VMEM((d, f), BF16), pltpu.VMEM((f, d), BF16)],
        compiler_params=_cparams(("arbitrary", "arbitrary"), 56),
        name="moe_expert_ffn",
    )(*x_in, w1, w3, w2)
    return outs if xs_ctx is not None else (outs[0], None)


def _combine_kernel(start_ref, slotc_ref, aff_ref, y_ref, x_ref, gt2_ref, gpost_ref, xo_ref, acc_ref, *, cap, win):
    b, kt = pl.program_id(0), pl.program_id(1)
    slotc = slotc_ref[0]
    aff = aff_ref[0]
    tile = slotc.shape[0]
    firsts, rounds = _window_plan(start_ref, b, kt, cap, win)
    width = N_EXPERTS * win
    col = lax.broadcasted_iota(jnp.int32, (1, width), 1)
    owner = lax.shift_right_logical(col, _log2(win))
    off = col - owner * win
    spread = jnp.where(lax.broadcasted_iota(jnp.int32, (LANES, width), 0) == owner, 1.0, 0.0).astype(BF16)
    half = jnp.floor(slotc * 0.5)
    g_hi = aff.astype(BF16)
    g_lo = (aff - g_hi.astype(F32)).astype(BF16)
    stacked = jnp.concatenate([half.astype(BF16), (slotc - 2.0 * half).astype(BF16), g_hi, g_lo], axis=0)
    rep = _dot(stacked, spread)
    scol = 2.0 * rep[:tile] + rep[tile:2 * tile]
    ghi_rep = rep[2 * tile:3 * tile]
    glo_rep = rep[3 * tile:]
    acc_ref[...] = jnp.zeros_like(acc_ref)

    def one_round(r, carry):
        row0v = jnp.zeros((1, width), jnp.int32)
        basev = jnp.zeros((1, width), jnp.int32)
        rhs_p = []
        for e in range(N_EXPERTS):
            base = firsts[e] + r * win
            row0 = pl.multiple_of(jnp.minimum(base, cap - win), BF16_ROWS)
            row0v = jnp.where(owner == e, row0, row0v)
            basev = jnp.where(owner == e, base, basev)
            rhs_p.append(y_ref[0, e, pl.ds(row0, win), :])
        abs_slot = row0v + off
        hit = (abs_slot.astype(F32) == scol) & (abs_slot >= basev)
        lhs = jnp.concatenate([jnp.where(hit, ghi_rep, 0.0).astype(BF16),
                               jnp.where(hit, glo_rep, 0.0).astype(BF16)], axis=0)
        res = _dot(lhs, jnp.concatenate(rhs_p, axis=0))
        acc_ref[...] += res[:tile] + res[tile:]
        return carry

    lax.fori_loop(0, rounds, one_round, 0)
    xo_ref[0] = x_ref[0] + _rms(acc_ref[...], gpost_ref[...] * gt2_ref[0])


def _combine(starts, slotc, aff, y, X, mod3, g_post2, n, tok_off, is_ctx, in_place):
    B, T, d = X.shape
    E, cap = y.shape[1], y.shape[2]
    tt = TOK_TILE
    blk0 = tok_off // tt
    mrow = (lambda b: B) if is_ctx else (lambda b: b)
    if in_place:
        out_spec = pl.BlockSpec((1, tt, d), lambda b, t, st: (b, blk0 + t, 0))
        out_shape, aliases = jax.ShapeDtypeStruct((B, T, d), F32), {4: 0}
    else:
        out_spec = pl.BlockSpec((1, tt, d), lambda b, t, st: (b, t, 0))
        out_shape, aliases = jax.ShapeDtypeStruct((B, n, d), F32), {}
    grid_spec = pltpu.PrefetchScalarGridSpec(
        num_scalar_prefetch=1,
        grid=(B, n // tt),
        in_specs=[pl.BlockSpec((1, tt, LANES), lambda b, t, st: (b, t, 0)),
                  pl.BlockSpec((1, tt, LANES), lambda b, t, st: (b, blk0 + t, 0)),
                  pl.BlockSpec((1, E, cap, d), lambda b, t, st: (b, 0, 0, 0)),
                  pl.BlockSpec((1, tt, d), lambda b, t, st: (b, blk0 + t, 0)),
                  pl.BlockSpec((1, 1, d), lambda b, t, st: (mrow(b), 0, 5)),
                  pl.BlockSpec((1, d), lambda b, t, st: (0, 0))],
        out_specs=out_spec,
        scratch_shapes=[pltpu.VMEM((tt, d), F32)])
    return pl.pallas_call(
        functools.partial(_combine_kernel, cap=cap, win=min(SLOT_WIN, cap)),
        grid_spec=grid_spec,
        out_shape=out_shape,
        input_output_aliases=aliases,
        compiler_params=_cparams(("parallel", "arbitrary"), 56),
        name="moe_combine",
    )(starts, slotc, aff, y, X, mod3, g_post2)


def _moe(X, h2, aff, mod3, g_post2, w1, w3, w2, layer, n_lat, with_ctx, in_place):
    def route(n, off):
        cap = EC_FACTOR * n // N_EXPERTS
        aff_t = jnp.transpose(aff[:, off:off + n, :N_EXPERTS], (0, 2, 1))
        slot, starts = _select(aff_t, cap)
        slotc = _pad_cols(jnp.transpose(slot, (0, 2, 1)), 0, LANES)
        return slotc, starts, _dispatch(starts, slot, h2, cap, n, off)

    slotc, starts, xs = route(n_lat, 0)
    slotc_c, starts_c, xs_c = route(CTX_LEN, n_lat) if with_ctx else (None, None, None)
    y, y_c = _expert_ffn(xs, xs_c, w1, w3, w2, layer)
    X = _combine(starts, slotc, aff, y, X, mod3, g_post2, n_lat, 0, False, in_place)
    if with_ctx:
        X = _combine(starts_c, slotc_c, aff, y_c, X, mod3, g_post2, CTX_LEN, n_lat, True, True)
    return X


def _rope_partner(w):
    lead = w.shape[:-1]
    ws = w.reshape(lead + (2, 2, MLA_ROPE // 4))
    return jnp.stack([-ws[..., 1, :], ws[..., 0, :]], axis=-2).reshape(lead + (MLA_ROPE,))


def _pad_cols(w, left, total):
    return jnp.pad(w, [(0, 0)] * (w.ndim - 1) + [(left, total - left - w.shape[-1])])


def _prep_in_weights(w_in):
    sizes = (MLA_Q_RANK, MLA_KV_RANK, MLA_ROPE, 256, 256, 256, 256, 256, 512, 512, 4 * ML_HEADS, 3 * D_MODEL)
    offs = np.cumsum(sizes)[:-1].tolist()
    cq, ckv, kr, nq, nk, nv, mq, mk, mv, mo, mg, bg = jnp.split(w_in, offs, axis=-1)
    kr1 = _pad_cols(kr, MLA_NOPE, HEAD_PAD)
    kr2 = _pad_cols(_rope_partner(kr), MLA_NOPE, HEAD_PAD)
    mgp = _pad_cols(mg, 0, LANES)
    return jnp.concatenate([cq, ckv, kr1, kr2, nq, nk, nv, mq, mk, mv, 0.5 * mo, mgp, 0.5 * bg], axis=-1).astype(BF16)


def _prep_mla_weights(w_uq, w_ukv):
    nope, rope = w_uq[..., :MLA_NOPE], w_uq[..., MLA_NOPE:]
    wqa = _pad_cols(jnp.concatenate([nope, rope], axis=-1), 0, HEAD_PAD)
    wqb = _pad_cols(_rope_partner(rope), MLA_NOPE, HEAD_PAD)
    wk = _pad_cols(w_ukv[..., :MLA_NOPE], 0, HEAD_PAD)
    wv = _pad_cols(w_ukv[..., MLA_NOPE:], 0, HEAD_PAD)
    flat = lambda w: w.reshape(w.shape[0], MLA_HEADS * HEAD_PAD).astype(BF16)
    return flat(wqa), flat(wqb), flat(wk), flat(wv)


def _rope_tables(n_lat):
    t = np.arange(n_lat)
    pos = np.stack([t // GRID_W, t % GRID_W], axis=-1).astype(np.float64)
    nf = MLA_ROPE // 4
    inv = np.float32(ROPE_BASE) ** (-np.arange(nf, dtype=np.float32) / nf)
    ang = (pos[..., None].astype(np.float32) * inv).astype(np.float64)
    cos = np.repeat(np.cos(ang)[:, :, None, :], 2, axis=2).reshape(n_lat, MLA_ROPE)
    sin = np.repeat(np.sin(ang)[:, :, None, :], 2, axis=2).reshape(n_lat, MLA_ROPE)
    cos = np.concatenate([cos, np.ones((CTX_LEN, MLA_ROPE))], axis=0)
    sin = np.concatenate([sin, np.zeros((CTX_LEN, MLA_ROPE))], axis=0)
    T = n_lat + CTX_LEN
    z = lambda w: np.zeros((T, w))
    scale = (MLA_NOPE + MLA_ROPE) ** -0.5 * np.log2(np.e)
    ta = scale * np.concatenate([np.ones((T, MLA_NOPE)), cos, z(HEAD_PAD - MLA_NOPE - MLA_ROPE)], axis=1)
    tb = scale * np.concatenate([z(MLA_NOPE), sin, z(HEAD_PAD - MLA_NOPE - MLA_ROPE)], axis=1)
    tc = np.concatenate([z(MLA_NOPE), cos, z(HEAD_PAD - MLA_NOPE - MLA_ROPE)], axis=1)
    ts = np.concatenate([z(MLA_NOPE), sin, z(HEAD_PAD - MLA_NOPE - MLA_ROPE)], axis=1)
    return tuple(jnp.asarray(a, F32) for a in (ta, tb, tc, ts))


def kernel(x, c, ctx, c_ctx, w_mod, b_mod, g_pre1, g_post1, g_pre2, g_post2, w_in, g_q, g_kv, w_uq, w_ukv, rpb,
           b_ml_gates, g_ml, w_br_mla, w_br_na, w_br_ml, w_out, w_router, w_e1, w_e3, w_e2):
    B, n_lat, d = x.shape
    depth = w_mod.shape[0]
    rows = n_lat // GRID_W
    assert d == D_MODEL and ctx.shape[1] == CTX_LEN and n_lat % 512 == 0 and rows >= NA_KROWS + NA_QROWS and B % PAIR == 0

    X = (x, ctx)
    mod_rows = -(-(B + PAIR) // 8) * 8
    c_all = jnp.concatenate([c, jnp.broadcast_to(c_ctx[None, :], (PAIR, d)),
                             jnp.zeros((mod_rows - B - PAIR, d), F32)], axis=0)
    mod = _modulation(c_all, w_mod, b_mod)
    tabs = _rope_tables(n_lat)
    row2 = lambda v: v.astype(F32)[None, :]

    for l in range(depth):
        ctx_out = l < depth - 1
        mod3 = mod[l][:, None, :]
        w_pad = _prep_in_weights(w_in[l])
        wqa, wqb, wk, wv = _prep_mla_weights(w_uq[l], w_ukv[l])
        (Q, K, V, nq, nk, nv, mq, mk, mv, mo, mg, bg) = _inproj(
            X, mod3, row2(g_pre1[l]), w_pad, row2(g_q[l]), row2(g_kv[l]), wqa, wqb, wk, wv, tabs, n_lat)
        a = jnp.concatenate([_mla_attention(Q, K, V, n_lat, False), _mla_attention(Q, K, V, n_lat, True)], axis=1)
        bn = _neighbourhood_attention(nq, nk, nv, _na_bias_table(rpb[l], rows), n_lat, True)
        mgt = jnp.transpose(mg[:, :, :4 * ML_HEADS], (0, 2, 1))
        hf, hb = _mlstm(mq, mk, mv, mg, mgt, b_ml_gates[l], n_lat)
        wr = _pad_cols(w_router[l].astype(F32), 0, LANES)
        wrh = wr.astype(BF16)
        wrl = (wr - wrh.astype(F32)).astype(BF16)
        half = lambda w: (0.5 * w).astype(BF16)
        X, h2, aff = _merge(a, bn, hf, hb, mo, bg, X, mod3, row2(0.5 * g_ml[l]), row2(g_post1[l]), row2(g_pre2[l]),
                            half(w_br_mla[l]), half(w_br_na[l]), half(w_br_ml[l]),
                            w_out[l].astype(BF16), wrh, wrl, n_lat, True)
        X = _moe(X, h2, aff, mod3, row2(g_post2[l]), w_e1, w_e3, w_e2, l, n_lat, ctx_out, in_place=ctx_out)
    return X
```
